```python
import jax
import jax.numpy as jnp
from jax import lax
import numpy as np

D_MODEL = 4096
BATCH = 2
SEQ = 4096
DEPTH = 2

CTX_LEN = 256
GRID_W = 64
EPS = 1e-6

A_HEADS = 16
A_KV_HEADS = 4
A_GROUP = A_HEADS // A_KV_HEADS
A_HEAD_DIM = 128
WINDOW = 128
WBLK = WINDOW
ROPE_BASE = 10000.0

B_HEADS = 8
B_QK_DIM = 128
B_V_DIM = 256
B_CONV_W = 5
B_CHUNK = 64
GATE_CAP = 15.0

C_HEAD_DIM = 128
C_HEADS = D_MODEL // C_HEAD_DIM
C_CHUNK = 32

N_EXPERTS = 64
TOP_K = 8
N_GROUPS = 8
TOPK_GROUPS = 4
D_EXPERT = 256
D_SHARED = 1024
ROUTED_SCALE = 2.5

N_EVEN = (DEPTH + 1) // 2
N_ODD = DEPTH // 2

A_Q = A_HEADS * A_HEAD_DIM
A_KV = A_KV_HEADS * A_HEAD_DIM
B_QK = B_HEADS * B_QK_DIM
B_V = B_HEADS * B_V_DIM
AB_IN = A_Q + 2 * A_KV + 2 * B_QK + 2 * B_V + 4 * B_HEADS
AB_SPLITS = (A_Q, A_Q + A_KV, A_Q + 2 * A_KV, A_Q + 2 * A_KV + B_QK, A_Q + 2 * A_KV + 2 * B_QK,
             A_Q + 2 * A_KV + 2 * B_QK + B_V, A_Q + 2 * A_KV + 2 * B_QK + 2 * B_V)
C_IN = 5 * D_MODEL

kernel_name = 'hybrid_dit_swa_mlstm_hgrn2_moe'


def rms_norm(x, g):
    xf = x.astype(jnp.float32)
    y = xf * lax.rsqrt(jnp.mean(xf * xf, axis=-1, keepdims=True) + EPS)
    return (y * g.astype(jnp.float32)).astype(x.dtype)


def modulate(x, g, shift, scale):
    return rms_norm(x, g) * (1 + scale) + shift


def axial_rope(rows, head_dim):
    pos_r = jnp.repeat(jnp.arange(rows), GRID_W).astype(jnp.float32)
    pos_c = jnp.tile(jnp.arange(GRID_W), rows).astype(jnp.float32)
    n = head_dim // 4
    inv = ROPE_BASE ** (-jnp.arange(n, dtype=jnp.float32) / n)
    ang = jnp.concatenate([pos_r[:, None] * inv, pos_c[:, None] * inv], axis=-1)
    return jnp.cos(ang), jnp.sin(ang)


def apply_rope(x, cos, sin):
    x1, x2 = jnp.split(x.astype(jnp.float32), 2, axis=-1)
    c, s = cos[:, None, :], sin[:, None, :]
    return jnp.concatenate([x1 * c - x2 * s, x1 * s + x2 * c], axis=-1).astype(x.dtype)


def centred_dwconv(x, w):
    return lax.conv_general_dilated(x, w[:, None, :].astype(x.dtype), window_strides=(1,), padding='SAME',
                                    dimension_numbers=('NWC', 'WIO', 'NWC'), feature_group_count=x.shape[-1])


def window_attention_latent(q, k, v, kc, vc, sink):
    B, L = q.shape[0], q.shape[1]
    M = kc.shape[1]
    nb = L // WBLK
    scale = A_HEAD_DIM ** -0.5
    qb = q.reshape(B, nb, WBLK, A_KV_HEADS, A_GROUP, A_HEAD_DIM)

    def band(t):
        tp = jnp.pad(t, ((0, 0), (WBLK, WBLK), (0, 0), (0, 0))).reshape(B, nb + 2, WBLK, A_KV_HEADS, A_HEAD_DIM)
        return jnp.concatenate([tp[:, :-2], tp[:, 1:-1], tp[:, 2:]], axis=2)

    kw, vw = band(k), band(v)
    nw = 3 * WBLK
    s_win = jnp.einsum('bnqhgd,bnwhd->bnhgqw', qb, kw, preferred_element_type=jnp.float32) * scale
    qi = jnp.arange(WBLK)[:, None]
    wi = jnp.arange(nw)[None, :]
    key_pos = (jnp.arange(nb)[:, None, None] - 1) * WBLK + wi[None]
    valid = (jnp.abs(wi - WBLK - qi) <= WINDOW)[None] & (key_pos >= 0) & (key_pos < L)
    s_win = jnp.where(valid[None, :, None, None], s_win, -jnp.inf)
    s_ctx = jnp.einsum('bnqhgd,bmhd->bnhgqm', qb, kc, preferred_element_type=jnp.float32) * scale
    s_sink = jnp.broadcast_to(sink.astype(jnp.float32).reshape(1, 1, A_KV_HEADS, A_GROUP, 1, 1),
                              s_win.shape[:-1] + (1,))
    p = jax.nn.softmax(jnp.concatenate([s_win, s_ctx, s_sink], axis=-1), axis=-1)
    o = (jnp.einsum('bnhgqw,bnwhd->bnqhgd', p[..., :nw].astype(v.dtype), vw)
         + jnp.einsum('bnhgqm,bmhd->bnqhgd', p[..., nw:nw + M].astype(v.dtype), vc))
    return o.reshape(B, L, A_Q)


def context_attention(qc, kc, vc, sink):
    B, M = qc.shape[0], qc.shape[1]
    qg = qc.reshape(B, M, A_KV_HEADS, A_GROUP, A_HEAD_DIM)
    s = jnp.einsum('bqhgd,bmhd->bhgqm', qg, kc, preferred_element_type=jnp.float32) * (A_HEAD_DIM ** -0.5)
    s_sink = jnp.broadcast_to(sink.astype(jnp.float32).reshape(1, A_KV_HEADS, A_GROUP, 1, 1), s.shape[:-1] + (1,))
    p = jax.nn.softmax(jnp.concatenate([s, s_sink], axis=-1), axis=-1)
    o = jnp.einsum('bhgqm,bmhd->bqhgd', p[..., :M].astype(vc.dtype), vc)
    return o.reshape(B, M, A_Q)


def to_chunks(t, ch):
    B, T, H = t.shape[0], t.shape[1], t.shape[2]
    t = t.reshape((B, T // ch, ch, H) + t.shape[3:])
    return jnp.moveaxis(t, (1, 3), (0, 2))


def from_chunks(t):
    t = jnp.moveaxis(t, (0, 2), (1, 3))
    return t.reshape((t.shape[0], t.shape[1] * t.shape[2]) + t.shape[3:])


def mlstm_scan(q, k, v, ig, lf, state):
    xs = tuple(to_chunks(t.astype(jnp.float32), B_CHUNK) for t in (q, k, v, ig, lf))
    tril = jnp.tril(jnp.ones((B_CHUNK, B_CHUNK), dtype=bool))

    def step(carry, inp):
        C, n, m = carry
        qc, kc, vc, ic, fc = inp
        b = jnp.cumsum(fc, axis=-1)
        d_in = jnp.where(tril, b[..., :, None] - b[..., None, :] + ic[..., None, :], -jnp.inf)
        d_st = b + m[..., None]
        m_t = jnp.maximum(d_st, jnp.max(d_in, axis=-1))
        s = jnp.exp(d_in - m_t[..., None]) * jnp.einsum('bhtd,bhsd->bhts', qc, kc)
        w_st = jnp.exp(d_st - m_t)
        num = jnp.einsum('bhts,bhsv->bhtv', s, vc) + w_st[..., None] * jnp.einsum('bhvd,bhtd->bhtv', C, qc)
        den = jnp.sum(s, axis=-1) + w_st * jnp.einsum('bhd,bhtd->bht', n, qc)
        h = num / jnp.maximum(jnp.abs(den), jnp.exp(-m_t))[..., None]
        b_end = b[..., -1]
        d_up = b_end[..., None] - b + ic
        m_new = jnp.maximum(b_end + m, jnp.max(d_up, axis=-1))
        w_up = jnp.exp(d_up - m_new[..., None])
        a = jnp.exp(b_end + m - m_new)
        C = a[..., None, None] * C + jnp.einsum('bhs,bhsv,bhsd->bhvd', w_up, vc, kc)
        n = a[..., None] * n + jnp.einsum('bhs,bhsd->bhd', w_up, kc)
        return (C, n, m_new), h

    state, h = lax.scan(step, state, xs)
    return state, from_chunks(h)


def hgrn2_scan(q, k, v, lf, state):
    xs = tuple(to_chunks(t.astype(jnp.float32), C_CHUNK) for t in (q, k, v, lf))
    tril = jnp.tril(jnp.ones((C_CHUNK, C_CHUNK), dtype=bool))

    def step(S, inp):
        qc, kc, vc, fc = inp
        A = jnp.cumsum(fc, axis=2)
        decay = jnp.exp(jnp.where(tril[..., None], A[:, :, :, None, :] - A[:, :, None, :, :], -jnp.inf))
        att = jnp.einsum('bhtk,bhsk,bhtsk->bhts', qc, kc, decay)
        o = jnp.einsum('bhts,bhsv->bhtv', att, vc) + jnp.einsum('bhtk,bhkv->bhtv', qc * jnp.exp(A), S)
        A_end = A[:, :, -1:, :]
        S = jnp.exp(A_end[:, :, 0])[..., None] * S + jnp.einsum('bhsk,bhsv->bhkv', kc * jnp.exp(A_end - A), vc)
        return S, o

    state, o = lax.scan(step, state, xs)
    return state, from_chunks(o)


def two_segment_scan(scan_fn, ctx_seqs, lat_seqs, state0, reverse):
    orient = (lambda t: jnp.flip(t, axis=1)) if reverse else (lambda t: t)
    state, out_c = scan_fn(*[orient(t) for t in ctx_seqs], state0)
    _, out_l = scan_fn(*[orient(t) for t in lat_seqs], state)
    return orient(out_c), orient(out_l)


def mixer_ab(hc, h, cos, sin, w_in, w_out, q_norm_g, k_norm_g, sink, conv_w, gate_b, norm_g, with_ctx_out):
    B, M = hc.shape[0], hc.shape[1]
    T = M + h.shape[1]
    proj = jnp.concatenate([hc, h], axis=1) @ w_in
    aq, ak, av, bq, bk, bv, bo, bg = jnp.split(proj, AB_SPLITS, axis=-1)
    aq = rms_norm(aq.reshape(B, T, A_HEADS, A_HEAD_DIM), q_norm_g)
    ak = rms_norm(ak.reshape(B, T, A_KV_HEADS, A_HEAD_DIM), k_norm_g)
    av = av.reshape(B, T, A_KV_HEADS, A_HEAD_DIM)
    ya = window_attention_latent(apply_rope(aq[:, M:], cos, sin), apply_rope(ak[:, M:], cos, sin),
                                 av[:, M:], ak[:, :M], av[:, :M], sink)
    bqk = jnp.concatenate([bq, bk], axis=-1)
    bqk = jax.nn.silu(jnp.concatenate([centred_dwconv(bqk[:, :M], conv_w), centred_dwconv(bqk[:, M:], conv_w)], axis=1))
    bq = bqk[..., :B_QK].reshape(B, T, B_HEADS, B_QK_DIM)
    bk = bqk[..., B_QK:].reshape(B, T, B_HEADS, B_QK_DIM) * (B_QK_DIM ** -0.5)
    bv = bv.reshape(B, T, B_HEADS, B_V_DIM)
    gates = bg.astype(jnp.float32) + gate_b.astype(jnp.float32)
    gates = GATE_CAP * jnp.tanh(gates / GATE_CAP)
    i_fw, f_fw, i_bw, f_bw = jnp.split(gates, 4, axis=-1)
    outs = []
    for ig, fg, rev in ((i_fw, f_fw, False), (i_bw, f_bw, True)):
        seqs = (bq, bk, bv, ig, jax.nn.log_sigmoid(fg))
        state0 = (jnp.zeros((B, B_HEADS, B_V_DIM, B_QK_DIM), jnp.float32),
                  jnp.zeros((B, B_HEADS, B_QK_DIM), jnp.float32),
                  jnp.zeros((B, B_HEADS), jnp.float32))
        oc, ol = two_segment_scan(mlstm_scan, [t[:, :M] for t in seqs], [t[:, M:] for t in seqs], state0, rev)
        outs.append(jnp.concatenate([oc, ol], axis=1))
    hb = outs[0] + outs[1]
    yb = (rms_norm(hb, norm_g.reshape(B_HEADS, B_V_DIM)).reshape(B, T, B_V)
          * jax.nn.sigmoid(bo.astype(jnp.float32))).astype(h.dtype)
    y = jnp.concatenate([ya, yb[:, M:]], axis=-1) @ w_out
    if not with_ctx_out:
        return None, y
    yac = context_attention(aq[:, :M], ak[:, :M], av[:, :M], sink)
    yc = jnp.concatenate([yac, yb[:, :M]], axis=-1) @ w_out
    return yc, y


def mixer_c(hc, h, w_in, w_out, lb, norm_g, with_ctx_out):
    B, M = hc.shape[0], hc.shape[1]
    T = M + h.shape[1]
    proj = jnp.concatenate([hc, h], axis=1) @ w_in
    q, i, g, f_fw, f_bw = jnp.split(proj, 5, axis=-1)
    shp = (B, T, C_HEADS, C_HEAD_DIM)
    q = jax.nn.silu(q).reshape(shp)
    i = i.reshape(shp)
    outs = []
    for fraw, lbd, rev in ((f_fw, lb[0], False), (f_bw, lb[1], True)):
        f = lbd + (1.0 - lbd) * jax.nn.sigmoid(fraw.astype(jnp.float32))
        seqs = (q, (1.0 - f).reshape(shp), i, jnp.log(f).reshape(shp))
        state0 = jnp.zeros((B, C_HEADS, C_HEAD_DIM, C_HEAD_DIM), jnp.float32)
        oc, ol = two_segment_scan(hgrn2_scan, [t[:, :M] for t in seqs], [t[:, M:] for t in seqs], state0, rev)
        outs.append(jnp.concatenate([oc, ol], axis=1))
    o = outs[0] + outs[1]
    y_all = (rms_norm(o, norm_g.reshape(C_HEADS, C_HEAD_DIM)).reshape(B, T, D_MODEL)
             * jax.nn.silu(g.astype(jnp.float32))).astype(h.dtype)
    y = y_all[:, M:] @ w_out
    if not with_ctx_out:
        return None, y
    return y_all[:, :M] @ w_out, y


def moe(h, router_w, router_b, w_gate, w_up, w_down, s_gate, s_up, s_down):
    shp = h.shape
    t = h.reshape(-1, shp[-1])
    N = t.shape[0]
    scores = jax.nn.sigmoid(jnp.dot(t, router_w, preferred_element_type=jnp.float32))
    sel = scores + router_b.astype(jnp.float32)
    grp = sel.reshape(N, N_GROUPS, N_EXPERTS // N_GROUPS)
    grp_score = jnp.sum(lax.top_k(grp, 2)[0], axis=-1)
    _, gidx = lax.top_k(grp_score, TOPK_GROUPS)
    gmask = jnp.sum(jax.nn.one_hot(gidx, N_GROUPS, dtype=jnp.float32), axis=1)
    emask = jnp.repeat(gmask, N_EXPERTS // N_GROUPS, axis=1) > 0
    _, eidx = lax.top_k(jnp.where(emask, sel, -jnp.inf), TOP_K)
    w = jnp.take_along_axis(scores, eidx, axis=-1)
    w = ROUTED_SCALE * w / jnp.sum(w, axis=-1, keepdims=True)
    gates = jnp.einsum('nk,nke->en', w, jax.nn.one_hot(eidx, N_EXPERTS, dtype=jnp.float32)).astype(t.dtype)

    def expert(acc, p):
        wg, wu, wd, ge = p
        hid = jax.nn.silu(t @ wg) * (t @ wu)
        return acc + ge[:, None] * (hid @ wd), None

    routed, _ = lax.scan(expert, jnp.zeros_like(t), (w_gate, w_up, w_down, gates))
    shared = (jax.nn.silu(t @ s_gate) * (t @ s_up)) @ s_down
    return (routed + shared).reshape(shp)


def setup_inputs(seed: int = 0) -> dict:
    key = jax.random.key(seed)
    keys = jax.random.split(key, 40)
    counter = [0]

    def nrm(shape, scale):
        k = keys[counter[0]]
        counter[0] += 1
        return jax.random.normal(k, shape, jnp.float32) * scale

    D = D_MODEL
    f_bias = jnp.linspace(3.0, 6.0, B_HEADS, dtype=jnp.float32)[None, :]
    b_gate_b = jnp.concatenate([nrm((N_EVEN, B_HEADS), 0.1), f_bias + nrm((N_EVEN, B_HEADS), 0.1),
                                nrm((N_EVEN, B_HEADS), 0.1), f_bias + nrm((N_EVEN, B_HEADS), 0.1)], axis=-1)
    return {
        'x': nrm((BATCH, SEQ, D), 1.0),
        'c': nrm((BATCH, D), 1.0),
        'ctx': nrm((BATCH, CTX_LEN, D), 1.0),
        'c_ctx': nrm((D,), 1.0),
        'ada_w': nrm((DEPTH, D, 6 * D), 0.5 * D ** -0.5),
        'ada_b': nrm((DEPTH, 6 * D), 0.02),
        'norm_mix_g': 1.0 + nrm((DEPTH, D), 0.02),
        'norm_ffn_g': 1.0 + nrm((DEPTH, D), 0.02),
        'ab_w_in': nrm((N_EVEN, D, AB_IN), D ** -0.5),
        'ab_w_out': nrm((N_EVEN, A_Q + B_V, D), (A_Q + B_V) ** -0.5),
        'a_q_norm_g': 1.0 + nrm((N_EVEN, A_HEAD_DIM), 0.02),
        'a_k_norm_g': 1.0 + nrm((N_EVEN, A_HEAD_DIM), 0.02),
        'a_sink': nrm((N_EVEN, A_HEADS), 0.5),
        'b_conv_w': nrm((N_EVEN, B_CONV_W, 2 * B_QK), B_CONV_W ** -0.5),
        'b_gate_b': b_gate_b,
        'b_norm_g': 1.0 + nrm((N_EVEN, B_V), 0.02),
        'c_w_in': nrm((N_ODD, D, C_IN), D ** -0.5),
        'c_w_out': nrm((N_ODD, D, D), D ** -0.5),
        'c_lb_logits': nrm((DEPTH, 2, D), 0.1),
        'c_norm_g': 1.0 + nrm((N_ODD, D), 0.02),
        'router_w': nrm((DEPTH, D, N_EXPERTS), D ** -0.5),
        'router_b': nrm((DEPTH, N_EXPERTS), 0.01),
        'exp_w_gate': nrm((DEPTH, N_EXPERTS, D, D_EXPERT), D ** -0.5),
        'exp_w_up': nrm((DEPTH, N_EXPERTS, D, D_EXPERT), D ** -0.5),
        'exp_w_down': nrm((DEPTH, N_EXPERTS, D_EXPERT, D), D_EXPERT ** -0.5),
        'shared_w_gate': nrm((DEPTH, D, D_SHARED), D ** -0.5),
        'shared_w_up': nrm((DEPTH, D, D_SHARED), D ** -0.5),
        'shared_w_down': nrm((DEPTH, D_SHARED, D), D_SHARED ** -0.5),
    }


def reference(x, c, ctx, c_ctx, ada_w, ada_b, norm_mix_g, norm_ffn_g, ab_w_in, ab_w_out, a_q_norm_g, a_k_norm_g,
              a_sink, b_conv_w, b_gate_b, b_norm_g, c_w_in, c_w_out, c_lb_logits, c_norm_g, router_w, router_b,
              exp_w_gate, exp_w_up, exp_w_down, shared_w_gate, shared_w_up, shared_w_down):
    L = x.shape[1]
    M = ctx.shape[1]
    ROWS = L // GRID_W
    cos, sin = axial_rope(ROWS, A_HEAD_DIM)
    lb_soft = jax.nn.softmax(c_lb_logits.astype(jnp.float32), axis=0)
    lower_bounds = jnp.cumsum(lb_soft, axis=0) - lb_soft[0:1]
    sc = jax.nn.silu(c)
    scc = jax.nn.silu(c_ctx)
    xc = ctx
    for l in range(DEPTH):
        last = l == DEPTH - 1
        mod = sc @ ada_w[l] + ada_b[l]
        mod_c = scc @ ada_w[l] + ada_b[l]
        sh1, sc1, g1, sh2, sc2, g2 = jnp.split(mod[:, None, :], 6, axis=-1)
        sh1c, sc1c, g1c, sh2c, sc2c, g2c = jnp.split(mod_c, 6, axis=-1)
        h = modulate(x, norm_mix_g[l], sh1, sc1)
        hc = modulate(xc, norm_mix_g[l], sh1c, sc1c)
        if l % 2 == 0:
            e = l // 2
            yc, y = mixer_ab(hc, h, cos, sin, ab_w_in[e], ab_w_out[e], a_q_norm_g[e], a_k_norm_g[e], a_sink[e],
                             b_conv_w[e], b_gate_b[e], b_norm_g[e], not last)
        else:
            o = l // 2
            yc, y = mixer_c(hc, h, c_w_in[o], c_w_out[o], lower_bounds[l], c_norm_g[o], not last)
        x = x + g1 * y
        h2 = modulate(x, norm_ffn_g[l], sh2, sc2)
        if last:
            x = x + g2 * moe(h2, router_w[l], router_b[l], exp_w_gate[l], exp_w_up[l], exp_w_down[l],
                             shared_w_gate[l], shared_w_up[l], shared_w_down[l])
        else:
            xc = xc + g1c * yc
            h2c = modulate(xc, norm_ffn_g[l], sh2c, sc2c)
            both = moe(jnp.concatenate([h2c, h2], axis=1), router_w[l], router_b[l], exp_w_gate[l], exp_w_up[l],
                       exp_w_down[l], shared_w_gate[l], shared_w_up[l], shared_w_down[l])
            xc = xc + g2c * both[:, :M]
            x = x + g2 * both[:, M:]
    return x
```

```python
import functools

import jax
import jax.numpy as jnp
from jax import lax
from jax.experimental import pallas as pl
from jax.experimental.pallas import tpu as pltpu

F32 = jnp.float32
BF16 = jnp.bfloat16
NEG_INF = float("-inf")

EPS = 1e-6
GRID_W = 64
ROPE_BASE = 10000.0

A_HEADS = 16
A_KV_HEADS = 4
A_GROUP = A_HEADS // A_KV_HEADS
A_HEAD_DIM = 128
WINDOW = 128
A_Q = A_HEADS * A_HEAD_DIM
A_KV = A_KV_HEADS * A_HEAD_DIM

B_HEADS = 8
B_QK_DIM = 128
B_V_DIM = 256
B_CONV_W = 5
GATE_CAP = 15.0
B_QK = B_HEADS * B_QK_DIM
B_V = B_HEADS * B_V_DIM
AB_MAIN = A_Q + 2 * A_KV + 2 * B_QK + 2 * B_V
AB_GATES = 4 * B_HEADS

C_HEAD_DIM = 128

N_EXPERTS = 64
TOP_K = 8
N_GROUPS = 8
TOPK_GROUPS = 4
GROUP_SIZE = N_EXPERTS // N_GROUPS
ROUTED_SCALE = 2.5

VMEM_LIMIT_BYTES = 56 * 1024 * 1024
LANE_BLOCK = 128
ROW_TILE = 256
COL_TILE = 512
MLSTM_CHUNK = 256
HGRN_BLOCK = 256
HGRN_CHUNK = 32
HGRN_HEADS = 2
MOE_TILE = 256
COMBINE_TOKENS = 32

NT_DIMS = (((1,), (1,)), ((), ()))
TN_DIMS = (((0,), (0,)), ((), ()))


def _cp(*sem):
    return pltpu.CompilerParams(dimension_semantics=sem, vmem_limit_bytes=VMEM_LIMIT_BYTES)


def _sigmoid(x):
    return jax.nn.sigmoid(x)


def _silu(x):
    return x * jax.nn.sigmoid(x)


def _rms(x, g):
    return x * lax.rsqrt(jnp.mean(x * x, axis=-1, keepdims=True) + EPS) * g


def _row_tile_size(rows, limit):
    best = 16
    for t in range(16, limit + 1, 16):
        if rows % t == 0:
            best = t
    return best


def _ada_body(a_ref, w_ref, b_ref, o_ref):
    o_ref[...] = jnp.dot(a_ref[...], w_ref[...].astype(BF16), preferred_element_type=F32) + b_ref[...]


def _ada(a, ada_w, ada_b, layer):
    depth, d, d6 = ada_w.shape
    rows = a.shape[0]
    return pl.pallas_call(
        _ada_body,
        grid=(d6 // COL_TILE,),
        in_specs=[pl.BlockSpec((rows, d), lambda j: (0, 0)),
                  pl.BlockSpec((None, d, COL_TILE), lambda j: (layer, 0, j)),
                  pl.BlockSpec((None, 1, COL_TILE), lambda j: (layer, 0, j))],
        out_specs=pl.BlockSpec((rows, COL_TILE), lambda j: (0, j)),
        out_shape=jax.ShapeDtypeStruct((rows, d6), F32),
        compiler_params=_cp("arbitrary"),
        name=f"ada{layer}",
    )(a, ada_w, ada_b.reshape(depth, 1, d6))


def _modulate_body(x_ref, g_ref, mod_ref, o_ref, *, shift_row, scale_row):
    y = _rms(x_ref[...], g_ref[...])
    o_ref[...] = (y * (1.0 + mod_ref[scale_row:scale_row + 1, :])
                  + mod_ref[shift_row:shift_row + 1, :]).astype(o_ref.dtype)


def _modulate(x, g, mod, sel, *, shift_row, scale_row, name):
    rows, d = x.shape
    return pl.pallas_call(
        functools.partial(_modulate_body, shift_row=shift_row, scale_row=scale_row),
        grid=(rows // ROW_TILE,),
        in_specs=[pl.BlockSpec((ROW_TILE, d), lambda i: (i, 0)),
                  pl.BlockSpec((1, d), lambda i: (0, 0)),
                  pl.BlockSpec((None, 6, d), lambda i: (sel(i), 0, 0))],
        out_specs=pl.BlockSpec((ROW_TILE, d), lambda i: (i, 0)),
        out_shape=jax.ShapeDtypeStruct((rows, d), BF16),
        compiler_params=_cp("arbitrary"),
        name=name,
    )(x, g, mod)


def _linear_body(*refs, n_a, gate_row, has_extra):
    a_refs = refs[:n_a]
    w_refs = refs[n_a:2 * n_a]
    p = 2 * n_a
    x_ref = mod_ref = e_ref = None
    if gate_row is not None:
        x_ref, mod_ref = refs[p], refs[p + 1]
        p += 2
        if has_extra:
            e_ref = refs[p]
            p += 1
    o_ref = refs[p]
    wb_refs = refs[p + 1:]

    @pl.when(pl.program_id(1) == 0)
    def _():
        for w_ref, wb in zip(w_refs, wb_refs):
            wb[...] = w_ref[...].astype(BF16)

    acc = jnp.dot(a_refs[0][...], wb_refs[0][...], preferred_element_type=F32)
    for a_ref, wb in zip(a_refs[1:], wb_refs[1:]):
        acc = acc + jnp.dot(a_ref[...], wb[...], preferred_element_type=F32)
    if gate_row is not None:
        if e_ref is not None:
            acc = acc + e_ref[...]
        acc = x_ref[...] + mod_ref[gate_row:gate_row + 1, :] * acc
    o_ref[...] = acc.astype(o_ref.dtype)


def _linear(a_list, w, layer, *, n_cols, tm, out_dtype, name, res=None):
    n_a = len(a_list)
    rows = a_list[0].shape[0]
    k_each = w.shape[1] // n_a
    tn = min(COL_TILE, n_cols)
    in_specs = [pl.BlockSpec((tm, k_each), lambda j, i: (i, 0)) for _ in a_list]
    for idx in range(n_a):
        in_specs.append(pl.BlockSpec((None, k_each, tn), lambda j, i, idx=idx: (layer, idx, j)))
    args = list(a_list) + [w] * n_a
    gate_row, has_extra = None, False
    if res is not None:
        x, mod, gate_row, extra, sel = res
        d = mod.shape[-1]
        in_specs.append(pl.BlockSpec((tm, tn), lambda j, i: (i, j)))
        in_specs.append(pl.BlockSpec((None, 6, tn), lambda j, i: (sel(i), 0, j)))
        args += [x, mod]
        if extra is not None:
            has_extra = True
            in_specs.append(pl.BlockSpec((tm, tn), lambda j, i: (i, j)))
            args.append(extra)
    return pl.pallas_call(
        functools.partial(_linear_body, n_a=n_a, gate_row=gate_row, has_extra=has_extra),
        grid=(n_cols // tn, rows // tm),
        in_specs=in_specs,
        out_specs=pl.BlockSpec((tm, tn), lambda j, i: (i, j)),
        out_shape=jax.ShapeDtypeStruct((rows, n_cols), out_dtype),
        scratch_shapes=[pltpu.VMEM((k_each, tn), BF16) for _ in range(n_a)],
        compiler_params=_cp("arbitrary", "arbitrary"),
        name=name,
    )(*args)


def _glu_body(a_ref, wg_ref, wu_ref, o_ref, wgb, wub):
    @pl.when(pl.program_id(1) == 0)
    def _():
        wgb[...] = wg_ref[...].astype(BF16)
        wub[...] = wu_ref[...].astype(BF16)

    a = a_ref[...]
    g = jnp.dot(a, wgb[...], preferred_element_type=F32)
    u = jnp.dot(a, wub[...], preferred_element_type=F32)
    o_ref[...] = (_silu(g) * u).astype(o_ref.dtype)


def _glu(a, wg, wu, layer, *, tm, name):
    rows, k = a.shape
    n_cols = wg.shape[2]
    tn = min(COL_TILE // 2, n_cols)
    wspec = pl.BlockSpec((None, k, tn), lambda j, i: (layer, 0, j))
    return pl.pallas_call(
        _glu_body,
        grid=(n_cols // tn, rows // tm),
        in_specs=[pl.BlockSpec((tm, k), lambda j, i: (i, 0)), wspec, wspec],
        out_specs=pl.BlockSpec((tm, tn), lambda j, i: (i, j)),
        out_shape=jax.ShapeDtypeStruct((rows, n_cols), BF16),
        scratch_shapes=[pltpu.VMEM((k, tn), BF16), pltpu.VMEM((k, tn), BF16)],
        compiler_params=_cp("arbitrary", "arbitrary"),
        name=name,
    )(a, wg, wu)


def _attn_prep_body(p_ref, cos_ref, sin_ref, qg_ref, kg_ref, q_ref, k_ref, v_ref):
    cos = cos_ref[...]
    sin = sin_ref[...]

    def norm_rope(xh, g):
        y = _rms(xh, g)
        return y * cos + pltpu.roll(y, A_HEAD_DIM // 2, axis=1) * sin

    for hd in range(A_HEADS):
        sl = slice(hd * A_HEAD_DIM, (hd + 1) * A_HEAD_DIM)
        q_ref[:, sl] = norm_rope(p_ref[:, sl], qg_ref[...]).astype(BF16)
    for hd in range(A_KV_HEADS):
        sl = slice(hd * A_HEAD_DIM, (hd + 1) * A_HEAD_DIM)
        k_ref[:, sl] = norm_rope(p_ref[:, A_Q + hd * A_HEAD_DIM:A_Q + (hd + 1) * A_HEAD_DIM], kg_ref[...]).astype(BF16)
    v_ref[...] = p_ref[:, A_Q + A_KV:A_Q + 2 * A_KV].astype(BF16)


def _attn_prep(proj, cos_t, sin_t, qg, kg, tiles_per_sample):
    rows = proj.shape[0]
    width = A_Q + 2 * A_KV
    return pl.pallas_call(
        _attn_prep_body,
        grid=(rows // ROW_TILE,),
        in_specs=[pl.BlockSpec((ROW_TILE, width), lambda i: (i, 0)),
                  pl.BlockSpec((ROW_TILE, A_HEAD_DIM), lambda i: (i % tiles_per_sample, 0)),
                  pl.BlockSpec((ROW_TILE, A_HEAD_DIM), lambda i: (i % tiles_per_sample, 0)),
                  pl.BlockSpec((1, A_HEAD_DIM), lambda i: (0, 0)),
                  pl.BlockSpec((1, A_HEAD_DIM), lambda i: (0, 0))],
        out_specs=[pl.BlockSpec((ROW_TILE, A_Q), lambda i: (i, 0)),
                   pl.BlockSpec((ROW_TILE, A_KV), lambda i: (i, 0)),
                   pl.BlockSpec((ROW_TILE, A_KV), lambda i: (i, 0))],
        out_shape=[jax.ShapeDtypeStruct((rows, A_Q), BF16),
                   jax.ShapeDtypeStruct((rows, A_KV), BF16),
                   jax.ShapeDtypeStruct((rows, A_KV), BF16)],
        compiler_params=_cp("arbitrary"),
        name="attn_prep",
    )(proj, cos_t, sin_t, qg, kg)


def _attn_body(sink_ref, q_ref, kp_ref, ko_ref, kn_ref, kc_ref, vp_ref, vo_ref, vn_ref, vc_ref, o_ref,
               *, ctx_blocks, lat_len):
    h = pl.program_id(1)
    j = pl.program_id(2)
    n = j - ctx_blocks
    nw = 3 * WINDOW
    m_ctx = kc_ref.shape[0]
    kw = jnp.concatenate([kp_ref[...], ko_ref[...], kn_ref[...], kc_ref[...]], axis=0)
    vw = jnp.concatenate([vp_ref[...], vo_ref[...], vn_ref[...], vc_ref[...]], axis=0)
    qi = lax.broadcasted_iota(jnp.int32, (WINDOW, nw + m_ctx), 0)
    wi = lax.broadcasted_iota(jnp.int32, (WINDOW, nw + m_ctx), 1)
    key_pos = (n - 1) * WINDOW + wi
    in_win = (jnp.abs(wi - WINDOW - qi) <= WINDOW) & (key_pos >= 0) & (key_pos < lat_len) & (n >= 0)
    valid = (wi >= nw) | in_win
    bias = jnp.where(valid, 0.0, NEG_INF).astype(F32)
    scale = A_HEAD_DIM ** -0.5
    for g in range(A_GROUP):
        sl = slice(g * A_HEAD_DIM, (g + 1) * A_HEAD_DIM)
        s = lax.dot_general(q_ref[:, sl], kw, NT_DIMS, preferred_element_type=F32) * scale + bias
        sink = sink_ref[h * A_GROUP + g]
        m = jnp.maximum(jnp.max(s, axis=-1, keepdims=True), sink)
        p = jnp.exp(s - m)
        denom = jnp.sum(p, axis=-1, keepdims=True) + jnp.exp(sink - m)
        o = jnp.dot(p.astype(BF16), vw, preferred_element_type=F32)
        o_ref[:, sl] = (o / denom).astype(o_ref.dtype)


def _attention(qn, kn, vb, sink, *, batch, seq_all, ctx_len):
    blocks = seq_all // WINDOW
    ctx_blocks = ctx_len // WINDOW
    last = blocks - 1
    qw = A_GROUP * A_HEAD_DIM

    def kv_spec(shift):
        def imap(b, h, j):
            return (b * blocks + jnp.clip(j + shift, ctx_blocks, last), h)
        return pl.BlockSpec((WINDOW, A_HEAD_DIM), imap)

    ctx_spec = pl.BlockSpec((ctx_len, A_HEAD_DIM), lambda b, h, j: (b * (seq_all // ctx_len), h))
    return pl.pallas_call(
        functools.partial(_attn_body, ctx_blocks=ctx_blocks, lat_len=seq_all - ctx_len),
        grid=(batch, A_KV_HEADS, blocks),
        in_specs=[pl.BlockSpec(memory_space=pltpu.SMEM),
                  pl.BlockSpec((WINDOW, qw), lambda b, h, j: (b * blocks + j, h)),
                  kv_spec(-1), kv_spec(0), kv_spec(1), ctx_spec,
                  kv_spec(-1), kv_spec(0), kv_spec(1), ctx_spec],
        out_specs=pl.BlockSpec((WINDOW, qw), lambda b, h, j: (b * blocks + j, h)),
        out_shape=jax.ShapeDtypeStruct((batch * seq_all, A_Q), BF16),
        compiler_params=_cp("arbitrary", "arbitrary", "arbitrary"),
        name="window_attention",
    )(sink, qn, kn, kn, kn, kn, vb, vb, vb, vb)


def _conv_body(x_ref, w_ref, o_ref, *, ctx_len, k_first_block):
    x = x_ref[...]
    t_len = x.shape[0]
    t = lax.broadcasted_iota(jnp.int32, x.shape, 0)
    half = B_CONV_W // 2
    acc = x * w_ref[half:half + 1, :]
    for d in range(-half, half + 1):
        if d == 0:
            continue
        xs = pltpu.roll(x, (-d) % t_len, axis=0)
        u = t + d
        ok = ((t < ctx_len) & (u >= 0) & (u < ctx_len)) | ((t >= ctx_len) & (u >= ctx_len) & (u < t_len))
        acc = acc + jnp.where(ok, xs, 0.0) * w_ref[half + d:half + d + 1, :]
    k_scale = jnp.where(pl.program_id(1) >= k_first_block, B_QK_DIM ** -0.5, 1.0).astype(F32)
    o_ref[...] = (_silu(acc) * k_scale).astype(o_ref.dtype)


def _mlstm_conv(proj, conv_w, *, batch, seq_all, ctx_len):
    first = (A_Q + 2 * A_KV) // LANE_BLOCK
    nblk = 2 * B_QK // LANE_BLOCK
    return pl.pallas_call(
        functools.partial(_conv_body, ctx_len=ctx_len, k_first_block=B_QK // LANE_BLOCK),
        grid=(batch, nblk),
        in_specs=[pl.BlockSpec((seq_all, LANE_BLOCK), lambda b, c: (b, first + c)),
                  pl.BlockSpec((B_CONV_W, LANE_BLOCK), lambda b, c: (0, c))],
        out_specs=pl.BlockSpec((seq_all, LANE_BLOCK), lambda b, c: (b, c)),
        out_shape=jax.ShapeDtypeStruct((batch * seq_all, 2 * B_QK), BF16),
        compiler_params=_cp("arbitrary", "arbitrary"),
        name="mlstm_conv",
    )(proj, conv_w)


def _gates_body(raw_ref, b_ref, o_ref):
    g = raw_ref[...] + b_ref[...]
    g = GATE_CAP * jnp.tanh(g / GATE_CAP)
    lane = lax.broadcasted_iota(jnp.int32, g.shape, 1)
    is_forget = (lane // B_HEADS) % 2 == 1
    log_sig = jnp.minimum(g, 0.0) - jnp.log(1.0 + jnp.exp(-jnp.abs(g)))
    o_ref[...] = jnp.where(is_forget, log_sig, g)


def _mlstm_gates(raw, gate_b):
    rows, width = raw.shape
    tr = _row_tile_size(rows, 2048)
    return pl.pallas_call(
        _gates_body,
        grid=(rows // tr,),
        in_specs=[pl.BlockSpec((tr, width), lambda i: (i, 0)),
                  pl.BlockSpec((1, width), lambda i: (0, 0))],
        out_specs=pl.BlockSpec((tr, width), lambda i: (i, 0)),
        out_shape=jax.ShapeDtypeStruct((rows, width), F32),
        compiler_params=_cp("arbitrary"),
        name="mlstm_gates",
    )(raw, gate_b)


def _mlstm_direction(q_ref, k_ref, v_ref, g_ref, o_ref, st_ref, n_ref, m_ref, d, rev):
    c_len = q_ref.shape[0]
    q = q_ref[...]
    k = k_ref[...]
    v = v_ref[...].astype(BF16)
    ig = g_ref[2 * d:2 * d + 1, :]
    lf = g_ref[2 * d + 1:2 * d + 2, :]
    r = lax.broadcasted_iota(jnp.int32, (c_len, c_len), 0)
    c = lax.broadcasted_iota(jnp.int32, (c_len, c_len), 1)
    eye = r == c
    tri = (c >= r) if rev else (c <= r)
    tri_t = (r >= c) if rev else (r <= c)
    lf_col = jnp.sum(jnp.where(eye, lf, 0.0), axis=1, keepdims=True)
    ig_col = jnp.sum(jnp.where(eye, ig, 0.0), axis=1, keepdims=True)
    b_col = jnp.sum(jnp.where(tri, lf, 0.0), axis=1, keepdims=True)
    b_row = jnp.sum(jnp.where(tri_t, lf_col, 0.0), axis=0, keepdims=True)
    m_prev = m_ref[d]
    d_st = b_col + m_prev
    d_in = jnp.where(tri, b_col - b_row + ig, NEG_INF)
    m_t = jnp.maximum(d_st, jnp.max(d_in, axis=1, keepdims=True))
    s = jnp.exp(d_in - m_t) * lax.dot_general(q, k, NT_DIMS, preferred_element_type=F32)
    w_st = jnp.exp(d_st - m_t)
    st = st_ref[d]
    num = (jnp.dot(s.astype(BF16), v, preferred_element_type=F32)
           + w_st * jnp.dot(q, st.astype(BF16), preferred_element_type=F32))
    qn = jnp.sum(q.astype(F32) * n_ref[d], axis=1, keepdims=True)
    den = jnp.sum(s, axis=1, keepdims=True) + w_st * qn
    o_ref[...] = num / jnp.maximum(jnp.abs(den), jnp.exp(-m_t))
    b_end = jnp.sum(lf, axis=1, keepdims=True)
    d_up = b_end - b_col + ig_col
    m_new = jnp.maximum(b_end + m_prev, jnp.max(d_up, axis=0, keepdims=True))
    w_up = jnp.exp(d_up - m_new)
    a = jnp.exp(b_end + m_prev - m_new)
    kw = k.astype(F32) * w_up
    st_ref[d] = a * st + lax.dot_general(kw.astype(BF16), v, TN_DIMS, preferred_element_type=F32)
    n_ref[d] = a * n_ref[d] + jnp.sum(kw, axis=0, keepdims=True)
    m_ref[d] = m_new


def _mlstm_body(qf, kf, vf, gf, qb, kb, vb, gb, of, ob, st_ref, n_ref, m_ref):
    @pl.when(pl.program_id(2) == 0)
    def _():
        st_ref[...] = jnp.zeros_like(st_ref)
        n_ref[...] = jnp.zeros_like(n_ref)
        m_ref[...] = jnp.zeros_like(m_ref)

    _mlstm_direction(qf, kf, vf, gf, of, st_ref, n_ref, m_ref, 0, False)
    _mlstm_direction(qb, kb, vb, gb, ob, st_ref, n_ref, m_ref, 1, True)


def _scan_orders(n_chunks, ctx_chunks):
    fwd = lambda j: j
    bwd = lambda j: jnp.where(j < ctx_chunks, ctx_chunks - 1 - j, n_chunks - 1 - (j - ctx_chunks))
    return fwd, bwd


def _mlstm_scan(qk, proj, gates_t, *, batch, seq_all, ctx_len):
    ch = MLSTM_CHUNK
    n_chunks = seq_all // ch
    fwd, bwd = _scan_orders(n_chunks, ctx_len // ch)
    v_first = (A_Q + 2 * A_KV + 2 * B_QK) // B_V_DIM

    def specs(order):
        return [pl.BlockSpec((ch, B_QK_DIM), lambda b, h, j: (b * n_chunks + order(j), h)),
                pl.BlockSpec((ch, B_QK_DIM), lambda b, h, j: (b * n_chunks + order(j), B_HEADS + h)),
                pl.BlockSpec((ch, B_V_DIM), lambda b, h, j: (b * n_chunks + order(j), v_first + h)),
                pl.BlockSpec((None, None, 4, ch), lambda b, h, j: (b, h, 0, order(j)))]

    def out_spec(order):
        return pl.BlockSpec((ch, B_V_DIM), lambda b, h, j: (b * n_chunks + order(j), h))

    out = jax.ShapeDtypeStruct((batch * seq_all, B_V), F32)
    return pl.pallas_call(
        _mlstm_body,
        grid=(batch, B_HEADS, n_chunks),
        in_specs=specs(fwd) + specs(bwd),
        out_specs=[out_spec(fwd), out_spec(bwd)],
        out_shape=[out, out],
        scratch_shapes=[pltpu.VMEM((2, B_QK_DIM, B_V_DIM), F32),
                        pltpu.VMEM((2, 1, B_QK_DIM), F32),
                        pltpu.VMEM((2, 1, 1), F32)],
        compiler_params=_cp("arbitrary", "arbitrary", "arbitrary"),
        name="mlstm_scan",
    )(qk, qk, proj, gates_t, qk, qk, proj, gates_t)


def _headnorm_gate_body(a_ref, b_ref, gate_ref, g_ref, o_ref, *, head_dim, gate_fn):
    for hd in range(a_ref.shape[1] // head_dim):
        sl = slice(hd * head_dim, (hd + 1) * head_dim)
        y = _rms(a_ref[:, sl] + b_ref[:, sl], g_ref[:, sl])
        o_ref[:, sl] = (y * gate_fn(gate_ref[:, sl])).astype(o_ref.dtype)


def _headnorm_gate(a, b, proj, gate_col0, norm_g, *, head_dim, gate_fn, name):
    rows, width = a.shape
    tc = min(1024, width)
    return pl.pallas_call(
        functools.partial(_headnorm_gate_body, head_dim=head_dim, gate_fn=gate_fn),
        grid=(rows // ROW_TILE, width // tc),
        in_specs=[pl.BlockSpec((ROW_TILE, tc), lambda i, c: (i, c)),
                  pl.BlockSpec((ROW_TILE, tc), lambda i, c: (i, c)),
                  pl.BlockSpec((ROW_TILE, tc), lambda i, c: (i, gate_col0 // tc + c)),
                  pl.BlockSpec((1, tc), lambda i, c: (0, c))],
        out_specs=pl.BlockSpec((ROW_TILE, tc), lambda i, c: (i, c)),
        out_shape=jax.ShapeDtypeStruct((rows, width), BF16),
        compiler_params=_cp("arbitrary", "arbitrary"),
        name=name,
    )(a, b, proj, norm_g)


def _hgrn_direction(q_ref, i_ref, f_ref, lb_ref, o_ref, st_ref, d, rev):
    rows = q_ref.shape[0]
    ch = HGRN_CHUNK
    lb = lb_ref[d:d + 1, :]
    q = _silu(q_ref[...])
    f = lb + (1.0 - lb) * _sigmoid(f_ref[...])
    kk = 1.0 - f
    a_cum = jnp.log(f)
    pos = lax.broadcasted_iota(jnp.int32, a_cum.shape, 0) % ch
    sh = 1
    while sh < ch:
        if rev:
            a_cum = a_cum + jnp.where(pos < ch - sh, pltpu.roll(a_cum, rows - sh, axis=0), 0.0)
        else:
            a_cum = a_cum + jnp.where(pos >= sh, pltpu.roll(a_cum, sh, axis=0), 0.0)
        sh *= 2
    q_in = (q * jnp.exp(a_cum)).astype(BF16)
    r = lax.broadcasted_iota(jnp.int32, (ch, ch), 0)
    c = lax.broadcasted_iota(jnp.int32, (ch, ch), 1)
    tri = (c >= r) if rev else (c <= r)
    n_chunks = rows // ch
    order = range(n_chunks - 1, -1, -1) if rev else range(n_chunks)
    end_row = 0 if rev else ch - 1
    mid_row = ch // 2
    for ci in order:
        rs = slice(ci * ch, (ci + 1) * ch)
        for hd in range(HGRN_HEADS):
            cs = slice(hd * C_HEAD_DIM, (hd + 1) * C_HEAD_DIM)
            a_c = a_cum[rs, cs]
            a_mid = a_c[mid_row:mid_row + 1, :]
            a_end = a_c[end_row:end_row + 1, :]
            v = i_ref[rs, cs].astype(BF16)
            qh = (q[rs, cs] * jnp.exp(a_c - a_mid)).astype(BF16)
            kh = (kk[rs, cs] * jnp.exp(a_mid - a_c)).astype(BF16)
            att = jnp.where(tri, lax.dot_general(qh, kh, NT_DIMS, preferred_element_type=F32), 0.0)
            st = st_ref[d, hd]
            o = (jnp.dot(att.astype(BF16), v, preferred_element_type=F32)
                 + lax.dot_general(q_in[rs, cs], st.astype(BF16), NT_DIMS, preferred_element_type=F32))
            o_ref[rs, cs] = o
            ke = (kk[rs, cs] * jnp.exp(a_end - a_c)).astype(BF16)
            st_ref[d, hd] = jnp.exp(a_end) * st + lax.dot_general(v, ke, TN_DIMS, preferred_element_type=F32)


def _hgrn_body(qf, vf, ff, qb, vb, fb, lb_ref, of, ob, st_ref):
    @pl.when(pl.program_id(2) == 0)
    def _():
        st_ref[...] = jnp.zeros_like(st_ref)

    _hgrn_direction(qf, vf, ff, lb_ref, of, st_ref, 0, False)
    _hgrn_direction(qb, vb, fb, lb_ref, ob, st_ref, 1, True)


def _hgrn_scan(proj, lb, *, batch, seq_all, ctx_len, d_model):
    blk = HGRN_BLOCK
    n_blocks = seq_all // blk
    fwd, bwd = _scan_orders(n_blocks, ctx_len // blk)
    wcols = HGRN_HEADS * C_HEAD_DIM
    per = d_model // wcols

    def spec(order, part):
        return pl.BlockSpec((blk, wcols), lambda b, h, j: (b * n_blocks + order(j), part * per + h))

    def out_spec(order):
        return pl.BlockSpec((blk, wcols), lambda b, h, j: (b * n_blocks + order(j), h))

    out = jax.ShapeDtypeStruct((batch * seq_all, d_model), F32)
    return pl.pallas_call(
        _hgrn_body,
        grid=(batch, per, n_blocks),
        in_specs=[spec(fwd, 0), spec(fwd, 1), spec(fwd, 3), spec(bwd, 0), spec(bwd, 1), spec(bwd, 4),
                  pl.BlockSpec((2, wcols), lambda b, h, j: (0, h))],
        out_specs=[out_spec(fwd), out_spec(bwd)],
        out_shape=[out, out],
        scratch_shapes=[pltpu.VMEM((2, HGRN_HEADS, C_HEAD_DIM, C_HEAD_DIM), F32)],
        compiler_params=_cp("arbitrary", "arbitrary", "arbitrary"),
        name="hgrn_scan",
    )(proj, proj, proj, proj, proj, proj, lb)


def _ffn_router_body(x_ref, g_ref, mod_ref, rwt_ref, rb_ref, hb_ref, hf_ref, idx_ref, wgt_ref):
    y = _rms(x_ref[...], g_ref[...])
    h2 = y * (1.0 + mod_ref[4:5, :]) + mod_ref[3:4, :]
    hb_ref[...] = h2.astype(BF16)
    hf_ref[...] = h2
    tm = h2.shape[0]
    logits = lax.dot_general(rwt_ref[...], h2, NT_DIMS, precision=lax.Precision.HIGHEST,
                             preferred_element_type=F32)
    s = _sigmoid(logits)
    sel = s + rb_ref[...]
    iota_g = lax.broadcasted_iota(jnp.int32, (GROUP_SIZE, tm), 0)
    group_scores = []
    for g in range(N_GROUPS):
        xg = sel[g * GROUP_SIZE:(g + 1) * GROUP_SIZE, :]
        m1 = jnp.max(xg, axis=0, keepdims=True)
        i1 = jnp.min(jnp.where(xg == m1, iota_g, GROUP_SIZE), axis=0, keepdims=True)
        m2 = jnp.max(jnp.where(iota_g == i1, NEG_INF, xg), axis=0, keepdims=True)
        group_scores.append(m1 + m2)
    gsc = jnp.concatenate(group_scores, axis=0)
    iota_n = lax.broadcasted_iota(jnp.int32, (N_GROUPS, tm), 0)
    gmask = jnp.zeros((N_GROUPS, tm), F32)
    for _ in range(TOPK_GROUPS):
        m = jnp.max(gsc, axis=0, keepdims=True)
        i = jnp.min(jnp.where(gsc == m, iota_n, N_GROUPS), axis=0, keepdims=True)
        hit = iota_n == i
        gmask = jnp.where(hit, 1.0, gmask)
        gsc = jnp.where(hit, NEG_INF, gsc)
    emask = jnp.concatenate([jnp.broadcast_to(gmask[g:g + 1, :], (GROUP_SIZE, tm)) for g in range(N_GROUPS)], axis=0)
    cur = jnp.where(emask > 0.0, sel, NEG_INF)
    iota_e = lax.broadcasted_iota(jnp.int32, (N_EXPERTS, tm), 0)
    idx_rows, w_rows = [], []
    for _ in range(TOP_K):
        m = jnp.max(cur, axis=0, keepdims=True)
        i = jnp.min(jnp.where(cur == m, iota_e, N_EXPERTS), axis=0, keepdims=True)
        hit = iota_e == i
        idx_rows.append(i)
        w_rows.append(jnp.sum(jnp.where(hit, s, 0.0), axis=0, keepdims=True))
        cur = jnp.where(hit, NEG_INF, cur)
    w = jnp.concatenate(w_rows, axis=0)
    idx_ref[...] = jnp.concatenate(idx_rows, axis=0)
    wgt_ref[...] = ROUTED_SCALE * w / jnp.sum(w, axis=0, keepdims=True)


def _ffn_router(x, g, mod, sel, router_wt, router_b, layer):
    rows, d = x.shape
    return pl.pallas_call(
        _ffn_router_body,
        grid=(rows // ROW_TILE,),
        in_specs=[pl.BlockSpec((ROW_TILE, d), lambda i: (i, 0)),
                  pl.BlockSpec((1, d), lambda i: (0, 0)),
                  pl.BlockSpec((None, 6, d), lambda i: (sel(i), 0, 0)),
                  pl.BlockSpec((None, N_EXPERTS, d), lambda i: (layer, 0, 0)),
                  pl.BlockSpec((None, N_EXPERTS, 1), lambda i: (layer, 0, 0))],
        out_specs=[pl.BlockSpec((ROW_TILE, d), lambda i: (i, 0)),
                   pl.BlockSpec((ROW_TILE, d), lambda i: (i, 0)),
                   pl.BlockSpec((TOP_K, ROW_TILE), lambda i: (0, i)),
                   pl.BlockSpec((TOP_K, ROW_TILE), lambda i: (0, i))],
        out_shape=[jax.ShapeDtypeStruct((rows, d), BF16),
                   jax.ShapeDtypeStruct((rows, d), F32),
                   jax.ShapeDtypeStruct((TOP_K, rows), jnp.int32),
                   jax.ShapeDtypeStruct((TOP_K, rows), F32)],
        compiler_params=_cp("arbitrary"),
        name=f"ffn_router{layer}",
    )(x, g, mod, router_wt, router_b)


def _dispatch(eidx, wgt):
    n, k = eidx.shape
    tm = MOE_TILE
    mask = jnp.sum((eidx[:, :, None] == jnp.arange(N_EXPERTS, dtype=jnp.int32)).astype(jnp.int32), axis=1)
    before = jnp.cumsum(mask, axis=0) - mask
    counts = jnp.sum(mask, axis=0)
    padded = ((counts + tm - 1) // tm) * tm
    p_end = jnp.cumsum(padded)
    p_start = p_end - padded
    pos = (p_start[eidx] + jnp.take_along_axis(before, eidx, axis=1)).astype(jnp.int32)
    n_tiles = (n * k + N_EXPERTS * (tm - 1)) // tm
    slots = n_tiles * tm
    flat = pos.reshape(-1)
    tok = jnp.zeros((slots,), jnp.int32).at[flat].set(jnp.repeat(jnp.arange(n, dtype=jnp.int32), k))
    gate = jnp.zeros((slots,), F32).at[flat].set(wgt.reshape(-1))
    tile_e = jnp.searchsorted(p_end, jnp.arange(n_tiles, dtype=jnp.int32) * tm, side="right")
    tile_e = jnp.minimum(tile_e, N_EXPERTS - 1).astype(jnp.int32)
    n_used = (p_end[-1] // tm).astype(jnp.int32).reshape(1)
    return tok, gate.reshape(slots, 1), tile_e, n_used, flat


def _expert_body(tile_e_ref, n_used_ref, tok_ref, h_hbm, gate_ref, wg_ref, wu_ref, wd_ref, y_ref, xbuf, sem):
    i = pl.program_id(0)
    tm = xbuf.shape[0]

    def row_copy(r):
        return pltpu.make_async_copy(h_hbm.at[pl.ds(tok_ref[i * tm + r], 1)], xbuf.at[pl.ds(r, 1)], sem)

    @pl.when(i < n_used_ref[0])
    def _():
        def issue(r, carry):
            row_copy(r).start()
            return carry

        lax.fori_loop(0, tm, issue, 0)

        def drain(r, carry):
            row_copy(r).wait()
            return carry

        lax.fori_loop(0, tm, drain, 0)
        x = xbuf[...].astype(BF16)
        g = jnp.dot(x, wg_ref[...].astype(BF16), preferred_element_type=F32)
        u = jnp.dot(x, wu_ref[...].astype(BF16), preferred_element_type=F32)
        hid = (_silu(g) * u).astype(BF16)
        y = jnp.dot(hid, wd_ref[...].astype(BF16), preferred_element_type=F32)
        y_ref[...] = gate_ref[...] * y

    @pl.when(i >= n_used_ref[0])
    def _():
        y_ref[...] = jnp.zeros_like(y_ref)


def _experts(h2f, tok, gate, tile_e, n_used, wg, wu, wd, layer):
    d = h2f.shape[1]
    de = wg.shape[3]
    tm = MOE_TILE
    n_tiles = tile_e.shape[0]

    def live(i, n_used_ref):
        return jnp.minimum(i, n_used_ref[0] - 1)

    grid_spec = pltpu.PrefetchScalarGridSpec(
        num_scalar_prefetch=3,
        grid=(n_tiles,),
        in_specs=[pl.BlockSpec(memory_space=pl.ANY),
                  pl.BlockSpec((tm, 1), lambda i, te, nu, tk: (live(i, nu), 0)),
                  pl.BlockSpec((None, None, d, de), lambda i, te, nu, tk: (layer, te[i], 0, 0)),
                  pl.BlockSpec((None, None, d, de), lambda i, te, nu, tk: (layer, te[i], 0, 0)),
                  pl.BlockSpec((None, None, de, d), lambda i, te, nu, tk: (layer, te[i], 0, 0))],
        out_specs=pl.BlockSpec((tm, d), lambda i, te, nu, tk: (i, 0)),
        scratch_shapes=[pltpu.VMEM((tm, d), F32), pltpu.SemaphoreType.DMA(())],
    )
    return pl.pallas_call(
        _expert_body,
        grid_spec=grid_spec,
        out_shape=jax.ShapeDtypeStruct((n_tiles * tm, d), F32),
        compiler_params=_cp("arbitrary"),
        name=f"experts{layer}",
    )(tile_e, n_used, tok, h2f, gate, wg, wu, wd)


def _combine_body(pos_ref, y_hbm, o_ref, buf, sem):
    i = pl.program_id(0)
    tt = o_ref.shape[0]
    n = TOP_K * tt

    def row_copy(a):
        r = a // TOP_K
        k = a % TOP_K
        return pltpu.make_async_copy(y_hbm.at[pl.ds(pos_ref[i * n + a], 1)], buf.at[k, pl.ds(r, 1)], sem)

    def issue(a, carry):
        row_copy(a).start()
        return carry

    lax.fori_loop(0, n, issue, 0)

    def drain(a, carry):
        row_copy(a).wait()
        return carry

    lax.fori_loop(0, n, drain, 0)
    acc = buf[0]
    for k in range(1, TOP_K):
        acc = acc + buf[k]
    o_ref[...] = acc


def _combine(ys, pos_flat, rows, layer):
    d = ys.shape[1]
    tt = COMBINE_TOKENS
    grid_spec = pltpu.PrefetchScalarGridSpec(
        num_scalar_prefetch=1,
        grid=(rows // tt,),
        in_specs=[pl.BlockSpec(memory_space=pl.ANY)],
        out_specs=pl.BlockSpec((tt, d), lambda i, ps: (i, 0)),
        scratch_shapes=[pltpu.VMEM((TOP_K, tt, d), F32), pltpu.SemaphoreType.DMA(())],
    )
    return pl.pallas_call(
        _combine_body,
        grid_spec=grid_spec,
        out_shape=jax.ShapeDtypeStruct((rows, d), F32),
        compiler_params=_cp("arbitrary"),
        name=f"combine{layer}",
    )(pos_flat, ys)


def _moe(x1, mod, sel, layer, norm_g, router_w, router_b, exp_wg, exp_wu, exp_wd, s_gate, s_up, s_down, tm_big):
    rows, d = x1.shape
    router_wt = jnp.swapaxes(router_w, 1, 2)
    h2b, h2f, idx_t, wgt_t = _ffn_router(x1, norm_g[layer][None], mod, sel, router_wt,
                                         router_b[:, :, None], layer)
    tok, gate, tile_e, n_used, pos_flat = _dispatch(idx_t.T, wgt_t.T)
    ys = _experts(h2f, tok, gate, tile_e, n_used, exp_wg, exp_wu, exp_wd, layer)
    routed = _combine(ys, pos_flat, rows, layer)
    hs = _glu(h2b, s_gate, s_up, layer, tm=tm_big, name=f"shared_glu{layer}")
    return _linear([hs], s_down, layer, n_cols=d, tm=ROW_TILE, out_dtype=F32, name=f"shared_down{layer}",
                   res=(x1, mod, 5, routed, sel))


def _rope_tables(ctx_len, lat_len):
    rows = lat_len // GRID_W
    pos_r = jnp.repeat(jnp.arange(rows), GRID_W).astype(F32)
    pos_c = jnp.tile(jnp.arange(GRID_W), rows).astype(F32)
    n = A_HEAD_DIM // 4
    inv = ROPE_BASE ** (-jnp.arange(n, dtype=F32) / n)
    ang = jnp.concatenate([pos_r[:, None] * inv, pos_c[:, None] * inv], axis=-1)
    cos, sin = jnp.cos(ang), jnp.sin(ang)
    cos_t = jnp.concatenate([jnp.ones((ctx_len, A_HEAD_DIM), F32), jnp.concatenate([cos, cos], axis=-1)], axis=0)
    sin_t = jnp.concatenate([jnp.zeros((ctx_len, A_HEAD_DIM), F32), jnp.concatenate([-sin, sin], axis=-1)], axis=0)
    return cos_t, sin_t


def kernel(x, c, ctx, c_ctx, ada_w, ada_b, norm_mix_g, norm_ffn_g, ab_w_in, ab_w_out, a_q_norm_g, a_k_norm_g,
           a_sink, b_conv_w, b_gate_b, b_norm_g, c_w_in, c_w_out, c_lb_logits, c_norm_g, router_w, router_b,
           exp_w_gate, exp_w_up, exp_w_down, shared_w_gate, shared_w_up, shared_w_down):
    batch, lat_len, d = x.shape
    ctx_len = ctx.shape[1]
    depth = ada_w.shape[0]
    seq_all = ctx_len + lat_len
    rows = batch * seq_all
    assert ctx_len % ROW_TILE == 0 and lat_len % ROW_TILE == 0 and ctx_len % MLSTM_CHUNK == 0
    assert lat_len % GRID_W == 0 and d % (HGRN_HEADS * C_HEAD_DIM) == 0
    tiles_per_sample = seq_all // ROW_TILE
    ctx_tiles = ctx_len // ROW_TILE
    tm_big = _row_tile_size(rows, 1152)

    def sel(i):
        return jnp.where(i % tiles_per_sample < ctx_tiles, batch, i // tiles_per_sample)

    cos_t, sin_t = _rope_tables(ctx_len, lat_len)
    lb_soft = jax.nn.softmax(c_lb_logits.astype(F32), axis=0)
    lower_bounds = jnp.cumsum(lb_soft, axis=0) - lb_soft[0:1]
    cond = jnp.concatenate([jax.nn.silu(c), jax.nn.silu(c_ctx)[None]], axis=0)
    cond = jnp.pad(cond, ((0, 16 - cond.shape[0]), (0, 0))).astype(BF16)

    xa = jnp.concatenate([ctx, x], axis=1).reshape(rows, d)
    for layer in range(depth):
        mod = _ada(cond, ada_w, ada_b, layer)[:batch + 1].reshape(batch + 1, 6, d)
        h = _modulate(xa, norm_mix_g[layer][None], mod, sel, shift_row=0, scale_row=1, name=f"mod_mix{layer}")
        if layer % 2 == 0:
            e = layer // 2
            proj = _linear([h], ab_w_in, e, n_cols=AB_MAIN, tm=tm_big, out_dtype=F32, name=f"ab_in{layer}")
            w_gates = jnp.pad(ab_w_in[e][:, AB_MAIN:], ((0, 0), (0, LANE_BLOCK - AB_GATES)))[None]
            graw = _linear([h], w_gates, 0, n_cols=LANE_BLOCK, tm=tm_big, out_dtype=F32, name=f"ab_gates{layer}")
            qn, kn, vb = _attn_prep(proj, cos_t, sin_t, a_q_norm_g[e][None], a_k_norm_g[e][None], tiles_per_sample)
            ya = _attention(qn, kn, vb, a_sink[e], batch=batch, seq_all=seq_all, ctx_len=ctx_len)
            qk = _mlstm_conv(proj, b_conv_w[e], batch=batch, seq_all=seq_all, ctx_len=ctx_len)
            gate_b = jnp.pad(b_gate_b[e], (0, LANE_BLOCK - AB_GATES))[None]
            gates = _mlstm_gates(graw, gate_b)[:, :AB_GATES]
            gates_t = gates.reshape(batch, seq_all, 4, B_HEADS).transpose(0, 3, 2, 1)
            hf, hb = _mlstm_scan(qk, proj, gates_t, batch=batch, seq_all=seq_all, ctx_len=ctx_len)
            yb = _headnorm_gate(hf, hb, proj, AB_MAIN - B_V, b_norm_g[e][None], head_dim=B_V_DIM,
                                gate_fn=_sigmoid, name="mlstm_out")
            xa = _linear([ya, yb], ab_w_out, e, n_cols=d, tm=ROW_TILE, out_dtype=F32, name=f"ab_out{layer}",
                         res=(xa, mod, 2, None, sel))
        else:
            o = layer // 2
            proj = _linear([h], c_w_in, o, n_cols=5 * d, tm=tm_big, out_dtype=F32, name=f"c_in{layer}")
            of, ob = _hgrn_scan(proj, lower_bounds[layer], batch=batch, seq_all=seq_all, ctx_len=ctx_len, d_model=d)
            yc = _headnorm_gate(of, ob, proj, 2 * d, c_norm_g[o][None], head_dim=C_HEAD_DIM, gate_fn=_silu,
                                name="hgrn_out")
            xa = _linear([yc], c_w_out, o, n_cols=d, tm=ROW_TILE, out_dtype=F32, name=f"c_out{layer}",
                         res=(xa, mod, 2, None, sel))
        xa = _moe(xa, mod, sel, layer, norm_ffn_g, router_w, router_b, exp_w_gate, exp_w_up, exp_w_down,
                  shared_w_gate, shared_w_up, shared_w_down, tm_big)
    return xa.reshape(batch, seq_all, d)[:, ctx_len:]
```

```python
import functools

import jax
import jax.numpy as jnp
from jax import lax
from jax.experimental import pallas as pl
from jax.experimental.pallas import tpu as pltpu

F32 = jnp.float32
BF16 = jnp.bfloat16
NEG_INF = float("-inf")

EPS = 1e-6
GRID_W = 64
ROPE_BASE = 10000.0

A_HEADS = 16
A_KV_HEADS = 4
A_GROUP = A_HEADS // A_KV_HEADS
A_HEAD_DIM = 128
WINDOW = 128
A_Q = A_HEADS * A_HEAD_DIM
A_KV = A_KV_HEADS * A_HEAD_DIM

B_HEADS = 8
B_QK_DIM = 128
B_V_DIM = 256
B_CONV_W = 5
GATE_CAP = 15.0
B_QK = B_HEADS * B_QK_DIM
B_V = B_HEADS * B_V_DIM
AB_MAIN = A_Q + 2 * A_KV + 2 * B_QK + 2 * B_V
AB_GATES = 4 * B_HEADS

C_HEAD_DIM = 128

N_EXPERTS = 64
TOP_K = 8
N_GROUPS = 8
TOPK_GROUPS = 4
GROUP_SIZE = N_EXPERTS // N_GROUPS
ROUTED_SCALE = 2.5

VMEM_LIMIT_BYTES = 56 * 1024 * 1024
LANE_BLOCK = 128
ROW_TILE = 256
COL_TILE = 512
MLSTM_CHUNK = 256
HGRN_BLOCK = 256
HGRN_CHUNK = 32
HGRN_HEADS = 2
MOE_TILE = 256
PUSH_TOKENS = 128
COMBINE_TOKENS = 32

NT_DIMS = (((1,), (1,)), ((), ()))
TN_DIMS = (((0,), (0,)), ((), ()))


def _cp(*sem):
    return pltpu.CompilerParams(dimension_semantics=sem, vmem_limit_bytes=VMEM_LIMIT_BYTES)


def _sigmoid(x):
    return jax.nn.sigmoid(x)


def _silu(x):
    return x * jax.nn.sigmoid(x)


def _rms(x, g):
    return x * lax.rsqrt(jnp.mean(x * x, axis=-1, keepdims=True) + EPS) * g


def _row_tile_size(rows, limit):
    best = 16
    for t in range(16, limit + 1, 16):
        if rows % t == 0:
            best = t
    return best


def _ada_body(a_ref, w_ref, b_ref, o_ref):
    o_ref[...] = jnp.dot(a_ref[...], w_ref[...].astype(BF16), preferred_element_type=F32) + b_ref[...]


def _ada(a, ada_w, ada_b, layer):
    depth, d, d6 = ada_w.shape
    rows = a.shape[0]
    return pl.pallas_call(
        _ada_body,
        grid=(d6 // COL_TILE,),
        in_specs=[pl.BlockSpec((rows, d), lambda j: (0, 0)),
                  pl.BlockSpec((None, d, COL_TILE), lambda j: (layer, 0, j)),
                  pl.BlockSpec((None, 1, COL_TILE), lambda j: (layer, 0, j))],
        out_specs=pl.BlockSpec((rows, COL_TILE), lambda j: (0, j)),
        out_shape=jax.ShapeDtypeStruct((rows, d6), F32),
        compiler_params=_cp("arbitrary"),
        name=f"ada{layer}",
    )(a, ada_w, ada_b.reshape(depth, 1, d6))


def _modulate_body(x_ref, g_ref, mod_ref, o_ref, *, shift_row, scale_row):
    y = _rms(x_ref[...], g_ref[...])
    o_ref[...] = (y * (1.0 + mod_ref[scale_row:scale_row + 1, :])
                  + mod_ref[shift_row:shift_row + 1, :]).astype(o_ref.dtype)


def _modulate(x, g, mod, sel, *, shift_row, scale_row, name):
    rows, d = x.shape
    return pl.pallas_call(
        functools.partial(_modulate_body, shift_row=shift_row, scale_row=scale_row),
        grid=(rows // ROW_TILE,),
        in_specs=[pl.BlockSpec((ROW_TILE, d), lambda i: (i, 0)),
                  pl.BlockSpec((1, d), lambda i: (0, 0)),
                  pl.BlockSpec((None, 6, d), lambda i: (sel(i), 0, 0))],
        out_specs=pl.BlockSpec((ROW_TILE, d), lambda i: (i, 0)),
        out_shape=jax.ShapeDtypeStruct((rows, d), BF16),
        compiler_params=_cp("arbitrary"),
        name=name,
    )(x, g, mod)


def _linear_body(*refs, n_a, gate_row, has_extra):
    a_refs = refs[:n_a]
    w_refs = refs[n_a:2 * n_a]
    p = 2 * n_a
    x_ref = mod_ref = e_ref = None
    if gate_row is not None:
        x_ref, mod_ref = refs[p], refs[p + 1]
        p += 2
        if has_extra:
            e_ref = refs[p]
            p += 1
    o_ref = refs[p]
    wb_refs = refs[p + 1:]

    @pl.when(pl.program_id(1) == 0)
    def _():
        for w_ref, wb in zip(w_refs, wb_refs):
            wb[...] = w_ref[...].astype(BF16)

    acc = jnp.dot(a_refs[0][...], wb_refs[0][...], preferred_element_type=F32)
    for a_ref, wb in zip(a_refs[1:], wb_refs[1:]):
        acc = acc + jnp.dot(a_ref[...], wb[...], preferred_element_type=F32)
    if gate_row is not None:
        if e_ref is not None:
            acc = acc + e_ref[...]
        acc = x_ref[...] + mod_ref[gate_row:gate_row + 1, :] * acc
    o_ref[...] = acc.astype(o_ref.dtype)


def _linear(a_list, w, layer, *, n_cols, tm, out_dtype, name, res=None):
    n_a = len(a_list)
    rows = a_list[0].shape[0]
    k_each = w.shape[1] // n_a
    tn = min(COL_TILE, n_cols)
    in_specs = [pl.BlockSpec((tm, k_each), lambda j, i: (i, 0)) for _ in a_list]
    for idx in range(n_a):
        in_specs.append(pl.BlockSpec((None, k_each, tn), lambda j, i, idx=idx: (layer, idx, j)))
    args = list(a_list) + [w] * n_a
    gate_row, has_extra = None, False
    if res is not None:
        x, mod, gate_row, extra, sel = res
        d = mod.shape[-1]
        in_specs.append(pl.BlockSpec((tm, tn), lambda j, i: (i, j)))
        in_specs.append(pl.BlockSpec((None, 6, tn), lambda j, i: (sel(i), 0, j)))
        args += [x, mod]
        if extra is not None:
            has_extra = True
            in_specs.append(pl.BlockSpec((tm, tn), lambda j, i: (i, j)))
            args.append(extra)
    return pl.pallas_call(
        functools.partial(_linear_body, n_a=n_a, gate_row=gate_row, has_extra=has_extra),
        grid=(n_cols // tn, rows // tm),
        in_specs=in_specs,
        out_specs=pl.BlockSpec((tm, tn), lambda j, i: (i, j)),
        out_shape=jax.ShapeDtypeStruct((rows, n_cols), out_dtype),
        scratch_shapes=[pltpu.VMEM((k_each, tn), BF16) for _ in range(n_a)],
        compiler_params=_cp("arbitrary", "arbitrary"),
        name=name,
    )(*args)


def _glu_body(a_ref, wg_ref, wu_ref, o_ref, wgb, wub):
    @pl.when(pl.program_id(1) == 0)
    def _():
        wgb[...] = wg_ref[...].astype(BF16)
        wub[...] = wu_ref[...].astype(BF16)

    a = a_ref[...]
    g = jnp.dot(a, wgb[...], preferred_element_type=F32)
    u = jnp.dot(a, wub[...], preferred_element_type=F32)
    o_ref[...] = (_silu(g) * u).astype(o_ref.dtype)


def _glu(a, wg, wu, layer, *, tm, name):
    rows, k = a.shape
    n_cols = wg.shape[2]
    tn = min(COL_TILE // 2, n_cols)
    wspec = pl.BlockSpec((None, k, tn), lambda j, i: (layer, 0, j))
    return pl.pallas_call(
        _glu_body,
        grid=(n_cols // tn, rows // tm),
        in_specs=[pl.BlockSpec((tm, k), lambda j, i: (i, 0)), wspec, wspec],
        out_specs=pl.BlockSpec((tm, tn), lambda j, i: (i, j)),
        out_shape=jax.ShapeDtypeStruct((rows, n_cols), BF16),
        scratch_shapes=[pltpu.VMEM((k, tn), BF16), pltpu.VMEM((k, tn), BF16)],
        compiler_params=_cp("arbitrary", "arbitrary"),
        name=name,
    )(a, wg, wu)


def _attn_prep_body(p_ref, cos_ref, sin_ref, qg_ref, kg_ref, q_ref, k_ref, v_ref):
    cos = cos_ref[...]
    sin = sin_ref[...]

    def norm_rope(xh, g):
        y = _rms(xh, g)
        return y * cos + pltpu.roll(y, A_HEAD_DIM // 2, axis=1) * sin

    for hd in range(A_HEADS):
        sl = slice(hd * A_HEAD_DIM, (hd + 1) * A_HEAD_DIM)
        q_ref[:, sl] = norm_rope(p_ref[:, sl], qg_ref[...]).astype(BF16)
    for hd in range(A_KV_HEADS):
        sl = slice(hd * A_HEAD_DIM, (hd + 1) * A_HEAD_DIM)
        k_ref[:, sl] = norm_rope(p_ref[:, A_Q + hd * A_HEAD_DIM:A_Q + (hd + 1) * A_HEAD_DIM], kg_ref[...]).astype(BF16)
    v_ref[...] = p_ref[:, A_Q + A_KV:A_Q + 2 * A_KV].astype(BF16)


def _attn_prep(proj, cos_t, sin_t, qg, kg, tiles_per_sample):
    rows = proj.shape[0]
    width = A_Q + 2 * A_KV
    return pl.pallas_call(
        _attn_prep_body,
        grid=(rows // ROW_TILE,),
        in_specs=[pl.BlockSpec((ROW_TILE, width), lambda i: (i, 0)),
                  pl.BlockSpec((ROW_TILE, A_HEAD_DIM), lambda i: (i % tiles_per_sample, 0)),
                  pl.BlockSpec((ROW_TILE, A_HEAD_DIM), lambda i: (i % tiles_per_sample, 0)),
                  pl.BlockSpec((1, A_HEAD_DIM), lambda i: (0, 0)),
                  pl.BlockSpec((1, A_HEAD_DIM), lambda i: (0, 0))],
        out_specs=[pl.BlockSpec((ROW_TILE, A_Q), lambda i: (i, 0)),
                   pl.BlockSpec((ROW_TILE, A_KV), lambda i: (i, 0)),
                   pl.BlockSpec((ROW_TILE, A_KV), lambda i: (i, 0))],
        out_shape=[jax.ShapeDtypeStruct((rows, A_Q), BF16),
                   jax.ShapeDtypeStruct((rows, A_KV), BF16),
                   jax.ShapeDtypeStruct((rows, A_KV), BF16)],
        compiler_params=_cp("arbitrary"),
        name="attn_prep",
    )(proj, cos_t, sin_t, qg, kg)


def _attn_body(sink_ref, q_ref, kp_ref, ko_ref, kn_ref, kc_ref, vp_ref, vo_ref, vn_ref, vc_ref, o_ref,
               *, ctx_blocks, lat_len):
    h = pl.program_id(1)
    j = pl.program_id(2)
    n = j - ctx_blocks
    nw = 3 * WINDOW
    m_ctx = kc_ref.shape[0]
    kw = jnp.concatenate([kp_ref[...], ko_ref[...], kn_ref[...], kc_ref[...]], axis=0)
    vw = jnp.concatenate([vp_ref[...], vo_ref[...], vn_ref[...], vc_ref[...]], axis=0)
    qi = lax.broadcasted_iota(jnp.int32, (WINDOW, nw + m_ctx), 0)
    wi = lax.broadcasted_iota(jnp.int32, (WINDOW, nw + m_ctx), 1)
    key_pos = (n - 1) * WINDOW + wi
    in_win = (jnp.abs(wi - WINDOW - qi) <= WINDOW) & (key_pos >= 0) & (key_pos < lat_len) & (n >= 0)
    valid = (wi >= nw) | in_win
    bias = jnp.where(valid, 0.0, NEG_INF).astype(F32)
    scale = A_HEAD_DIM ** -0.5
    for g in range(A_GROUP):
        sl = slice(g * A_HEAD_DIM, (g + 1) * A_HEAD_DIM)
        s = lax.dot_general(q_ref[:, sl], kw, NT_DIMS, preferred_element_type=F32) * scale + bias
        sink = sink_ref[h * A_GROUP + g]
        m = jnp.maximum(jnp.max(s, axis=-1, keepdims=True), sink)
        p = jnp.exp(s - m)
        denom = jnp.sum(p, axis=-1, keepdims=True) + jnp.exp(sink - m)
        o = jnp.dot(p.astype(BF16), vw, preferred_element_type=F32)
        o_ref[:, sl] = (o / denom).astype(o_ref.dtype)


def _attention(qn, kn, vb, sink, *, batch, seq_all, ctx_len):
    blocks = seq_all // WINDOW
    ctx_blocks = ctx_len // WINDOW
    last = blocks - 1
    qw = A_GROUP * A_HEAD_DIM

    def kv_spec(shift):
        def imap(b, h, j):
            return (b * blocks + jnp.clip(j + shift, ctx_blocks, last), h)
        return pl.BlockSpec((WINDOW, A_HEAD_DIM), imap)

    ctx_spec = pl.BlockSpec((ctx_len, A_HEAD_DIM), lambda b, h, j: (b * (seq_all // ctx_len), h))
    return pl.pallas_call(
        functools.partial(_attn_body, ctx_blocks=ctx_blocks, lat_len=seq_all - ctx_len),
        grid=(batch, A_KV_HEADS, blocks),
        in_specs=[pl.BlockSpec(memory_space=pltpu.SMEM),
                  pl.BlockSpec((WINDOW, qw), lambda b, h, j: (b * blocks + j, h)),
                  kv_spec(-1), kv_spec(0), kv_spec(1), ctx_spec,
                  kv_spec(-1), kv_spec(0), kv_spec(1), ctx_spec],
        out_specs=pl.BlockSpec((WINDOW, qw), lambda b, h, j: (b * blocks + j, h)),
        out_shape=jax.ShapeDtypeStruct((batch * seq_all, A_Q), BF16),
        compiler_params=_cp("arbitrary", "arbitrary", "arbitrary"),
        name="window_attention",
    )(sink, qn, kn, kn, kn, kn, vb, vb, vb, vb)


def _conv_body(x_ref, w_ref, o_ref, *, ctx_len, k_first_block):
    x = x_ref[...]
    t_len = x.shape[0]
    t = lax.broadcasted_iota(jnp.int32, x.shape, 0)
    half = B_CONV_W // 2
    acc = x * w_ref[half:half + 1, :]
    for d in range(-half, half + 1):
        if d == 0:
            continue
        xs = pltpu.roll(x, (-d) % t_len, axis=0)
        u = t + d
        ok = ((t < ctx_len) & (u >= 0) & (u < ctx_len)) | ((t >= ctx_len) & (u >= ctx_len) & (u < t_len))
        acc = acc + jnp.where(ok, xs, 0.0) * w_ref[half + d:half + d + 1, :]
    k_scale = jnp.where(pl.program_id(1) >= k_first_block, B_QK_DIM ** -0.5, 1.0).astype(F32)
    o_ref[...] = (_silu(acc) * k_scale).astype(o_ref.dtype)


def _mlstm_conv(proj, conv_w, *, batch, seq_all, ctx_len):
    first = (A_Q + 2 * A_KV) // LANE_BLOCK
    nblk = 2 * B_QK // LANE_BLOCK
    return pl.pallas_call(
        functools.partial(_conv_body, ctx_len=ctx_len, k_first_block=B_QK // LANE_BLOCK),
        grid=(batch, nblk),
        in_specs=[pl.BlockSpec((seq_all, LANE_BLOCK), lambda b, c: (b, first + c)),
                  pl.BlockSpec((B_CONV_W, LANE_BLOCK), lambda b, c: (0, c))],
        out_specs=pl.BlockSpec((seq_all, LANE_BLOCK), lambda b, c: (b, c)),
        out_shape=jax.ShapeDtypeStruct((batch * seq_all, 2 * B_QK), BF16),
        compiler_params=_cp("arbitrary", "arbitrary"),
        name="mlstm_conv",
    )(proj, conv_w)


def _gates_body(raw_ref, b_ref, o_ref):
    g = raw_ref[...] + b_ref[...]
    g = GATE_CAP * jnp.tanh(g / GATE_CAP)
    lane = lax.broadcasted_iota(jnp.int32, g.shape, 1)
    is_forget = (lane // B_HEADS) % 2 == 1
    log_sig = jnp.minimum(g, 0.0) - jnp.log(1.0 + jnp.exp(-jnp.abs(g)))
    o_ref[...] = jnp.where(is_forget, log_sig, g)


def _mlstm_gates(raw, gate_b):
    rows, width = raw.shape
    tr = _row_tile_size(rows, 2048)
    return pl.pallas_call(
        _gates_body,
        grid=(rows // tr,),
        in_specs=[pl.BlockSpec((tr, width), lambda i: (i, 0)),
                  pl.BlockSpec((1, width), lambda i: (0, 0))],
        out_specs=pl.BlockSpec((tr, width), lambda i: (i, 0)),
        out_shape=jax.ShapeDtypeStruct((rows, width), F32),
        compiler_params=_cp("arbitrary"),
        name="mlstm_gates",
    )(raw, gate_b)


def _mlstm_direction(q_ref, k_ref, v_ref, g_ref, o_ref, st_ref, n_ref, m_ref, d, rev):
    c_len = q_ref.shape[0]
    q = q_ref[...]
    k = k_ref[...]
    v = v_ref[...].astype(BF16)
    ig = g_ref[2 * d:2 * d + 1, :]
    lf = g_ref[2 * d + 1:2 * d + 2, :]
    r = lax.broadcasted_iota(jnp.int32, (c_len, c_len), 0)
    c = lax.broadcasted_iota(jnp.int32, (c_len, c_len), 1)
    eye = r == c
    tri = (c >= r) if rev else (c <= r)
    tri_t = (r >= c) if rev else (r <= c)
    lf_col = jnp.sum(jnp.where(eye, lf, 0.0), axis=1, keepdims=True)
    ig_col = jnp.sum(jnp.where(eye, ig, 0.0), axis=1, keepdims=True)
    b_col = jnp.sum(jnp.where(tri, lf, 0.0), axis=1, keepdims=True)
    b_row = jnp.sum(jnp.where(tri_t, lf_col, 0.0), axis=0, keepdims=True)
    m_prev = m_ref[d]
    d_st = b_col + m_prev
    d_in = jnp.where(tri, b_col - b_row + ig, NEG_INF)
    m_t = jnp.maximum(d_st, jnp.max(d_in, axis=1, keepdims=True))
    s = jnp.exp(d_in - m_t) * lax.dot_general(q, k, NT_DIMS, preferred_element_type=F32)
    w_st = jnp.exp(d_st - m_t)
    st = st_ref[d]
    num = (jnp.dot(s.astype(BF16), v, preferred_element_type=F32)
           + w_st * jnp.dot(q, st.astype(BF16), preferred_element_type=F32))
    qn = jnp.sum(q.astype(F32) * n_ref[d], axis=1, keepdims=True)
    den = jnp.sum(s, axis=1, keepdims=True) + w_st * qn
    o_ref[...] = num / jnp.maximum(jnp.abs(den), jnp.exp(-m_t))
    b_end = jnp.sum(lf, axis=1, keepdims=True)
    d_up = b_end - b_col + ig_col
    m_new = jnp.maximum(b_end + m_prev, jnp.max(d_up, axis=0, keepdims=True))
    w_up = jnp.exp(d_up - m_new)
    a = jnp.exp(b_end + m_prev - m_new)
    kw = k.astype(F32) * w_up
    st_ref[d] = a * st + lax.dot_general(kw.astype(BF16), v, TN_DIMS, preferred_element_type=F32)
    n_ref[d] = a * n_ref[d] + jnp.sum(kw, axis=0, keepdims=True)
    m_ref[d] = m_new


def _mlstm_body(qf, kf, vf, gf, qb, kb, vb, gb, of, ob, st_ref, n_ref, m_ref):
    @pl.when(pl.program_id(2) == 0)
    def _():
        st_ref[...] = jnp.zeros_like(st_ref)
        n_ref[...] = jnp.zeros_like(n_ref)
        m_ref[...] = jnp.zeros_like(m_ref)

    _mlstm_direction(qf, kf, vf, gf, of, st_ref, n_ref, m_ref, 0, False)
    _mlstm_direction(qb, kb, vb, gb, ob, st_ref, n_ref, m_ref, 1, True)


def _scan_orders(n_chunks, ctx_chunks):
    fwd = lambda j: j
    bwd = lambda j: jnp.where(j < ctx_chunks, ctx_chunks - 1 - j, n_chunks - 1 - (j - ctx_chunks))
    return fwd, bwd


def _mlstm_scan(qk, proj, gates_t, *, batch, seq_all, ctx_len):
    ch = MLSTM_CHUNK
    n_chunks = seq_all // ch
    fwd, bwd = _scan_orders(n_chunks, ctx_len // ch)
    v_first = (A_Q + 2 * A_KV + 2 * B_QK) // B_V_DIM

    def specs(order):
        return [pl.BlockSpec((ch, B_QK_DIM), lambda b, h, j: (b * n_chunks + order(j), h)),
                pl.BlockSpec((ch, B_QK_DIM), lambda b, h, j: (b * n_chunks + order(j), B_HEADS + h)),
                pl.BlockSpec((ch, B_V_DIM), lambda b, h, j: (b * n_chunks + order(j), v_first + h)),
                pl.BlockSpec((None, None, 4, ch), lambda b, h, j: (b, h, 0, order(j)))]

    def out_spec(order):
        return pl.BlockSpec((ch, B_V_DIM), lambda b, h, j: (b * n_chunks + order(j), h))

    out = jax.ShapeDtypeStruct((batch * seq_all, B_V), F32)
    return pl.pallas_call(
        _mlstm_body,
        grid=(batch, B_HEADS, n_chunks),
        in_specs=specs(fwd) + specs(bwd),
        out_specs=[out_spec(fwd), out_spec(bwd)],
        out_shape=[out, out],
        scratch_shapes=[pltpu.VMEM((2, B_QK_DIM, B_V_DIM), F32),
                        pltpu.VMEM((2, 1, B_QK_DIM), F32),
                        pltpu.VMEM((2, 1, 1), F32)],
        compiler_params=_cp("arbitrary", "arbitrary", "arbitrary"),
        name="mlstm_scan",
    )(qk, qk, proj, gates_t, qk, qk, proj, gates_t)


def _headnorm_gate_body(a_ref, b_ref, gate_ref, g_ref, o_ref, *, head_dim, gate_fn):
    for hd in range(a_ref.shape[1] // head_dim):
        sl = slice(hd * head_dim, (hd + 1) * head_dim)
        y = _rms(a_ref[:, sl] + b_ref[:, sl], g_ref[:, sl])
        o_ref[:, sl] = (y * gate_fn(gate_ref[:, sl])).astype(o_ref.dtype)


def _headnorm_gate(a, b, proj, gate_col0, norm_g, *, head_dim, gate_fn, name):
    rows, width = a.shape
    tc = min(1024, width)
    return pl.pallas_call(
        functools.partial(_headnorm_gate_body, head_dim=head_dim, gate_fn=gate_fn),
        grid=(rows // ROW_TILE, width // tc),
        in_specs=[pl.BlockSpec((ROW_TILE, tc), lambda i, c: (i, c)),
                  pl.BlockSpec((ROW_TILE, tc), lambda i, c: (i, c)),
                  pl.BlockSpec((ROW_TILE, tc), lambda i, c: (i, gate_col0 // tc + c)),
                  pl.BlockSpec((1, tc), lambda i, c: (0, c))],
        out_specs=pl.BlockSpec((ROW_TILE, tc), lambda i, c: (i, c)),
        out_shape=jax.ShapeDtypeStruct((rows, width), BF16),
        compiler_params=_cp("arbitrary", "arbitrary"),
        name=name,
    )(a, b, proj, norm_g)


def _hgrn_direction(q_ref, i_ref, f_ref, lb_ref, o_ref, st_ref, d, rev):
    rows = q_ref.shape[0]
    ch = HGRN_CHUNK
    lb = lb_ref[d:d + 1, :]
    q = _silu(q_ref[...])
    f = lb + (1.0 - lb) * _sigmoid(f_ref[...])
    kk = 1.0 - f
    a_cum = jnp.log(f)
    pos = lax.broadcasted_iota(jnp.int32, a_cum.shape, 0) % ch
    sh = 1
    while sh < ch:
        if rev:
            a_cum = a_cum + jnp.where(pos < ch - sh, pltpu.roll(a_cum, rows - sh, axis=0), 0.0)
        else:
            a_cum = a_cum + jnp.where(pos >= sh, pltpu.roll(a_cum, sh, axis=0), 0.0)
        sh *= 2
    q_in = (q * jnp.exp(a_cum)).astype(BF16)
    r = lax.broadcasted_iota(jnp.int32, (ch, ch), 0)
    c = lax.broadcasted_iota(jnp.int32, (ch, ch), 1)
    tri = (c >= r) if rev else (c <= r)
    n_chunks = rows // ch
    order = range(n_chunks - 1, -1, -1) if rev else range(n_chunks)
    end_row = 0 if rev else ch - 1
    mid_row = ch // 2
    for ci in order:
        rs = slice(ci * ch, (ci + 1) * ch)
        for hd in range(HGRN_HEADS):
            cs = slice(hd * C_HEAD_DIM, (hd + 1) * C_HEAD_DIM)
            a_c = a_cum[rs, cs]
            a_mid = a_c[mid_row:mid_row + 1, :]
            a_end = a_c[end_row:end_row + 1, :]
            v = i_ref[rs, cs].astype(BF16)
            qh = (q[rs, cs] * jnp.exp(a_c - a_mid)).astype(BF16)
            kh = (kk[rs, cs] * jnp.exp(a_mid - a_c)).astype(BF16)
            att = jnp.where(tri, lax.dot_general(qh, kh, NT_DIMS, preferred_element_type=F32), 0.0)
            st = st_ref[d, hd]
            o = (jnp.dot(att.astype(BF16), v, preferred_element_type=F32)
                 + lax.dot_general(q_in[rs, cs], st.astype(BF16), NT_DIMS, preferred_element_type=F32))
            o_ref[rs, cs] = o
            ke = (kk[rs, cs] * jnp.exp(a_end - a_c)).astype(BF16)
            st_ref[d, hd] = jnp.exp(a_end) * st + lax.dot_general(v, ke, TN_DIMS, preferred_element_type=F32)


def _hgrn_body(qf, vf, ff, qb, vb, fb, lb_ref, of, ob, st_ref):
    @pl.when(pl.program_id(2) == 0)
    def _():
        st_ref[...] = jnp.zeros_like(st_ref)

    _hgrn_direction(qf, vf, ff, lb_ref, of, st_ref, 0, False)
    _hgrn_direction(qb, vb, fb, lb_ref, ob, st_ref, 1, True)


def _hgrn_scan(proj, lb, *, batch, seq_all, ctx_len, d_model):
    blk = HGRN_BLOCK
    n_blocks = seq_all // blk
    fwd, bwd = _scan_orders(n_blocks, ctx_len // blk)
    wcols = HGRN_HEADS * C_HEAD_DIM
    per = d_model // wcols

    def spec(order, part):
        return pl.BlockSpec((blk, wcols), lambda b, h, j: (b * n_blocks + order(j), part * per + h))

    def out_spec(order):
        return pl.BlockSpec((blk, wcols), lambda b, h, j: (b * n_blocks + order(j), h))

    out = jax.ShapeDtypeStruct((batch * seq_all, d_model), F32)
    return pl.pallas_call(
        _hgrn_body,
        grid=(batch, per, n_blocks),
        in_specs=[spec(fwd, 0), spec(fwd, 1), spec(fwd, 3), spec(bwd, 0), spec(bwd, 1), spec(bwd, 4),
                  pl.BlockSpec((2, wcols), lambda b, h, j: (0, h))],
        out_specs=[out_spec(fwd), out_spec(bwd)],
        out_shape=[out, out],
        scratch_shapes=[pltpu.VMEM((2, HGRN_HEADS, C_HEAD_DIM, C_HEAD_DIM), F32)],
        compiler_params=_cp("arbitrary", "arbitrary", "arbitrary"),
        name="hgrn_scan",
    )(proj, proj, proj, proj, proj, proj, lb)


def _ffn_router_body(x_ref, g_ref, mod_ref, rwt_ref, rb_ref, hb_ref, hf_ref, idx_ref, wgt_ref, rank_ref, cnt_ref):
    @pl.when(pl.program_id(0) == 0)
    def _():
        cnt_ref[...] = jnp.zeros_like(cnt_ref)

    y = _rms(x_ref[...], g_ref[...])
    h2 = y * (1.0 + mod_ref[4:5, :]) + mod_ref[3:4, :]
    hb_ref[...] = h2.astype(BF16)
    hf_ref[...] = h2
    tm = h2.shape[0]
    logits = lax.dot_general(rwt_ref[...], h2, NT_DIMS, precision=lax.Precision.HIGHEST,
                             preferred_element_type=F32)
    s = _sigmoid(logits)
    sel = s + rb_ref[...]
    iota_g = lax.broadcasted_iota(jnp.int32, (GROUP_SIZE, tm), 0)
    group_scores = []
    for g in range(N_GROUPS):
        xg = sel[g * GROUP_SIZE:(g + 1) * GROUP_SIZE, :]
        m1 = jnp.max(xg, axis=0, keepdims=True)
        i1 = jnp.min(jnp.where(xg == m1, iota_g, GROUP_SIZE), axis=0, keepdims=True)
        m2 = jnp.max(jnp.where(iota_g == i1, NEG_INF, xg), axis=0, keepdims=True)
        group_scores.append(m1 + m2)
    gsc = jnp.concatenate(group_scores, axis=0)
    iota_n = lax.broadcasted_iota(jnp.int32, (N_GROUPS, tm), 0)
    gmask = jnp.zeros((N_GROUPS, tm), F32)
    for _ in range(TOPK_GROUPS):
        m = jnp.max(gsc, axis=0, keepdims=True)
        i = jnp.min(jnp.where(gsc == m, iota_n, N_GROUPS), axis=0, keepdims=True)
        hit = iota_n == i
        gmask = jnp.where(hit, 1.0, gmask)
        gsc = jnp.where(hit, NEG_INF, gsc)
    emask = jnp.concatenate([jnp.broadcast_to(gmask[g:g + 1, :], (GROUP_SIZE, tm)) for g in range(N_GROUPS)], axis=0)
    cur = jnp.where(emask > 0.0, sel, NEG_INF)
    iota_e = lax.broadcasted_iota(jnp.int32, (N_EXPERTS, tm), 0)
    idx_rows, w_rows, hits = [], [], []
    for _ in range(TOP_K):
        m = jnp.max(cur, axis=0, keepdims=True)
        i = jnp.min(jnp.where(cur == m, iota_e, N_EXPERTS), axis=0, keepdims=True)
        hit = iota_e == i
        idx_rows.append(i)
        hits.append(hit)
        w_rows.append(jnp.sum(jnp.where(hit, s, 0.0), axis=0, keepdims=True))
        cur = jnp.where(hit, NEG_INF, cur)
    w = jnp.concatenate(w_rows, axis=0)
    idx_ref[...] = jnp.concatenate(idx_rows, axis=0)
    wgt_ref[...] = ROUTED_SCALE * w / jnp.sum(w, axis=0, keepdims=True)
    chosen = jnp.zeros((N_EXPERTS, tm), F32)
    for hit in hits:
        chosen = jnp.where(hit, 1.0, chosen)
    earlier = (lax.broadcasted_iota(jnp.int32, (tm, tm), 0) < lax.broadcasted_iota(jnp.int32, (tm, tm), 1))
    before = cnt_ref[...] + jnp.dot(chosen.astype(BF16), earlier.astype(BF16), preferred_element_type=F32)
    rank_ref[...] = jnp.concatenate(
        [jnp.sum(jnp.where(hit, before, 0.0), axis=0, keepdims=True) for hit in hits], axis=0).astype(jnp.int32)
    cnt_ref[...] = cnt_ref[...] + jnp.sum(chosen, axis=1, keepdims=True)


def _ffn_router(x, g, mod, sel, router_wt, router_b, layer):
    rows, d = x.shape
    return pl.pallas_call(
        _ffn_router_body,
        grid=(rows // ROW_TILE,),
        in_specs=[pl.BlockSpec((ROW_TILE, d), lambda i: (i, 0)),
                  pl.BlockSpec((1, d), lambda i: (0, 0)),
                  pl.BlockSpec((None, 6, d), lambda i: (sel(i), 0, 0)),
                  pl.BlockSpec((None, N_EXPERTS, d), lambda i: (layer, 0, 0)),
                  pl.BlockSpec((None, N_EXPERTS, 1), lambda i: (layer, 0, 0))],
        out_specs=[pl.BlockSpec((ROW_TILE, d), lambda i: (i, 0)),
                   pl.BlockSpec((ROW_TILE, d), lambda i: (i, 0)),
                   pl.BlockSpec((TOP_K, ROW_TILE), lambda i: (0, i)),
                   pl.BlockSpec((TOP_K, ROW_TILE), lambda i: (0, i)),
                   pl.BlockSpec((TOP_K, ROW_TILE), lambda i: (0, i)),
                   pl.BlockSpec((N_EXPERTS, 1), lambda i: (0, 0))],
        out_shape=[jax.ShapeDtypeStruct((rows, d), BF16),
                   jax.ShapeDtypeStruct((rows, d), F32),
                   jax.ShapeDtypeStruct((TOP_K, rows), jnp.int32),
                   jax.ShapeDtypeStruct((TOP_K, rows), F32),
                   jax.ShapeDtypeStruct((TOP_K, rows), jnp.int32),
                   jax.ShapeDtypeStruct((N_EXPERTS, 1), F32)],
        compiler_params=_cp("arbitrary"),
        name=f"ffn_router{layer}",
    )(x, g, mod, router_wt, router_b)


def _dispatch(idx_t, rank_t, counts):
    tm = MOE_TILE
    slots = idx_t.shape[0] * idx_t.shape[1]
    n_tiles = slots // tm
    counts = counts.reshape(-1).astype(jnp.int32)
    ends = jnp.cumsum(counts)
    starts = ends - counts
    pos = (starts[idx_t] + rank_t).T.reshape(-1)
    cuts = jnp.sort(jnp.concatenate([jnp.arange(n_tiles, dtype=jnp.int32) * tm, ends[:-1]]))
    nxt = jnp.concatenate([cuts[1:], jnp.full((1,), slots, jnp.int32)])
    v_tile = jnp.minimum(cuts // tm, n_tiles - 1)
    v_exp = jnp.minimum(jnp.sum((ends[None, :] <= cuts[:, None]).astype(jnp.int32), axis=1), N_EXPERTS - 1)
    v_lo = cuts - v_tile * tm
    v_hi = nxt - v_tile * tm
    return pos.astype(jnp.int32), v_tile.astype(jnp.int32), v_exp.astype(jnp.int32), v_lo, v_hi


def _push_body(pos_ref, h_ref, xs_hbm, sem):
    i = pl.program_id(0)
    tt = h_ref.shape[0]
    base = i * tt * TOP_K

    def issue(a, carry):
        pltpu.make_async_copy(h_ref.at[pl.ds(a // TOP_K, 1)], xs_hbm.at[pl.ds(pos_ref[base + a], 1)], sem).start()
        return carry

    lax.fori_loop(0, tt * TOP_K, issue, 0, unroll=8)
    for _ in range(TOP_K):
        pltpu.make_async_copy(h_ref, xs_hbm.at[pl.ds(0, tt)], sem).wait()


def _push(h2f, pos, layer):
    rows, d = h2f.shape
    tt = PUSH_TOKENS
    grid_spec = pltpu.PrefetchScalarGridSpec(
        num_scalar_prefetch=1,
        grid=(rows // tt,),
        in_specs=[pl.BlockSpec((tt, d), lambda i, ps: (i, 0))],
        out_specs=pl.BlockSpec(memory_space=pl.ANY),
        scratch_shapes=[pltpu.SemaphoreType.DMA(())],
    )
    return pl.pallas_call(
        _push_body,
        grid_spec=grid_spec,
        out_shape=jax.ShapeDtypeStruct((rows * TOP_K, d), h2f.dtype),
        compiler_params=_cp("arbitrary"),
        name=f"push{layer}",
    )(pos, h2f)


def _expert_body(vt_ref, ve_ref, lo_ref, hi_ref, x_ref, wg_ref, wu_ref, wd_ref, y_ref, wgb, wub, wdb):
    v = pl.program_id(0)
    lo = lo_ref[v]
    hi = hi_ref[v]
    prev = jnp.maximum(v - 1, 0)

    @pl.when((v == 0) | (ve_ref[v] != ve_ref[prev]))
    def _():
        wgb[...] = wg_ref[...].astype(BF16)
        wub[...] = wu_ref[...].astype(BF16)
        wdb[...] = wd_ref[...].astype(BF16)

    @pl.when(hi > lo)
    def _():
        x = x_ref[...].astype(BF16)
        g = jnp.dot(x, wgb[...], preferred_element_type=F32)
        u = jnp.dot(x, wub[...], preferred_element_type=F32)
        hid = (_silu(g) * u).astype(BF16)
        y = jnp.dot(hid, wdb[...], preferred_element_type=F32)
        r = lax.broadcasted_iota(jnp.int32, (y.shape[0], 1), 0)
        mine = (r >= lo) & (r < hi)

        @pl.when(lo == 0)
        def _():
            y_ref[...] = jnp.where(mine, y, 0.0)

        @pl.when(lo > 0)
        def _():
            y_ref[...] = jnp.where(mine, y, y_ref[...])


def _experts(xs, v_tile, v_exp, v_lo, v_hi, wg, wu, wd, layer):
    slots, d = xs.shape
    de = wg.shape[3]
    tm = MOE_TILE
    grid_spec = pltpu.PrefetchScalarGridSpec(
        num_scalar_prefetch=4,
        grid=(v_tile.shape[0],),
        in_specs=[pl.BlockSpec((tm, d), lambda v, vt, ve, lo, hi: (vt[v], 0)),
                  pl.BlockSpec((None, None, d, de), lambda v, vt, ve, lo, hi: (layer, ve[v], 0, 0)),
                  pl.BlockSpec((None, None, d, de), lambda v, vt, ve, lo, hi: (layer, ve[v], 0, 0)),
                  pl.BlockSpec((None, None, de, d), lambda v, vt, ve, lo, hi: (layer, ve[v], 0, 0))],
        out_specs=pl.BlockSpec((tm, d), lambda v, vt, ve, lo, hi: (vt[v], 0)),
        scratch_shapes=[pltpu.VMEM((d, de), BF16), pltpu.VMEM((d, de), BF16), pltpu.VMEM((de, d), BF16)],
    )
    return pl.pallas_call(
        _expert_body,
        grid_spec=grid_spec,
        out_shape=jax.ShapeDtypeStruct((slots, d), F32),
        compiler_params=_cp("arbitrary"),
        name=f"experts{layer}",
    )(v_tile, v_exp, v_lo, v_hi, xs, wg, wu, wd)


def _combine_body(pos_ref, w_ref, y_hbm, o_ref, buf, sem):
    i = pl.program_id(0)
    tt = o_ref.shape[0]
    n = TOP_K * tt
    slot = i % 2

    def issue(step, into):
        def body(a, carry):
            pltpu.make_async_copy(y_hbm.at[pl.ds(pos_ref[step * n + a], 1)],
                                  buf.at[into, a % TOP_K, pl.ds(a // TOP_K, 1)], sem.at[into]).start()
            return carry

        lax.fori_loop(0, n, body, 0, unroll=8)

    @pl.when(i == 0)
    def _():
        issue(0, 0)

    @pl.when(i + 1 < pl.num_programs(0))
    def _():
        issue(i + 1, 1 - slot)

    for k in range(TOP_K):
        pltpu.make_async_copy(y_hbm.at[pl.ds(0, tt)], buf.at[slot, k], sem.at[slot]).wait()
    w = w_ref[...]
    acc = w[:, 0:1] * buf[slot, 0]
    for k in range(1, TOP_K):
        acc = acc + w[:, k:k + 1] * buf[slot, k]
    o_ref[...] = acc


def _combine(ys, pos_flat, wgt, rows, layer):
    d = ys.shape[1]
    tt = COMBINE_TOKENS
    grid_spec = pltpu.PrefetchScalarGridSpec(
        num_scalar_prefetch=1,
        grid=(rows // tt,),
        in_specs=[pl.BlockSpec((tt, TOP_K), lambda i, ps: (i, 0)),
                  pl.BlockSpec(memory_space=pl.ANY)],
        out_specs=pl.BlockSpec((tt, d), lambda i, ps: (i, 0)),
        scratch_shapes=[pltpu.VMEM((2, TOP_K, tt, d), F32), pltpu.SemaphoreType.DMA((2,))],
    )
    return pl.pallas_call(
        _combine_body,
        grid_spec=grid_spec,
        out_shape=jax.ShapeDtypeStruct((rows, d), F32),
        compiler_params=_cp("arbitrary"),
        name=f"combine{layer}",
    )(pos_flat, wgt, ys)


def _moe(x1, mod, sel, layer, norm_g, router_w, router_b, exp_wg, exp_wu, exp_wd, s_gate, s_up, s_down, tm_big):
    rows, d = x1.shape
    router_wt = jnp.swapaxes(router_w, 1, 2)
    h2b, h2f, idx_t, wgt_t, rank_t, counts = _ffn_router(x1, norm_g[layer][None], mod, sel, router_wt,
                                                         router_b[:, :, None], layer)
    pos, v_tile, v_exp, v_lo, v_hi = _dispatch(idx_t, rank_t, counts)
    xs = _push(h2f, pos, layer)
    ys = _experts(xs, v_tile, v_exp, v_lo, v_hi, exp_wg, exp_wu, exp_wd, layer)
    routed = _combine(ys, pos, wgt_t.T, rows, layer)
    hs = _glu(h2b, s_gate, s_up, layer, tm=tm_big, name=f"shared_glu{layer}")
    return _linear([hs], s_down, layer, n_cols=d, tm=ROW_TILE, out_dtype=F32, name=f"shared_down{layer}",
                   res=(x1, mod, 5, routed, sel))


def _rope_tables(ctx_len, lat_len):
    rows = lat_len // GRID_W
    pos_r = jnp.repeat(jnp.arange(rows), GRID_W).astype(F32)
    pos_c = jnp.tile(jnp.arange(GRID_W), rows).astype(F32)
    n = A_HEAD_DIM // 4
    inv = ROPE_BASE ** (-jnp.arange(n, dtype=F32) / n)
    ang = jnp.concatenate([pos_r[:, None] * inv, pos_c[:, None] * inv], axis=-1)
    cos, sin = jnp.cos(ang), jnp.sin(ang)
    cos_t = jnp.concatenate([jnp.ones((ctx_len, A_HEAD_DIM), F32), jnp.concatenate([cos, cos], axis=-1)], axis=0)
    sin_t = jnp.concatenate([jnp.zeros((ctx_len, A_HEAD_DIM), F32), jnp.concatenate([-sin, sin], axis=-1)], axis=0)
    return cos_t, sin_t


def kernel(x, c, ctx, c_ctx, ada_w, ada_b, norm_mix_g, norm_ffn_g, ab_w_in, ab_w_out, a_q_norm_g, a_k_norm_g,
           a_sink, b_conv_w, b_gate_b, b_norm_g, c_w_in, c_w_out, c_lb_logits, c_norm_g, router_w, router_b,
           exp_w_gate, exp_w_up, exp_w_down, shared_w_gate, shared_w_up, shared_w_down):
    batch, lat_len, d = x.shape
    ctx_len = ctx.shape[1]
    depth = ada_w.shape[0]
    seq_all = ctx_len + lat_len
    rows = batch * seq_all
    assert ctx_len % ROW_TILE == 0 and lat_len % ROW_TILE == 0 and ctx_len % MLSTM_CHUNK == 0
    assert lat_len % GRID_W == 0 and d % (HGRN_HEADS * C_HEAD_DIM) == 0
    tiles_per_sample = seq_all // ROW_TILE
    ctx_tiles = ctx_len // ROW_TILE
    tm_big = _row_tile_size(rows, 1152)

    def sel(i):
        return jnp.where(i % tiles_per_sample < ctx_tiles, batch, i // tiles_per_sample)

    cos_t, sin_t = _rope_tables(ctx_len, lat_len)
    lb_soft = jax.nn.softmax(c_lb_logits.astype(F32), axis=0)
    lower_bounds = jnp.cumsum(lb_soft, axis=0) - lb_soft[0:1]
    cond = jnp.concatenate([jax.nn.silu(c), jax.nn.silu(c_ctx)[None]], axis=0)
    cond = jnp.pad(cond, ((0, 16 - cond.shape[0]), (0, 0))).astype(BF16)

    xa = jnp.concatenate([ctx, x], axis=1).reshape(rows, d)
    for layer in range(depth):
        mod = _ada(cond, ada_w, ada_b, layer)[:batch + 1].reshape(batch + 1, 6, d)
        h = _modulate(xa, norm_mix_g[layer][None], mod, sel, shift_row=0, scale_row=1, name=f"mod_mix{layer}")
        if layer % 2 == 0:
            e = layer // 2
            proj = _linear([h], ab_w_in, e, n_cols=AB_MAIN, tm=tm_big, out_dtype=F32, name=f"ab_in{layer}")
            w_gates = jnp.pad(ab_w_in[e][:, AB_MAIN:], ((0, 0), (0, LANE_BLOCK - AB_GATES)))[None]
            graw = _linear([h], w_gates, 0, n_cols=LANE_BLOCK, tm=tm_big, out_dtype=F32, name=f"ab_gates{layer}")
            qn, kn, vb = _attn_prep(proj, cos_t, sin_t, a_q_norm_g[e][None], a_k_norm_g[e][None], tiles_per_sample)
            ya = _attention(qn, kn, vb, a_sink[e], batch=batch, seq_all=seq_all, ctx_len=ctx_len)
            qk = _mlstm_conv(proj, b_conv_w[e], batch=batch, seq_all=seq_all, ctx_len=ctx_len)
            gate_b = jnp.pad(b_gate_b[e], (0, LANE_BLOCK - AB_GATES))[None]
            gates = _mlstm_gates(graw, gate_b)[:, :AB_GATES]
            gates_t = gates.reshape(batch, seq_all, 4, B_HEADS).transpose(0, 3, 2, 1)
            hf, hb = _mlstm_scan(qk, proj, gates_t, batch=batch, seq_all=seq_all, ctx_len=ctx_len)
            yb = _headnorm_gate(hf, hb, proj, AB_MAIN - B_V, b_norm_g[e][None], head_dim=B_V_DIM,
                                gate_fn=_sigmoid, name="mlstm_out")
            xa = _linear([ya, yb], ab_w_out, e, n_cols=d, tm=ROW_TILE, out_dtype=F32, name=f"ab_out{layer}",
                         res=(xa, mod, 2, None, sel))
        else:
            o = layer // 2
            proj = _linear([h], c_w_in, o, n_cols=5 * d, tm=tm_big, out_dtype=F32, name=f"c_in{layer}")
            of, ob = _hgrn_scan(proj, lower_bounds[layer], batch=batch, seq_all=seq_all, ctx_len=ctx_len, d_model=d)
            yc = _headnorm_gate(of, ob, proj, 2 * d, c_norm_g[o][None], head_dim=C_HEAD_DIM, gate_fn=_silu,
                                name="hgrn_out")
            xa = _linear([yc], c_w_out, o, n_cols=d, tm=ROW_TILE, out_dtype=F32, name=f"c_out{layer}",
                         res=(xa, mod, 2, None, sel))
        xa = _moe(xa, mod, sel, layer, norm_ffn_g, router_w, router_b, exp_w_gate, exp_w_up, exp_w_down,
                  shared_w_gate, shared_w_up, shared_w_down, tm_big)
    return xa.reshape(batch, seq_all, d)[:, ctx_len:]
```

```python
import functools

import jax
import jax.numpy as jnp
from jax import lax
from jax.experimental import pallas as pl
from jax.experimental.pallas import tpu as pltpu

F32 = jnp.float32
BF16 = jnp.bfloat16
NEG_INF = float("-inf")

EPS = 1e-6
GRID_W = 64
ROPE_BASE = 10000.0

A_HEADS = 16
A_KV_HEADS = 4
A_GROUP = A_HEADS // A_KV_HEADS
A_HEAD_DIM = 128
WINDOW = 128
A_Q = A_HEADS * A_HEAD_DIM
A_KV = A_KV_HEADS * A_HEAD_DIM

B_HEADS = 8
B_QK_DIM = 128
B_V_DIM = 256
B_CONV_W = 5
GATE_CAP = 15.0
B_QK = B_HEADS * B_QK_DIM
B_V = B_HEADS * B_V_DIM
AB_MAIN = A_Q + 2 * A_KV + 2 * B_QK + 2 * B_V
AB_GATES = 4 * B_HEADS

C_HEAD_DIM = 128

N_EXPERTS = 64
TOP_K = 8
N_GROUPS = 8
TOPK_GROUPS = 4
GROUP_SIZE = N_EXPERTS // N_GROUPS
ROUTED_SCALE = 2.5

VMEM_LIMIT_BYTES = 56 * 1024 * 1024
LANE_BLOCK = 128
ROW_TILE = 256
COL_TILE = 512
MLSTM_CHUNK = 256
HGRN_BLOCK = 256
HGRN_CHUNK = 32
HGRN_HEADS = 4
MOE_TILE = 256
PUSH_TOKENS = 128
COMBINE_TOKENS = 64

NT_DIMS = (((1,), (1,)), ((), ()))
TN_DIMS = (((0,), (0,)), ((), ()))


def _cp(*sem):
    return pltpu.CompilerParams(dimension_semantics=sem, vmem_limit_bytes=VMEM_LIMIT_BYTES)


def _sigmoid(x):
    return jax.nn.sigmoid(x)


def _silu(x):
    return x * jax.nn.sigmoid(x)


def _rms(x, g):
    return x * lax.rsqrt(jnp.mean(x * x, axis=-1, keepdims=True) + EPS) * g


def _row_tile_size(rows, limit):
    best = 16
    for t in range(16, limit + 1, 16):
        if rows % t == 0:
            best = t
    return best


def _ada_body(a_ref, w_ref, b_ref, o_ref):
    o_ref[...] = jnp.dot(a_ref[...], w_ref[...].astype(BF16), preferred_element_type=F32) + b_ref[...]


def _ada(a, ada_w, ada_b, layer):
    depth, d, d6 = ada_w.shape
    rows = a.shape[0]
    return pl.pallas_call(
        _ada_body,
        grid=(d6 // COL_TILE,),
        in_specs=[pl.BlockSpec((rows, d), lambda j: (0, 0)),
                  pl.BlockSpec((None, d, COL_TILE), lambda j: (layer, 0, j)),
                  pl.BlockSpec((None, 1, COL_TILE), lambda j: (layer, 0, j))],
        out_specs=pl.BlockSpec((rows, COL_TILE), lambda j: (0, j)),
        out_shape=jax.ShapeDtypeStruct((rows, d6), F32),
        compiler_params=_cp("arbitrary"),
        name=f"ada{layer}",
    )(a, ada_w, ada_b.reshape(depth, 1, d6))


def _modulate_body(x_ref, g_ref, mod_ref, o_ref, *, shift_row, scale_row):
    y = _rms(x_ref[...], g_ref[...])
    o_ref[...] = (y * (1.0 + mod_ref[scale_row:scale_row + 1, :])
                  + mod_ref[shift_row:shift_row + 1, :]).astype(o_ref.dtype)


def _modulate(x, g, mod, sel, *, shift_row, scale_row, name):
    rows, d = x.shape
    return pl.pallas_call(
        functools.partial(_modulate_body, shift_row=shift_row, scale_row=scale_row),
        grid=(rows // ROW_TILE,),
        in_specs=[pl.BlockSpec((ROW_TILE, d), lambda i: (i, 0)),
                  pl.BlockSpec((1, d), lambda i: (0, 0)),
                  pl.BlockSpec((None, 6, d), lambda i: (sel(i), 0, 0))],
        out_specs=pl.BlockSpec((ROW_TILE, d), lambda i: (i, 0)),
        out_shape=jax.ShapeDtypeStruct((rows, d), BF16),
        compiler_params=_cp("arbitrary"),
        name=name,
    )(x, g, mod)


def _linear_body(*refs, n_a, gate_row, has_extra):
    a_refs = refs[:n_a]
    w_refs = refs[n_a:2 * n_a]
    p = 2 * n_a
    x_ref = mod_ref = e_ref = None
    if gate_row is not None:
        x_ref, mod_ref = refs[p], refs[p + 1]
        p += 2
        if has_extra:
            e_ref = refs[p]
            p += 1
    o_ref = refs[p]
    wb_refs = refs[p + 1:]

    @pl.when(pl.program_id(1) == 0)
    def _():
        for w_ref, wb in zip(w_refs, wb_refs):
            wb[...] = w_ref[...].astype(BF16)

    acc = jnp.dot(a_refs[0][...], wb_refs[0][...], preferred_element_type=F32)
    for a_ref, wb in zip(a_refs[1:], wb_refs[1:]):
        acc = acc + jnp.dot(a_ref[...], wb[...], preferred_element_type=F32)
    if gate_row is not None:
        if e_ref is not None:
            acc = acc + e_ref[...]
        acc = x_ref[...] + mod_ref[gate_row:gate_row + 1, :] * acc
    o_ref[...] = acc.astype(o_ref.dtype)


def _linear(a_list, w, layer, *, n_cols, tm, out_dtype, name, res=None, tn=COL_TILE):
    n_a = len(a_list)
    rows = a_list[0].shape[0]
    k_each = w.shape[1] // n_a
    tn = min(tn, n_cols)
    in_specs = [pl.BlockSpec((tm, k_each), lambda j, i: (i, 0)) for _ in a_list]
    for idx in range(n_a):
        in_specs.append(pl.BlockSpec((None, k_each, tn), lambda j, i, idx=idx: (layer, idx, j)))
    args = list(a_list) + [w] * n_a
    gate_row, has_extra = None, False
    if res is not None:
        x, mod, gate_row, extra, sel = res
        d = mod.shape[-1]
        in_specs.append(pl.BlockSpec((tm, tn), lambda j, i: (i, j)))
        in_specs.append(pl.BlockSpec((None, 6, tn), lambda j, i: (sel(i), 0, j)))
        args += [x, mod]
        if extra is not None:
            has_extra = True
            in_specs.append(pl.BlockSpec((tm, tn), lambda j, i: (i, j)))
            args.append(extra)
    return pl.pallas_call(
        functools.partial(_linear_body, n_a=n_a, gate_row=gate_row, has_extra=has_extra),
        grid=(n_cols // tn, rows // tm),
        in_specs=in_specs,
        out_specs=pl.BlockSpec((tm, tn), lambda j, i: (i, j)),
        out_shape=jax.ShapeDtypeStruct((rows, n_cols), out_dtype),
        scratch_shapes=[pltpu.VMEM((k_each, tn), BF16) for _ in range(n_a)],
        compiler_params=_cp("arbitrary", "arbitrary"),
        name=name,
    )(*args)


def _glu_body(a_ref, wg_ref, wu_ref, o_ref, wgb, wub):
    @pl.when(pl.program_id(1) == 0)
    def _():
        wgb[...] = wg_ref[...].astype(BF16)
        wub[...] = wu_ref[...].astype(BF16)

    a = a_ref[...]
    g = jnp.dot(a, wgb[...], preferred_element_type=F32)
    u = jnp.dot(a, wub[...], preferred_element_type=F32)
    o_ref[...] = (_silu(g) * u).astype(o_ref.dtype)


def _glu(a, wg, wu, layer, *, tm, name):
    rows, k = a.shape
    n_cols = wg.shape[2]
    tn = min(COL_TILE // 2, n_cols)
    wspec = pl.BlockSpec((None, k, tn), lambda j, i: (layer, 0, j))
    return pl.pallas_call(
        _glu_body,
        grid=(n_cols // tn, rows // tm),
        in_specs=[pl.BlockSpec((tm, k), lambda j, i: (i, 0)), wspec, wspec],
        out_specs=pl.BlockSpec((tm, tn), lambda j, i: (i, j)),
        out_shape=jax.ShapeDtypeStruct((rows, n_cols), BF16),
        scratch_shapes=[pltpu.VMEM((k, tn), BF16), pltpu.VMEM((k, tn), BF16)],
        compiler_params=_cp("arbitrary", "arbitrary"),
        name=name,
    )(a, wg, wu)


def _attn_prep_body(p_ref, cos_ref, sin_ref, qg_ref, kg_ref, q_ref, k_ref, v_ref):
    cos = cos_ref[...]
    sin = sin_ref[...]

    def norm_rope(xh, g):
        y = _rms(xh, g)
        return y * cos + pltpu.roll(y, A_HEAD_DIM // 2, axis=1) * sin

    for hd in range(A_HEADS):
        sl = slice(hd * A_HEAD_DIM, (hd + 1) * A_HEAD_DIM)
        q_ref[:, sl] = norm_rope(p_ref[:, sl], qg_ref[...]).astype(BF16)
    for hd in range(A_KV_HEADS):
        sl = slice(hd * A_HEAD_DIM, (hd + 1) * A_HEAD_DIM)
        k_ref[:, sl] = norm_rope(p_ref[:, A_Q + hd * A_HEAD_DIM:A_Q + (hd + 1) * A_HEAD_DIM], kg_ref[...]).astype(BF16)
    v_ref[...] = p_ref[:, A_Q + A_KV:A_Q + 2 * A_KV].astype(BF16)


def _attn_prep(proj, cos_t, sin_t, qg, kg, tiles_per_sample):
    rows = proj.shape[0]
    width = A_Q + 2 * A_KV
    return pl.pallas_call(
        _attn_prep_body,
        grid=(rows // ROW_TILE,),
        in_specs=[pl.BlockSpec((ROW_TILE, width), lambda i: (i, 0)),
                  pl.BlockSpec((ROW_TILE, A_HEAD_DIM), lambda i: (i % tiles_per_sample, 0)),
                  pl.BlockSpec((ROW_TILE, A_HEAD_DIM), lambda i: (i % tiles_per_sample, 0)),
                  pl.BlockSpec((1, A_HEAD_DIM), lambda i: (0, 0)),
                  pl.BlockSpec((1, A_HEAD_DIM), lambda i: (0, 0))],
        out_specs=[pl.BlockSpec((ROW_TILE, A_Q), lambda i: (i, 0)),
                   pl.BlockSpec((ROW_TILE, A_KV), lambda i: (i, 0)),
                   pl.BlockSpec((ROW_TILE, A_KV), lambda i: (i, 0))],
        out_shape=[jax.ShapeDtypeStruct((rows, A_Q), BF16),
                   jax.ShapeDtypeStruct((rows, A_KV), BF16),
                   jax.ShapeDtypeStruct((rows, A_KV), BF16)],
        compiler_params=_cp("arbitrary"),
        name="attn_prep",
    )(proj, cos_t, sin_t, qg, kg)


def _attn_body(sink_ref, q_ref, kp_ref, ko_ref, kn_ref, kc_ref, vp_ref, vo_ref, vn_ref, vc_ref, o_ref,
               *, ctx_blocks, lat_len):
    h = pl.program_id(1)
    j = pl.program_id(2)
    n = j - ctx_blocks
    nw = 3 * WINDOW
    m_ctx = kc_ref.shape[0]
    kw = jnp.concatenate([kp_ref[...], ko_ref[...], kn_ref[...], kc_ref[...]], axis=0)
    vw = jnp.concatenate([vp_ref[...], vo_ref[...], vn_ref[...], vc_ref[...]], axis=0)
    qi = lax.broadcasted_iota(jnp.int32, (WINDOW, nw + m_ctx), 0)
    wi = lax.broadcasted_iota(jnp.int32, (WINDOW, nw + m_ctx), 1)
    key_pos = (n - 1) * WINDOW + wi
    in_win = (jnp.abs(wi - WINDOW - qi) <= WINDOW) & (key_pos >= 0) & (key_pos < lat_len) & (n >= 0)
    valid = (wi >= nw) | in_win
    bias = jnp.where(valid, 0.0, NEG_INF).astype(F32)
    scale = A_HEAD_DIM ** -0.5
    for g in range(A_GROUP):
        sl = slice(g * A_HEAD_DIM, (g + 1) * A_HEAD_DIM)
        s = lax.dot_general(q_ref[:, sl], kw, NT_DIMS, preferred_element_type=F32) * scale + bias
        sink = sink_ref[h * A_GROUP + g]
        m = jnp.maximum(jnp.max(s, axis=-1, keepdims=True), sink)
        p = jnp.exp(s - m)
        denom = jnp.sum(p, axis=-1, keepdims=True) + jnp.exp(sink - m)
        o = jnp.dot(p.astype(BF16), vw, preferred_element_type=F32)
        o_ref[:, sl] = (o / denom).astype(o_ref.dtype)


def _attention(qn, kn, vb, sink, *, batch, seq_all, ctx_len):
    blocks = seq_all // WINDOW
    ctx_blocks = ctx_len // WINDOW
    last = blocks - 1
    qw = A_GROUP * A_HEAD_DIM

    def kv_spec(shift):
        def imap(b, h, j):
            return (b * blocks + jnp.clip(j + shift, ctx_blocks, last), h)
        return pl.BlockSpec((WINDOW, A_HEAD_DIM), imap)

    ctx_spec = pl.BlockSpec((ctx_len, A_HEAD_DIM), lambda b, h, j: (b * (seq_all // ctx_len), h))
    return pl.pallas_call(
        functools.partial(_attn_body, ctx_blocks=ctx_blocks, lat_len=seq_all - ctx_len),
        grid=(batch, A_KV_HEADS, blocks),
        in_specs=[pl.BlockSpec(memory_space=pltpu.SMEM),
                  pl.BlockSpec((WINDOW, qw), lambda b, h, j: (b * blocks + j, h)),
                  kv_spec(-1), kv_spec(0), kv_spec(1), ctx_spec,
                  kv_spec(-1), kv_spec(0), kv_spec(1), ctx_spec],
        out_specs=pl.BlockSpec((WINDOW, qw), lambda b, h, j: (b * blocks + j, h)),
        out_shape=jax.ShapeDtypeStruct((batch * seq_all, A_Q), BF16),
        compiler_params=_cp("arbitrary", "arbitrary", "arbitrary"),
        name="window_attention",
    )(sink, qn, kn, kn, kn, kn, vb, vb, vb, vb)


def _conv_body(x_ref, w_ref, o_ref, *, ctx_len, k_first_block):
    x = x_ref[...]
    t_len = x.shape[0]
    t = lax.broadcasted_iota(jnp.int32, x.shape, 0)
    half = B_CONV_W // 2
    acc = x * w_ref[half:half + 1, :]
    for d in range(-half, half + 1):
        if d == 0:
            continue
        xs = pltpu.roll(x, (-d) % t_len, axis=0)
        u = t + d
        ok = ((t < ctx_len) & (u >= 0) & (u < ctx_len)) | ((t >= ctx_len) & (u >= ctx_len) & (u < t_len))
        acc = acc + jnp.where(ok, xs, 0.0) * w_ref[half + d:half + d + 1, :]
    k_scale = jnp.where(pl.program_id(1) >= k_first_block, B_QK_DIM ** -0.5, 1.0).astype(F32)
    o_ref[...] = (_silu(acc) * k_scale).astype(o_ref.dtype)


def _mlstm_conv(proj, conv_w, *, batch, seq_all, ctx_len):
    first = (A_Q + 2 * A_KV) // LANE_BLOCK
    nblk = 2 * B_QK // LANE_BLOCK
    return pl.pallas_call(
        functools.partial(_conv_body, ctx_len=ctx_len, k_first_block=B_QK // LANE_BLOCK),
        grid=(batch, nblk),
        in_specs=[pl.BlockSpec((seq_all, LANE_BLOCK), lambda b, c: (b, first + c)),
                  pl.BlockSpec((B_CONV_W, LANE_BLOCK), lambda b, c: (0, c))],
        out_specs=pl.BlockSpec((seq_all, LANE_BLOCK), lambda b, c: (b, c)),
        out_shape=jax.ShapeDtypeStruct((batch * seq_all, 2 * B_QK), BF16),
        compiler_params=_cp("arbitrary", "arbitrary"),
        name="mlstm_conv",
    )(proj, conv_w)


def _gates_body(raw_ref, b_ref, o_ref):
    g = raw_ref[...] + b_ref[...]
    g = GATE_CAP * jnp.tanh(g / GATE_CAP)
    lane = lax.broadcasted_iota(jnp.int32, g.shape, 1)
    is_forget = (lane // B_HEADS) % 2 == 1
    log_sig = jnp.minimum(g, 0.0) - jnp.log(1.0 + jnp.exp(-jnp.abs(g)))
    o_ref[...] = jnp.where(is_forget, log_sig, g)


def _mlstm_gates(raw, gate_b):
    rows, width = raw.shape
    tr = _row_tile_size(rows, 2048)
    return pl.pallas_call(
        _gates_body,
        grid=(rows // tr,),
        in_specs=[pl.BlockSpec((tr, width), lambda i: (i, 0)),
                  pl.BlockSpec((1, width), lambda i: (0, 0))],
        out_specs=pl.BlockSpec((tr, width), lambda i: (i, 0)),
        out_shape=jax.ShapeDtypeStruct((rows, width), F32),
        compiler_params=_cp("arbitrary"),
        name="mlstm_gates",
    )(raw, gate_b)


def _mlstm_direction(q_ref, k_ref, v_ref, g_ref, o_ref, st_ref, n_ref, m_ref, d, rev):
    c_len = q_ref.shape[0]
    q = q_ref[...]
    k = k_ref[...]
    v = v_ref[...].astype(BF16)
    ig = g_ref[2 * d:2 * d + 1, :]
    lf = g_ref[2 * d + 1:2 * d + 2, :]
    r = lax.broadcasted_iota(jnp.int32, (c_len, c_len), 0)
    c = lax.broadcasted_iota(jnp.int32, (c_len, c_len), 1)
    eye = r == c
    tri = (c >= r) if rev else (c <= r)
    tri_t = (r >= c) if rev else (r <= c)
    lf_col = jnp.sum(jnp.where(eye, lf, 0.0), axis=1, keepdims=True)
    ig_col = jnp.sum(jnp.where(eye, ig, 0.0), axis=1, keepdims=True)
    b_col = jnp.sum(jnp.where(tri, lf, 0.0), axis=1, keepdims=True)
    b_row = jnp.sum(jnp.where(tri_t, lf_col, 0.0), axis=0, keepdims=True)
    m_prev = m_ref[d]
    d_st = b_col + m_prev
    d_in = jnp.where(tri, b_col - b_row + ig, NEG_INF)
    m_t = jnp.maximum(d_st, jnp.max(d_in, axis=1, keepdims=True))
    s = jnp.exp(d_in - m_t) * lax.dot_general(q, k, NT_DIMS, preferred_element_type=F32)
    w_st = jnp.exp(d_st - m_t)
    st = st_ref[d]
    num = (jnp.dot(s.astype(BF16), v, preferred_element_type=F32)
           + w_st * jnp.dot(q, st.astype(BF16), preferred_element_type=F32))
    qn = jnp.sum(q.astype(F32) * n_ref[d], axis=1, keepdims=True)
    den = jnp.sum(s, axis=1, keepdims=True) + w_st * qn
    o_ref[...] = num / jnp.maximum(jnp.abs(den), jnp.exp(-m_t))
    b_end = jnp.sum(lf, axis=1, keepdims=True)
    d_up = b_end - b_col + ig_col
    m_new = jnp.maximum(b_end + m_prev, jnp.max(d_up, axis=0, keepdims=True))
    w_up = jnp.exp(d_up - m_new)
    a = jnp.exp(b_end + m_prev - m_new)
    kw = k.astype(F32) * w_up
    st_ref[d] = a * st + lax.dot_general(kw.astype(BF16), v, TN_DIMS, preferred_element_type=F32)
    n_ref[d] = a * n_ref[d] + jnp.sum(kw, axis=0, keepdims=True)
    m_ref[d] = m_new


def _mlstm_body(qf, kf, vf, gf, qb, kb, vb, gb, of, ob, st_ref, n_ref, m_ref):
    @pl.when(pl.program_id(2) == 0)
    def _():
        st_ref[...] = jnp.zeros_like(st_ref)
        n_ref[...] = jnp.zeros_like(n_ref)
        m_ref[...] = jnp.zeros_like(m_ref)

    _mlstm_direction(qf, kf, vf, gf, of, st_ref, n_ref, m_ref, 0, False)
    _mlstm_direction(qb, kb, vb, gb, ob, st_ref, n_ref, m_ref, 1, True)


def _scan_orders(n_chunks, ctx_chunks):
    fwd = lambda j: j
    bwd = lambda j: jnp.where(j < ctx_chunks, ctx_chunks - 1 - j, n_chunks - 1 - (j - ctx_chunks))
    return fwd, bwd


def _mlstm_scan(qk, proj, gates_t, *, batch, seq_all, ctx_len):
    ch = MLSTM_CHUNK
    n_chunks = seq_all // ch
    fwd, bwd = _scan_orders(n_chunks, ctx_len // ch)
    v_first = (A_Q + 2 * A_KV + 2 * B_QK) // B_V_DIM

    def specs(order):
        return [pl.BlockSpec((ch, B_QK_DIM), lambda b, h, j: (b * n_chunks + order(j), h)),
                pl.BlockSpec((ch, B_QK_DIM), lambda b, h, j: (b * n_chunks + order(j), B_HEADS + h)),
                pl.BlockSpec((ch, B_V_DIM), lambda b, h, j: (b * n_chunks + order(j), v_first + h)),
                pl.BlockSpec((None, None, 4, ch), lambda b, h, j: (b, h, 0, order(j)))]

    def out_spec(order):
        return pl.BlockSpec((ch, B_V_DIM), lambda b, h, j: (b * n_chunks + order(j), h))

    out = jax.ShapeDtypeStruct((batch * seq_all, B_V), F32)
    return pl.pallas_call(
        _mlstm_body,
        grid=(batch, B_HEADS, n_chunks),
        in_specs=specs(fwd) + specs(bwd),
        out_specs=[out_spec(fwd), out_spec(bwd)],
        out_shape=[out, out],
        scratch_shapes=[pltpu.VMEM((2, B_QK_DIM, B_V_DIM), F32),
                        pltpu.VMEM((2, 1, B_QK_DIM), F32),
                        pltpu.VMEM((2, 1, 1), F32)],
        compiler_params=_cp("arbitrary", "arbitrary", "arbitrary"),
        name="mlstm_scan",
    )(qk, qk, proj, gates_t, qk, qk, proj, gates_t)


def _headnorm_gate_body(a_ref, b_ref, gate_ref, g_ref, o_ref, *, head_dim, gate_fn):
    for hd in range(a_ref.shape[1] // head_dim):
        sl = slice(hd * head_dim, (hd + 1) * head_dim)
        y = _rms(a_ref[:, sl] + b_ref[:, sl], g_ref[:, sl])
        o_ref[:, sl] = (y * gate_fn(gate_ref[:, sl])).astype(o_ref.dtype)


def _headnorm_gate(a, b, proj, gate_col0, norm_g, *, head_dim, gate_fn, name):
    rows, width = a.shape
    tc = min(1024, width)
    return pl.pallas_call(
        functools.partial(_headnorm_gate_body, head_dim=head_dim, gate_fn=gate_fn),
        grid=(rows // ROW_TILE, width // tc),
        in_specs=[pl.BlockSpec((ROW_TILE, tc), lambda i, c: (i, c)),
                  pl.BlockSpec((ROW_TILE, tc), lambda i, c: (i, c)),
                  pl.BlockSpec((ROW_TILE, tc), lambda i, c: (i, gate_col0 // tc + c)),
                  pl.BlockSpec((1, tc), lambda i, c: (0, c))],
        out_specs=pl.BlockSpec((ROW_TILE, tc), lambda i, c: (i, c)),
        out_shape=jax.ShapeDtypeStruct((rows, width), BF16),
        compiler_params=_cp("arbitrary", "arbitrary"),
        name=name,
    )(a, b, proj, norm_g)


def _hgrn_direction(q_ref, i_ref, f_ref, lb_ref, o_ref, st_ref, d, rev):
    rows = q_ref.shape[0]
    ch = HGRN_CHUNK
    lb = lb_ref[d:d + 1, :]
    q = _silu(q_ref[...])
    f = lb + (1.0 - lb) * _sigmoid(f_ref[...])
    kk = 1.0 - f
    a_cum = jnp.log(f)
    pos = lax.broadcasted_iota(jnp.int32, a_cum.shape, 0) % ch
    sh = 1
    while sh < ch:
        if rev:
            a_cum = a_cum + jnp.where(pos < ch - sh, pltpu.roll(a_cum, rows - sh, axis=0), 0.0)
        else:
            a_cum = a_cum + jnp.where(pos >= sh, pltpu.roll(a_cum, sh, axis=0), 0.0)
        sh *= 2
    n_chunks = rows // ch
    end_row = 0 if rev else ch - 1
    mid_row = ch // 2
    width = a_cum.shape[1]

    def per_chunk_row(row):
        return jnp.concatenate([jnp.broadcast_to(a_cum[ci * ch + row:ci * ch + row + 1, :], (ch, width))
                                for ci in range(n_chunks)], axis=0)

    a_mid = per_chunk_row(mid_row)
    a_end = per_chunk_row(end_row)
    v = i_ref[...].astype(BF16)
    q_in = (q * jnp.exp(a_cum)).astype(BF16)
    qh = (q * jnp.exp(a_cum - a_mid)).astype(BF16)
    kh = (kk * jnp.exp(a_mid - a_cum)).astype(BF16)
    ke = (kk * jnp.exp(a_end - a_cum)).astype(BF16)
    r = lax.broadcasted_iota(jnp.int32, (rows, rows), 0)
    c = lax.broadcasted_iota(jnp.int32, (rows, rows), 1)
    visible = (r // ch == c // ch) & ((c >= r) if rev else (c <= r))
    order = range(n_chunks - 1, -1, -1) if rev else range(n_chunks)
    for hd in range(HGRN_HEADS):
        cs = slice(hd * C_HEAD_DIM, (hd + 1) * C_HEAD_DIM)
        att = jnp.where(visible, lax.dot_general(qh[:, cs], kh[:, cs], NT_DIMS, preferred_element_type=F32), 0.0)
        o_local = jnp.dot(att.astype(BF16), v[:, cs], preferred_element_type=F32)
        updates = {ci: lax.dot_general(v[ci * ch:(ci + 1) * ch, cs], ke[ci * ch:(ci + 1) * ch, cs], TN_DIMS,
                                       preferred_element_type=F32) for ci in order}
        st = st_ref[d, hd]
        for ci in order:
            rs = slice(ci * ch, (ci + 1) * ch)
            o_ref[rs, cs] = o_local[rs, :] + lax.dot_general(q_in[rs, cs], st.astype(BF16), NT_DIMS,
                                                              preferred_element_type=F32)
            st = jnp.exp(a_cum[ci * ch + end_row:ci * ch + end_row + 1, cs]) * st + updates[ci]
        st_ref[d, hd] = st


def _hgrn_body(qf, vf, ff, qb, vb, fb, lb_ref, of, ob, st_ref):
    @pl.when(pl.program_id(2) == 0)
    def _():
        st_ref[...] = jnp.zeros_like(st_ref)

    _hgrn_direction(qf, vf, ff, lb_ref, of, st_ref, 0, False)
    _hgrn_direction(qb, vb, fb, lb_ref, ob, st_ref, 1, True)


def _hgrn_scan(proj, lb, *, batch, seq_all, ctx_len, d_model):
    blk = HGRN_BLOCK
    n_blocks = seq_all // blk
    fwd, bwd = _scan_orders(n_blocks, ctx_len // blk)
    wcols = HGRN_HEADS * C_HEAD_DIM
    per = d_model // wcols

    def spec(order, part):
        return pl.BlockSpec((blk, wcols), lambda b, h, j: (b * n_blocks + order(j), part * per + h))

    def out_spec(order):
        return pl.BlockSpec((blk, wcols), lambda b, h, j: (b * n_blocks + order(j), h))

    out = jax.ShapeDtypeStruct((batch * seq_all, d_model), F32)
    return pl.pallas_call(
        _hgrn_body,
        grid=(batch, per, n_blocks),
        in_specs=[spec(fwd, 0), spec(fwd, 1), spec(fwd, 3), spec(bwd, 0), spec(bwd, 1), spec(bwd, 4),
                  pl.BlockSpec((2, wcols), lambda b, h, j: (0, h))],
        out_specs=[out_spec(fwd), out_spec(bwd)],
        out_shape=[out, out],
        scratch_shapes=[pltpu.VMEM((2, HGRN_HEADS, C_HEAD_DIM, C_HEAD_DIM), F32)],
        compiler_params=_cp("arbitrary", "arbitrary", "arbitrary"),
        name="hgrn_scan",
    )(proj, proj, proj, proj, proj, proj, lb)


def _pack_halves(v):
    half = v.shape[1] // 2
    bits = lax.bitcast_convert_type(v.astype(BF16).astype(F32), jnp.uint32)
    return (bits[:, :half] >> 16) | (bits[:, half:] & jnp.uint32(0xFFFF0000))


def _unpack_halves(p):
    lo = lax.bitcast_convert_type(p << 16, F32)
    hi = lax.bitcast_convert_type(p & jnp.uint32(0xFFFF0000), F32)
    return lo, hi


def _ffn_router_body(x_ref, g_ref, mod_ref, rwt_ref, rb_ref, hb_ref, hp_ref, idx_ref, wgt_ref, rank_ref, cnt_ref):
    @pl.when(pl.program_id(0) == 0)
    def _():
        cnt_ref[...] = jnp.zeros_like(cnt_ref)

    y = _rms(x_ref[...], g_ref[...])
    h2 = y * (1.0 + mod_ref[4:5, :]) + mod_ref[3:4, :]
    hb = h2.astype(BF16)
    hb_ref[...] = hb
    hp_ref[...] = _pack_halves(hb)
    tm = h2.shape[0]
    logits = lax.dot_general(rwt_ref[...], h2, NT_DIMS, precision=lax.Precision.HIGHEST,
                             preferred_element_type=F32)
    s = _sigmoid(logits)
    sel = s + rb_ref[...]
    iota_g = lax.broadcasted_iota(jnp.int32, (GROUP_SIZE, tm), 0)
    group_scores = []
    for g in range(N_GROUPS):
        xg = sel[g * GROUP_SIZE:(g + 1) * GROUP_SIZE, :]
        m1 = jnp.max(xg, axis=0, keepdims=True)
        i1 = jnp.min(jnp.where(xg == m1, iota_g, GROUP_SIZE), axis=0, keepdims=True)
        m2 = jnp.max(jnp.where(iota_g == i1, NEG_INF, xg), axis=0, keepdims=True)
        group_scores.append(m1 + m2)
    gsc = jnp.concatenate(group_scores, axis=0)
    iota_n = lax.broadcasted_iota(jnp.int32, (N_GROUPS, tm), 0)
    gmask = jnp.zeros((N_GROUPS, tm), F32)
    for _ in range(TOPK_GROUPS):
        m = jnp.max(gsc, axis=0, keepdims=True)
        i = jnp.min(jnp.where(gsc == m, iota_n, N_GROUPS), axis=0, keepdims=True)
        hit = iota_n == i
        gmask = jnp.where(hit, 1.0, gmask)
        gsc = jnp.where(hit, NEG_INF, gsc)
    emask = jnp.concatenate([jnp.broadcast_to(gmask[g:g + 1, :], (GROUP_SIZE, tm)) for g in range(N_GROUPS)], axis=0)
    cur = jnp.where(emask > 0.0, sel, NEG_INF)
    iota_e = lax.broadcasted_iota(jnp.int32, (N_EXPERTS, tm), 0)
    idx_rows, w_rows, hits = [], [], []
    for _ in range(TOP_K):
        m = jnp.max(cur, axis=0, keepdims=True)
        i = jnp.min(jnp.where(cur == m, iota_e, N_EXPERTS), axis=0, keepdims=True)
        hit = iota_e == i
        idx_rows.append(i)
        hits.append(hit)
        w_rows.append(jnp.sum(jnp.where(hit, s, 0.0), axis=0, keepdims=True))
        cur = jnp.where(hit, NEG_INF, cur)
    w = jnp.concatenate(w_rows, axis=0)
    idx_ref[...] = jnp.concatenate(idx_rows, axis=0)
    wgt_ref[...] = ROUTED_SCALE * w / jnp.sum(w, axis=0, keepdims=True)
    chosen = jnp.zeros((N_EXPERTS, tm), F32)
    for hit in hits:
        chosen = jnp.where(hit, 1.0, chosen)
    earlier = (lax.broadcasted_iota(jnp.int32, (tm, tm), 0) < lax.broadcasted_iota(jnp.int32, (tm, tm), 1))
    before = cnt_ref[...] + jnp.dot(chosen.astype(BF16), earlier.astype(BF16), preferred_element_type=F32)
    rank_ref[...] = jnp.concatenate(
        [jnp.sum(jnp.where(hit, before, 0.0), axis=0, keepdims=True) for hit in hits], axis=0).astype(jnp.int32)
    cnt_ref[...] = cnt_ref[...] + jnp.sum(chosen, axis=1, keepdims=True)


def _ffn_router(x, g, mod, sel, router_wt, router_b, layer):
    rows, d = x.shape
    return pl.pallas_call(
        _ffn_router_body,
        grid=(rows // ROW_TILE,),
        in_specs=[pl.BlockSpec((ROW_TILE, d), lambda i: (i, 0)),
                  pl.BlockSpec((1, d), lambda i: (0, 0)),
                  pl.BlockSpec((None, 6, d), lambda i: (sel(i), 0, 0)),
                  pl.BlockSpec((None, N_EXPERTS, d), lambda i: (layer, 0, 0)),
                  pl.BlockSpec((None, N_EXPERTS, 1), lambda i: (layer, 0, 0))],
        out_specs=[pl.BlockSpec((ROW_TILE, d), lambda i: (i, 0)),
                   pl.BlockSpec((ROW_TILE, d // 2), lambda i: (i, 0)),
                   pl.BlockSpec((TOP_K, ROW_TILE), lambda i: (0, i)),
                   pl.BlockSpec((TOP_K, ROW_TILE), lambda i: (0, i)),
                   pl.BlockSpec((TOP_K, ROW_TILE), lambda i: (0, i)),
                   pl.BlockSpec((N_EXPERTS, 1), lambda i: (0, 0))],
        out_shape=[jax.ShapeDtypeStruct((rows, d), BF16),
                   jax.ShapeDtypeStruct((rows, d // 2), jnp.uint32),
                   jax.ShapeDtypeStruct((TOP_K, rows), jnp.int32),
                   jax.ShapeDtypeStruct((TOP_K, rows), F32),
                   jax.ShapeDtypeStruct((TOP_K, rows), jnp.int32),
                   jax.ShapeDtypeStruct((N_EXPERTS, 1), F32)],
        compiler_params=_cp("arbitrary"),
        name=f"ffn_router{layer}",
    )(x, g, mod, router_wt, router_b)


def _dispatch(idx_t, rank_t, counts):
    tm = MOE_TILE
    slots = idx_t.shape[0] * idx_t.shape[1]
    n_tiles = slots // tm
    counts = counts.reshape(-1).astype(jnp.int32)
    ends = jnp.cumsum(counts)
    starts = ends - counts
    is_e = idx_t[:, :, None] == jnp.arange(N_EXPERTS, dtype=jnp.int32)
    pos = (jnp.sum(jnp.where(is_e, starts, 0), axis=-1) + rank_t).T.reshape(-1)
    cuts = jnp.sort(jnp.concatenate([jnp.arange(n_tiles, dtype=jnp.int32) * tm, ends[:-1]]))
    nxt = jnp.concatenate([cuts[1:], jnp.full((1,), slots, jnp.int32)])
    v_tile = jnp.minimum(cuts // tm, n_tiles - 1)
    v_exp = jnp.minimum(jnp.sum((ends[None, :] <= cuts[:, None]).astype(jnp.int32), axis=1), N_EXPERTS - 1)
    v_lo = cuts - v_tile * tm
    v_hi = nxt - v_tile * tm
    return pos.astype(jnp.int32), v_tile.astype(jnp.int32), v_exp.astype(jnp.int32), v_lo, v_hi


def _push_body(pos_ref, h_ref, xs_hbm, sem):
    i = pl.program_id(0)
    tt = h_ref.shape[0]
    base = i * tt * TOP_K

    def issue(r, carry):
        src = h_ref.at[pl.ds(r, 1)]
        for k in range(TOP_K):
            pltpu.make_async_copy(src, xs_hbm.at[pl.ds(pos_ref[base + r * TOP_K + k], 1)], sem).start()
        return carry

    lax.fori_loop(0, tt, issue, 0)
    for _ in range(TOP_K):
        pltpu.make_async_copy(h_ref, xs_hbm.at[pl.ds(0, tt)], sem).wait()


def _push(h2f, pos, layer):
    rows, d = h2f.shape
    tt = PUSH_TOKENS
    grid_spec = pltpu.PrefetchScalarGridSpec(
        num_scalar_prefetch=1,
        grid=(rows // tt,),
        in_specs=[pl.BlockSpec((tt, d), lambda i, ps: (i, 0))],
        out_specs=pl.BlockSpec(memory_space=pl.ANY),
        scratch_shapes=[pltpu.SemaphoreType.DMA(())],
    )
    return pl.pallas_call(
        _push_body,
        grid_spec=grid_spec,
        out_shape=jax.ShapeDtypeStruct((rows * TOP_K, d), h2f.dtype),
        compiler_params=_cp("arbitrary"),
        name=f"push{layer}",
    )(pos, h2f)


def _expert_body(vt_ref, ve_ref, lo_ref, hi_ref, x_ref, wg_ref, wu_ref, wd_ref, y_ref, wgb, wub, wdb):
    v = pl.program_id(0)
    lo = lo_ref[v]
    hi = hi_ref[v]
    prev = jnp.maximum(v - 1, 0)

    @pl.when((v == 0) | (ve_ref[v] != ve_ref[prev]))
    def _():
        wgb[...] = wg_ref[...].astype(BF16)
        wub[...] = wu_ref[...].astype(BF16)
        wdb[...] = wd_ref[...].astype(BF16)

    @pl.when(hi > lo)
    def _():
        x_lo, x_hi = _unpack_halves(x_ref[...])
        x_lo = x_lo.astype(BF16)
        x_hi = x_hi.astype(BF16)
        half = x_lo.shape[1]
        g = (jnp.dot(x_lo, wgb[:half, :], preferred_element_type=F32)
             + jnp.dot(x_hi, wgb[half:, :], preferred_element_type=F32))
        u = (jnp.dot(x_lo, wub[:half, :], preferred_element_type=F32)
             + jnp.dot(x_hi, wub[half:, :], preferred_element_type=F32))
        hid = (_silu(g) * u).astype(BF16)
        y = _pack_halves(jnp.dot(hid, wdb[...], preferred_element_type=F32))
        r = lax.broadcasted_iota(jnp.int32, (y.shape[0], 1), 0)
        mine = (r >= lo) & (r < hi)

        @pl.when(lo == 0)
        def _():
            y_ref[...] = jnp.where(mine, y, jnp.uint32(0))

        @pl.when(lo > 0)
        def _():
            y_ref[...] = jnp.where(mine, y, y_ref[...])


def _experts(xs, v_tile, v_exp, v_lo, v_hi, wg, wu, wd, layer):
    slots, dp = xs.shape
    d = wg.shape[2]
    de = wg.shape[3]
    tm = MOE_TILE
    grid_spec = pltpu.PrefetchScalarGridSpec(
        num_scalar_prefetch=4,
        grid=(v_tile.shape[0],),
        in_specs=[pl.BlockSpec((tm, dp), lambda v, vt, ve, lo, hi: (vt[v], 0)),
                  pl.BlockSpec((None, None, d, de), lambda v, vt, ve, lo, hi: (layer, ve[v], 0, 0)),
                  pl.BlockSpec((None, None, d, de), lambda v, vt, ve, lo, hi: (layer, ve[v], 0, 0)),
                  pl.BlockSpec((None, None, de, d), lambda v, vt, ve, lo, hi: (layer, ve[v], 0, 0))],
        out_specs=pl.BlockSpec((tm, dp), lambda v, vt, ve, lo, hi: (vt[v], 0)),
        scratch_shapes=[pltpu.VMEM((d, de), BF16), pltpu.VMEM((d, de), BF16), pltpu.VMEM((de, d), BF16)],
    )
    return pl.pallas_call(
        _expert_body,
        grid_spec=grid_spec,
        out_shape=jax.ShapeDtypeStruct((slots, dp), jnp.uint32),
        compiler_params=_cp("arbitrary"),
        name=f"experts{layer}",
    )(v_tile, v_exp, v_lo, v_hi, xs, wg, wu, wd)


def _combine_body(pos_ref, w_ref, y_hbm, o_ref, buf, sem):
    i = pl.program_id(0)
    tt = o_ref.shape[0]
    n = TOP_K * tt
    slot = i % 2
    half = buf.shape[3]
    group = 8

    def issue_rows(step, into, r0, rows):
        for j in range(rows):
            for k in range(TOP_K):
                pltpu.make_async_copy(y_hbm.at[pl.ds(pos_ref[step * n + (r0 + j) * TOP_K + k], 1)],
                                      buf.at[into, k, pl.ds(r0 + j, 1)], sem.at[into]).start()

    @pl.when(i == 0)
    def _():
        def first(g, carry):
            issue_rows(0, 0, g * group, group)
            return carry

        lax.fori_loop(0, tt // group, first, 0)

    for k in range(TOP_K):
        pltpu.make_async_copy(y_hbm.at[pl.ds(0, tt)], buf.at[slot, k], sem.at[slot]).wait()
    last = pl.num_programs(0) - 1
    nxt = jnp.minimum(i + 1, last)

    def body(g, carry):
        r0 = pl.multiple_of(g * group, group)
        issue_rows(nxt, 1 - slot, r0, group)
        w = w_ref[pl.ds(r0, group), :]
        acc_lo, acc_hi = _unpack_halves(buf[slot, 0, pl.ds(r0, group), :])
        acc_lo = w[:, 0:1] * acc_lo
        acc_hi = w[:, 0:1] * acc_hi
        for k in range(1, TOP_K):
            y_lo, y_hi = _unpack_halves(buf[slot, k, pl.ds(r0, group), :])
            acc_lo = acc_lo + w[:, k:k + 1] * y_lo
            acc_hi = acc_hi + w[:, k:k + 1] * y_hi
        o_ref[pl.ds(r0, group), :half] = acc_lo
        o_ref[pl.ds(r0, group), half:] = acc_hi
        return carry

    lax.fori_loop(0, tt // group, body, 0)

    @pl.when(i == last)
    def _():
        for k in range(TOP_K):
            pltpu.make_async_copy(y_hbm.at[pl.ds(0, tt)], buf.at[1 - slot, k], sem.at[1 - slot]).wait()


def _combine(ys, pos_flat, wgt, rows, layer):
    dp = ys.shape[1]
    d = 2 * dp
    tt = COMBINE_TOKENS
    grid_spec = pltpu.PrefetchScalarGridSpec(
        num_scalar_prefetch=1,
        grid=(rows // tt,),
        in_specs=[pl.BlockSpec((tt, TOP_K), lambda i, ps: (i, 0)),
                  pl.BlockSpec(memory_space=pl.ANY)],
        out_specs=pl.BlockSpec((tt, d), lambda i, ps: (i, 0)),
        scratch_shapes=[pltpu.VMEM((2, TOP_K, tt, dp), jnp.uint32), pltpu.SemaphoreType.DMA((2,))],
    )
    return pl.pallas_call(
        _combine_body,
        grid_spec=grid_spec,
        out_shape=jax.ShapeDtypeStruct((rows, d), F32),
        compiler_params=_cp("arbitrary"),
        name=f"combine{layer}",
    )(pos_flat, wgt, ys)


def _moe(x1, mod, sel, layer, norm_g, router_w, router_b, exp_wg, exp_wu, exp_wd, s_gate, s_up, s_down, tm_big):
    rows, d = x1.shape
    router_wt = jnp.swapaxes(router_w, 1, 2)
    h2b, h2p, idx_t, wgt_t, rank_t, counts = _ffn_router(x1, norm_g[layer][None], mod, sel, router_wt,
                                                         router_b[:, :, None], layer)
    pos, v_tile, v_exp, v_lo, v_hi = _dispatch(idx_t, rank_t, counts)
    xs = _push(h2p, pos, layer)
    ys = _experts(xs, v_tile, v_exp, v_lo, v_hi, exp_wg, exp_wu, exp_wd, layer)
    routed = _combine(ys, pos, wgt_t.T, rows, layer)
    hs = _glu(h2b, s_gate, s_up, layer, tm=tm_big, name=f"shared_glu{layer}")
    return _linear([hs], s_down, layer, n_cols=d, tm=ROW_TILE, out_dtype=F32, name=f"shared_down{layer}",
                   res=(x1, mod, 5, routed, sel), tn=4 * COL_TILE)


def _rope_tables(ctx_len, lat_len):
    rows = lat_len // GRID_W
    pos_r = jnp.repeat(jnp.arange(rows), GRID_W).astype(F32)
    pos_c = jnp.tile(jnp.arange(GRID_W), rows).astype(F32)
    n = A_HEAD_DIM // 4
    inv = ROPE_BASE ** (-jnp.arange(n, dtype=F32) / n)
    ang = jnp.concatenate([pos_r[:, None] * inv, pos_c[:, None] * inv], axis=-1)
    cos, sin = jnp.cos(ang), jnp.sin(ang)
    cos_t = jnp.concatenate([jnp.ones((ctx_len, A_HEAD_DIM), F32), jnp.concatenate([cos, cos], axis=-1)], axis=0)
    sin_t = jnp.concatenate([jnp.zeros((ctx_len, A_HEAD_DIM), F32), jnp.concatenate([-sin, sin], axis=-1)], axis=0)
    return cos_t, sin_t


def kernel(x, c, ctx, c_ctx, ada_w, ada_b, norm_mix_g, norm_ffn_g, ab_w_in, ab_w_out, a_q_norm_g, a_k_norm_g,
           a_sink, b_conv_w, b_gate_b, b_norm_g, c_w_in, c_w_out, c_lb_logits, c_norm_g, router_w, router_b,
           exp_w_gate, exp_w_up, exp_w_down, shared_w_gate, shared_w_up, shared_w_down):
    batch, lat_len, d = x.shape
    ctx_len = ctx.shape[1]
    depth = ada_w.shape[0]
    seq_all = ctx_len + lat_len
    rows = batch * seq_all
    assert ctx_len % ROW_TILE == 0 and lat_len % ROW_TILE == 0 and ctx_len % MLSTM_CHUNK == 0
    assert lat_len % GRID_W == 0 and d % (HGRN_HEADS * C_HEAD_DIM) == 0
    tiles_per_sample = seq_all // ROW_TILE
    ctx_tiles = ctx_len // ROW_TILE
    tm_big = _row_tile_size(rows, 1152)

    def sel(i):
        return jnp.where(i % tiles_per_sample < ctx_tiles, batch, i // tiles_per_sample)

    cos_t, sin_t = _rope_tables(ctx_len, lat_len)
    lb_soft = jax.nn.softmax(c_lb_logits.astype(F32), axis=0)
    lower_bounds = jnp.cumsum(lb_soft, axis=0) - lb_soft[0:1]
    cond = jnp.concatenate([jax.nn.silu(c), jax.nn.silu(c_ctx)[None]], axis=0)
    cond = jnp.pad(cond, ((0, 16 - cond.shape[0]), (0, 0))).astype(BF16)

    xa = jnp.concatenate([ctx, x], axis=1).reshape(rows, d)
    for layer in range(depth):
        mod = _ada(cond, ada_w, ada_b, layer)[:batch + 1].reshape(batch + 1, 6, d)
        h = _modulate(xa, norm_mix_g[layer][None], mod, sel, shift_row=0, scale_row=1, name=f"mod_mix{layer}")
        if layer % 2 == 0:
            e = layer // 2
            proj = _linear([h], ab_w_in, e, n_cols=AB_MAIN, tm=tm_big, out_dtype=F32, name=f"ab_in{layer}")
            w_gates = jnp.pad(ab_w_in[e][:, AB_MAIN:], ((0, 0), (0, LANE_BLOCK - AB_GATES)))[None]
            graw = _linear([h], w_gates, 0, n_cols=LANE_BLOCK, tm=tm_big, out_dtype=F32, name=f"ab_gates{layer}")
            qn, kn, vb = _attn_prep(proj, cos_t, sin_t, a_q_norm_g[e][None], a_k_norm_g[e][None], tiles_per_sample)
            ya = _attention(qn, kn, vb, a_sink[e], batch=batch, seq_all=seq_all, ctx_len=ctx_len)
            qk = _mlstm_conv(proj, b_conv_w[e], batch=batch, seq_all=seq_all, ctx_len=ctx_len)
            gate_b = jnp.pad(b_gate_b[e], (0, LANE_BLOCK - AB_GATES))[None]
            gates = _mlstm_gates(graw, gate_b)[:, :AB_GATES]
            gates_t = gates.reshape(batch, seq_all, 4, B_HEADS).transpose(0, 3, 2, 1)
            hf, hb = _mlstm_scan(qk, proj, gates_t, batch=batch, seq_all=seq_all, ctx_len=ctx_len)
            yb = _headnorm_gate(hf, hb, proj, AB_MAIN - B_V, b_norm_g[e][None], head_dim=B_V_DIM,
                                gate_fn=_sigmoid, name="mlstm_out")
            xa = _linear([ya, yb], ab_w_out, e, n_cols=d, tm=ROW_TILE, out_dtype=F32, name=f"ab_out{layer}",
                         res=(xa, mod, 2, None, sel), tn=2 * COL_TILE)
        else:
            o = layer // 2
            proj = _linear([h], c_w_in, o, n_cols=5 * d, tm=tm_big, out_dtype=F32, name=f"c_in{layer}")
            of, ob = _hgrn_scan(proj, lower_bounds[layer], batch=batch, seq_all=seq_all, ctx_len=ctx_len, d_model=d)
            yc = _headnorm_gate(of, ob, proj, 2 * d, c_norm_g[o][None], head_dim=C_HEAD_DIM, gate_fn=_silu,
                                name="hgrn_out")
            xa = _linear([yc], c_w_out, o, n_cols=d, tm=ROW_TILE, out_dtype=F32, name=f"c_out{layer}",
                         res=(xa, mod, 2, None, sel), tn=2 * COL_TILE)
        xa = _moe(xa, mod, sel, layer, norm_ffn_g, router_w, router_b, exp_w_gate, exp_w_up, exp_w_down,
                  shared_w_gate, shared_w_up, shared_w_down, tm_big)
    return xa.reshape(batch, seq_all, d)[:, ctx_len:]
```

```python
import functools

import jax
import jax.numpy as jnp
from jax import lax
from jax.experimental import pallas as pl
from jax.experimental.pallas import tpu as pltpu

F32 = jnp.float32
BF16 = jnp.bfloat16
NEG_INF = float("-inf")

EPS = 1e-6
GRID_W = 64
ROPE_BASE = 10000.0

A_HEADS = 16
A_KV_HEADS = 4
A_GROUP = A_HEADS // A_KV_HEADS
A_HEAD_DIM = 128
WINDOW = 128
A_Q = A_HEADS * A_HEAD_DIM
A_KV = A_KV_HEADS * A_HEAD_DIM

B_HEADS = 8
B_QK_DIM = 128
B_V_DIM = 256
B_CONV_W = 5
GATE_CAP = 15.0
B_QK = B_HEADS * B_QK_DIM
B_V = B_HEADS * B_V_DIM
AB_MAIN = A_Q + 2 * A_KV + 2 * B_QK + 2 * B_V
AB_GATES = 4 * B_HEADS

C_HEAD_DIM = 128

N_EXPERTS = 64
TOP_K = 8
N_GROUPS = 8
TOPK_GROUPS = 4
GROUP_SIZE = N_EXPERTS // N_GROUPS
ROUTED_SCALE = 2.5

VMEM_LIMIT_BYTES = 56 * 1024 * 1024
LANE_BLOCK = 128
ROW_TILE = 256
COL_TILE = 512
MLSTM_CHUNK = 256
HGRN_BLOCK = 256
HGRN_CHUNK = 32
HGRN_HEADS = 4
MOE_TILE = 256
PUSH_TOKENS = 128
COMBINE_TOKENS = 64

NT_DIMS = (((1,), (1,)), ((), ()))
TN_DIMS = (((0,), (0,)), ((), ()))


def _cp(*sem):
    return pltpu.CompilerParams(dimension_semantics=sem, vmem_limit_bytes=VMEM_LIMIT_BYTES)


def _sigmoid(x):
    return jax.nn.sigmoid(x)


def _silu(x):
    return x * jax.nn.sigmoid(x)


def _rms(x, g):
    return x * lax.rsqrt(jnp.mean(x * x, axis=-1, keepdims=True) + EPS) * g


def _row_tile_size(rows, limit):
    best = 16
    for t in range(16, limit + 1, 16):
        if rows % t == 0:
            best = t
    return best


def _ada_body(a_ref, w_ref, b_ref, o_ref):
    o_ref[...] = jnp.dot(a_ref[...], w_ref[...].astype(BF16), preferred_element_type=F32) + b_ref[...]


def _ada(a, ada_w, ada_b, layer):
    depth, d, d6 = ada_w.shape
    rows = a.shape[0]
    return pl.pallas_call(
        _ada_body,
        grid=(d6 // COL_TILE,),
        in_specs=[pl.BlockSpec((rows, d), lambda j: (0, 0)),
                  pl.BlockSpec((None, d, COL_TILE), lambda j: (layer, 0, j)),
                  pl.BlockSpec((None, 1, COL_TILE), lambda j: (layer, 0, j))],
        out_specs=pl.BlockSpec((rows, COL_TILE), lambda j: (0, j)),
        out_shape=jax.ShapeDtypeStruct((rows, d6), F32),
        compiler_params=_cp("arbitrary"),
        name=f"ada{layer}",
    )(a, ada_w, ada_b.reshape(depth, 1, d6))


def _modulate_body(x_ref, g_ref, mod_ref, o_ref, *, shift_row, scale_row):
    y = _rms(x_ref[...], g_ref[...])
    o_ref[...] = (y * (1.0 + mod_ref[scale_row:scale_row + 1, :])
                  + mod_ref[shift_row:shift_row + 1, :]).astype(o_ref.dtype)


def _modulate(x, g, mod, sel, *, shift_row, scale_row, name):
    rows, d = x.shape
    return pl.pallas_call(
        functools.partial(_modulate_body, shift_row=shift_row, scale_row=scale_row),
        grid=(rows // ROW_TILE,),
        in_specs=[pl.BlockSpec((ROW_TILE, d), lambda i: (i, 0)),
                  pl.BlockSpec((1, d), lambda i: (0, 0)),
                  pl.BlockSpec((None, 6, d), lambda i: (sel(i), 0, 0))],
        out_specs=pl.BlockSpec((ROW_TILE, d), lambda i: (i, 0)),
        out_shape=jax.ShapeDtypeStruct((rows, d), BF16),
        compiler_params=_cp("arbitrary"),
        name=name,
    )(x, g, mod)


def _linear_body(*refs, n_a, gate_row, has_extra):
    a_refs = refs[:n_a]
    w_refs = refs[n_a:2 * n_a]
    p = 2 * n_a
    x_ref = mod_ref = e_ref = None
    if gate_row is not None:
        x_ref, mod_ref = refs[p], refs[p + 1]
        p += 2
        if has_extra:
            e_ref = refs[p]
            p += 1
    o_ref = refs[p]
    wb_refs = refs[p + 1:]

    @pl.when(pl.program_id(1) == 0)
    def _():
        for w_ref, wb in zip(w_refs, wb_refs):
            wb[...] = w_ref[...].astype(BF16)

    acc = jnp.dot(a_refs[0][...], wb_refs[0][...], preferred_element_type=F32)
    for a_ref, wb in zip(a_refs[1:], wb_refs[1:]):
        acc = acc + jnp.dot(a_ref[...], wb[...], preferred_element_type=F32)
    if gate_row is not None:
        if e_ref is not None:
            acc = acc + e_ref[...]
        acc = x_ref[...] + mod_ref[gate_row:gate_row + 1, :] * acc
    o_ref[...] = acc.astype(o_ref.dtype)


def _linear(a_list, w, layer, *, n_cols, tm, out_dtype, name, res=None, tn=COL_TILE, row_map=None, out_rows=None):
    n_a = len(a_list)
    rows = a_list[0].shape[0]
    if row_map is None:
        row_map = lambda i: i
    else:
        rows = out_rows
    k_each = w.shape[1] // n_a
    tn = min(tn, n_cols)
    in_specs = [pl.BlockSpec((tm, k_each), lambda j, i: (row_map(i), 0)) for _ in a_list]
    for idx in range(n_a):
        in_specs.append(pl.BlockSpec((None, k_each, tn), lambda j, i, idx=idx: (layer, idx, j)))
    args = list(a_list) + [w] * n_a
    gate_row, has_extra = None, False
    if res is not None:
        x, mod, gate_row, extra, sel = res
        in_specs.append(pl.BlockSpec((tm, tn), lambda j, i: (row_map(i), j)))
        in_specs.append(pl.BlockSpec((None, 6, tn), lambda j, i: (sel(i), 0, j)))
        args += [x, mod]
        if extra is not None:
            has_extra = True
            in_specs.append(pl.BlockSpec((tm, tn), lambda j, i: (i, j)))
            args.append(extra)
    return pl.pallas_call(
        functools.partial(_linear_body, n_a=n_a, gate_row=gate_row, has_extra=has_extra),
        grid=(n_cols // tn, rows // tm),
        in_specs=in_specs,
        out_specs=pl.BlockSpec((tm, tn), lambda j, i: (i, j)),
        out_shape=jax.ShapeDtypeStruct((rows, n_cols), out_dtype),
        scratch_shapes=[pltpu.VMEM((k_each, tn), BF16) for _ in range(n_a)],
        compiler_params=_cp("arbitrary", "arbitrary"),
        name=name,
    )(*args)


def _glu_body(a_ref, wg_ref, wu_ref, o_ref, wgb, wub):
    @pl.when(pl.program_id(1) == 0)
    def _():
        wgb[...] = wg_ref[...].astype(BF16)
        wub[...] = wu_ref[...].astype(BF16)

    a = a_ref[...]
    g = jnp.dot(a, wgb[...], preferred_element_type=F32)
    u = jnp.dot(a, wub[...], preferred_element_type=F32)
    o_ref[...] = (_silu(g) * u).astype(o_ref.dtype)


def _glu(a, wg, wu, layer, *, tm, name):
    rows, k = a.shape
    n_cols = wg.shape[2]
    tn = min(COL_TILE // 2, n_cols)
    wspec = pl.BlockSpec((None, k, tn), lambda j, i: (layer, 0, j))
    return pl.pallas_call(
        _glu_body,
        grid=(n_cols // tn, rows // tm),
        in_specs=[pl.BlockSpec((tm, k), lambda j, i: (i, 0)), wspec, wspec],
        out_specs=pl.BlockSpec((tm, tn), lambda j, i: (i, j)),
        out_shape=jax.ShapeDtypeStruct((rows, n_cols), BF16),
        scratch_shapes=[pltpu.VMEM((k, tn), BF16), pltpu.VMEM((k, tn), BF16)],
        compiler_params=_cp("arbitrary", "arbitrary"),
        name=name,
    )(a, wg, wu)


def _attn_prep_body(p_ref, cos_ref, sin_ref, qg_ref, kg_ref, q_ref, k_ref, v_ref):
    cos = cos_ref[...]
    sin = sin_ref[...]

    def norm_rope(xh, g):
        y = _rms(xh, g)
        return y * cos + pltpu.roll(y, A_HEAD_DIM // 2, axis=1) * sin

    for hd in range(A_HEADS):
        sl = slice(hd * A_HEAD_DIM, (hd + 1) * A_HEAD_DIM)
        q_ref[:, sl] = norm_rope(p_ref[:, sl], qg_ref[...]).astype(BF16)
    for hd in range(A_KV_HEADS):
        sl = slice(hd * A_HEAD_DIM, (hd + 1) * A_HEAD_DIM)
        k_ref[:, sl] = norm_rope(p_ref[:, A_Q + hd * A_HEAD_DIM:A_Q + (hd + 1) * A_HEAD_DIM], kg_ref[...]).astype(BF16)
    v_ref[...] = p_ref[:, A_Q + A_KV:A_Q + 2 * A_KV].astype(BF16)


def _attn_prep(proj, cos_t, sin_t, qg, kg, tiles_per_sample):
    rows = proj.shape[0]
    width = A_Q + 2 * A_KV
    return pl.pallas_call(
        _attn_prep_body,
        grid=(rows // ROW_TILE,),
        in_specs=[pl.BlockSpec((ROW_TILE, width), lambda i: (i, 0)),
                  pl.BlockSpec((ROW_TILE, A_HEAD_DIM), lambda i: (i % tiles_per_sample, 0)),
                  pl.BlockSpec((ROW_TILE, A_HEAD_DIM), lambda i: (i % tiles_per_sample, 0)),
                  pl.BlockSpec((1, A_HEAD_DIM), lambda i: (0, 0)),
                  pl.BlockSpec((1, A_HEAD_DIM), lambda i: (0, 0))],
        out_specs=[pl.BlockSpec((ROW_TILE, A_Q), lambda i: (i, 0)),
                   pl.BlockSpec((ROW_TILE, A_KV), lambda i: (i, 0)),
                   pl.BlockSpec((ROW_TILE, A_KV), lambda i: (i, 0))],
        out_shape=[jax.ShapeDtypeStruct((rows, A_Q), BF16),
                   jax.ShapeDtypeStruct((rows, A_KV), BF16),
                   jax.ShapeDtypeStruct((rows, A_KV), BF16)],
        compiler_params=_cp("arbitrary"),
        name="attn_prep",
    )(proj, cos_t, sin_t, qg, kg)


def _attn_body(sink_ref, q_ref, kp_ref, ko_ref, kn_ref, kc_ref, vp_ref, vo_ref, vn_ref, vc_ref, o_ref,
               *, ctx_blocks, lat_len):
    h = pl.program_id(1)
    j = pl.program_id(2)
    n = j - ctx_blocks
    nw = 3 * WINDOW
    m_ctx = kc_ref.shape[0]
    kw = jnp.concatenate([kp_ref[...], ko_ref[...], kn_ref[...], kc_ref[...]], axis=0)
    vw = jnp.concatenate([vp_ref[...], vo_ref[...], vn_ref[...], vc_ref[...]], axis=0)
    qi = lax.broadcasted_iota(jnp.int32, (WINDOW, nw + m_ctx), 0)
    wi = lax.broadcasted_iota(jnp.int32, (WINDOW, nw + m_ctx), 1)
    key_pos = (n - 1) * WINDOW + wi
    in_win = (jnp.abs(wi - WINDOW - qi) <= WINDOW) & (key_pos >= 0) & (key_pos < lat_len) & (n >= 0)
    valid = (wi >= nw) | in_win
    bias = jnp.where(valid, 0.0, NEG_INF).astype(F32)
    scale = A_HEAD_DIM ** -0.5
    for g in range(A_GROUP):
        sl = slice(g * A_HEAD_DIM, (g + 1) * A_HEAD_DIM)
        s = lax.dot_general(q_ref[:, sl], kw, NT_DIMS, preferred_element_type=F32) * scale + bias
        sink = sink_ref[h * A_GROUP + g]
        m = jnp.maximum(jnp.max(s, axis=-1, keepdims=True), sink)
        p = jnp.exp(s - m)
        denom = jnp.sum(p, axis=-1, keepdims=True) + jnp.exp(sink - m)
        o = jnp.dot(p.astype(BF16), vw, preferred_element_type=F32)
        o_ref[:, sl] = (o / denom).astype(o_ref.dtype)


def _attention(qn, kn, vb, sink, *, batch, seq_all, ctx_len):
    blocks = seq_all // WINDOW
    ctx_blocks = ctx_len // WINDOW
    last = blocks - 1
    qw = A_GROUP * A_HEAD_DIM

    def kv_spec(shift):
        def imap(b, h, j):
            return (b * blocks + jnp.clip(j + shift, ctx_blocks, last), h)
        return pl.BlockSpec((WINDOW, A_HEAD_DIM), imap)

    ctx_spec = pl.BlockSpec((ctx_len, A_HEAD_DIM), lambda b, h, j: (b * (seq_all // ctx_len), h))
    return pl.pallas_call(
        functools.partial(_attn_body, ctx_blocks=ctx_blocks, lat_len=seq_all - ctx_len),
        grid=(batch, A_KV_HEADS, blocks),
        in_specs=[pl.BlockSpec(memory_space=pltpu.SMEM),
                  pl.BlockSpec((WINDOW, qw), lambda b, h, j: (b * blocks + j, h)),
                  kv_spec(-1), kv_spec(0), kv_spec(1), ctx_spec,
                  kv_spec(-1), kv_spec(0), kv_spec(1), ctx_spec],
        out_specs=pl.BlockSpec((WINDOW, qw), lambda b, h, j: (b * blocks + j, h)),
        out_shape=jax.ShapeDtypeStruct((batch * seq_all, A_Q), BF16),
        compiler_params=_cp("arbitrary", "arbitrary", "arbitrary"),
        name="window_attention",
    )(sink, qn, kn, kn, kn, kn, vb, vb, vb, vb)


def _conv_body(x_ref, w_ref, o_ref, *, ctx_len, k_first_block):
    x = x_ref[...]
    t_len = x.shape[0]
    t = lax.broadcasted_iota(jnp.int32, x.shape, 0)
    half = B_CONV_W // 2
    acc = x * w_ref[half:half + 1, :]
    for d in range(-half, half + 1):
        if d == 0:
            continue
        xs = pltpu.roll(x, (-d) % t_len, axis=0)
        u = t + d
        ok = ((t < ctx_len) & (u >= 0) & (u < ctx_len)) | ((t >= ctx_len) & (u >= ctx_len) & (u < t_len))
        acc = acc + jnp.where(ok, xs, 0.0) * w_ref[half + d:half + d + 1, :]
    k_scale = jnp.where(pl.program_id(1) >= k_first_block, B_QK_DIM ** -0.5, 1.0).astype(F32)
    o_ref[...] = (_silu(acc) * k_scale).astype(o_ref.dtype)


def _mlstm_conv(proj, conv_w, *, batch, seq_all, ctx_len):
    first = (A_Q + 2 * A_KV) // LANE_BLOCK
    nblk = 2 * B_QK // LANE_BLOCK
    return pl.pallas_call(
        functools.partial(_conv_body, ctx_len=ctx_len, k_first_block=B_QK // LANE_BLOCK),
        grid=(batch, nblk),
        in_specs=[pl.BlockSpec((seq_all, LANE_BLOCK), lambda b, c: (b, first + c)),
                  pl.BlockSpec((B_CONV_W, LANE_BLOCK), lambda b, c: (0, c))],
        out_specs=pl.BlockSpec((seq_all, LANE_BLOCK), lambda b, c: (b, c)),
        out_shape=jax.ShapeDtypeStruct((batch * seq_all, 2 * B_QK), BF16),
        compiler_params=_cp("arbitrary", "arbitrary"),
        name="mlstm_conv",
    )(proj, conv_w)


def _gates_body(raw_ref, b_ref, o_ref):
    g = raw_ref[...] + b_ref[...]
    g = GATE_CAP * jnp.tanh(g / GATE_CAP)
    lane = lax.broadcasted_iota(jnp.int32, g.shape, 1)
    is_forget = (lane // B_HEADS) % 2 == 1
    log_sig = jnp.minimum(g, 0.0) - jnp.log(1.0 + jnp.exp(-jnp.abs(g)))
    o_ref[...] = jnp.where(is_forget, log_sig, g)


def _mlstm_gates(raw, gate_b):
    rows, width = raw.shape
    tr = _row_tile_size(rows, 2048)
    return pl.pallas_call(
        _gates_body,
        grid=(rows // tr,),
        in_specs=[pl.BlockSpec((tr, width), lambda i: (i, 0)),
                  pl.BlockSpec((1, width), lambda i: (0, 0))],
        out_specs=pl.BlockSpec((tr, width), lambda i: (i, 0)),
        out_shape=jax.ShapeDtypeStruct((rows, width), F32),
        compiler_params=_cp("arbitrary"),
        name="mlstm_gates",
    )(raw, gate_b)


def _mlstm_direction(q_ref, k_ref, v_ref, g_ref, o_ref, st_ref, n_ref, m_ref, d, rev):
    c_len = q_ref.shape[0]
    q = q_ref[...]
    k = k_ref[...]
    v = v_ref[...].astype(BF16)
    ig = g_ref[2 * d:2 * d + 1, :]
    lf = g_ref[2 * d + 1:2 * d + 2, :]
    r = lax.broadcasted_iota(jnp.int32, (c_len, c_len), 0)
    c = lax.broadcasted_iota(jnp.int32, (c_len, c_len), 1)
    eye = r == c
    tri = (c >= r) if rev else (c <= r)
    tri_t = (r >= c) if rev else (r <= c)
    lf_col = jnp.sum(jnp.where(eye, lf, 0.0), axis=1, keepdims=True)
    ig_col = jnp.sum(jnp.where(eye, ig, 0.0), axis=1, keepdims=True)
    b_col = jnp.sum(jnp.where(tri, lf, 0.0), axis=1, keepdims=True)
    b_row = jnp.sum(jnp.where(tri_t, lf_col, 0.0), axis=0, keepdims=True)
    m_prev = m_ref[d]
    d_st = b_col + m_prev
    d_in = jnp.where(tri, b_col - b_row + ig, NEG_INF)
    m_t = jnp.maximum(d_st, jnp.max(d_in, axis=1, keepdims=True))
    s = jnp.exp(d_in - m_t) * lax.dot_general(q, k, NT_DIMS, preferred_element_type=F32)
    w_st = jnp.exp(d_st - m_t)
    st = st_ref[d]
    num = (jnp.dot(s.astype(BF16), v, preferred_element_type=F32)
           + w_st * jnp.dot(q, st.astype(BF16), preferred_element_type=F32))
    qn = jnp.sum(q.astype(F32) * n_ref[d], axis=1, keepdims=True)
    den = jnp.sum(s, axis=1, keepdims=True) + w_st * qn
    o_ref[...] = num / jnp.maximum(jnp.abs(den), jnp.exp(-m_t))
    b_end = jnp.sum(lf, axis=1, keepdims=True)
    d_up = b_end - b_col + ig_col
    m_new = jnp.maximum(b_end + m_prev, jnp.max(d_up, axis=0, keepdims=True))
    w_up = jnp.exp(d_up - m_new)
    a = jnp.exp(b_end + m_prev - m_new)
    kw = k.astype(F32) * w_up
    st_ref[d] = a * st + lax.dot_general(kw.astype(BF16), v, TN_DIMS, preferred_element_type=F32)
    n_ref[d] = a * n_ref[d] + jnp.sum(kw, axis=0, keepdims=True)
    m_ref[d] = m_new


def _mlstm_body(qf, kf, vf, gf, qb, kb, vb, gb, of, ob, st_ref, n_ref, m_ref):
    @pl.when(pl.program_id(2) == 0)
    def _():
        st_ref[...] = jnp.zeros_like(st_ref)
        n_ref[...] = jnp.zeros_like(n_ref)
        m_ref[...] = jnp.zeros_like(m_ref)

    _mlstm_direction(qf, kf, vf, gf, of, st_ref, n_ref, m_ref, 0, False)
    _mlstm_direction(qb, kb, vb, gb, ob, st_ref, n_ref, m_ref, 1, True)


def _scan_orders(n_chunks, ctx_chunks):
    fwd = lambda j: j
    bwd = lambda j: jnp.where(j < ctx_chunks, ctx_chunks - 1 - j, n_chunks - 1 - (j - ctx_chunks))
    return fwd, bwd


def _mlstm_scan(qk, proj, gates_t, *, batch, seq_all, ctx_len):
    ch = MLSTM_CHUNK
    n_chunks = seq_all // ch
    fwd, bwd = _scan_orders(n_chunks, ctx_len // ch)
    v_first = (A_Q + 2 * A_KV + 2 * B_QK) // B_V_DIM

    def specs(order):
        return [pl.BlockSpec((ch, B_QK_DIM), lambda b, h, j: (b * n_chunks + order(j), h)),
                pl.BlockSpec((ch, B_QK_DIM), lambda b, h, j: (b * n_chunks + order(j), B_HEADS + h)),
                pl.BlockSpec((ch, B_V_DIM), lambda b, h, j: (b * n_chunks + order(j), v_first + h)),
                pl.BlockSpec((None, None, 4, ch), lambda b, h, j: (b, h, 0, order(j)))]

    def out_spec(order):
        return pl.BlockSpec((ch, B_V_DIM), lambda b, h, j: (b * n_chunks + order(j), h))

    out = jax.ShapeDtypeStruct((batch * seq_all, B_V), F32)
    return pl.pallas_call(
        _mlstm_body,
        grid=(batch, B_HEADS, n_chunks),
        in_specs=specs(fwd) + specs(bwd),
        out_specs=[out_spec(fwd), out_spec(bwd)],
        out_shape=[out, out],
        scratch_shapes=[pltpu.VMEM((2, B_QK_DIM, B_V_DIM), F32),
                        pltpu.VMEM((2, 1, B_QK_DIM), F32),
                        pltpu.VMEM((2, 1, 1), F32)],
        compiler_params=_cp("arbitrary", "arbitrary", "arbitrary"),
        name="mlstm_scan",
    )(qk, qk, proj, gates_t, qk, qk, proj, gates_t)


def _headnorm_gate_body(a_ref, b_ref, gate_ref, g_ref, o_ref, *, head_dim, gate_fn):
    for hd in range(a_ref.shape[1] // head_dim):
        sl = slice(hd * head_dim, (hd + 1) * head_dim)
        y = _rms(a_ref[:, sl] + b_ref[:, sl], g_ref[:, sl])
        o_ref[:, sl] = (y * gate_fn(gate_ref[:, sl])).astype(o_ref.dtype)


def _headnorm_gate(a, b, proj, gate_col0, norm_g, *, head_dim, gate_fn, name):
    rows, width = a.shape
    tc = min(1024, width)
    return pl.pallas_call(
        functools.partial(_headnorm_gate_body, head_dim=head_dim, gate_fn=gate_fn),
        grid=(rows // ROW_TILE, width // tc),
        in_specs=[pl.BlockSpec((ROW_TILE, tc), lambda i, c: (i, c)),
                  pl.BlockSpec((ROW_TILE, tc), lambda i, c: (i, c)),
                  pl.BlockSpec((ROW_TILE, tc), lambda i, c: (i, gate_col0 // tc + c)),
                  pl.BlockSpec((1, tc), lambda i, c: (0, c))],
        out_specs=pl.BlockSpec((ROW_TILE, tc), lambda i, c: (i, c)),
        out_shape=jax.ShapeDtypeStruct((rows, width), BF16),
        compiler_params=_cp("arbitrary", "arbitrary"),
        name=name,
    )(a, b, proj, norm_g)


def _hgrn_direction(q_ref, i_ref, f_ref, lb_ref, o_ref, st_ref, d, rev):
    rows = q_ref.shape[0]
    ch = HGRN_CHUNK
    lb = lb_ref[d:d + 1, :]
    q = _silu(q_ref[...])
    f = lb + (1.0 - lb) * _sigmoid(f_ref[...])
    kk = 1.0 - f
    a_cum = jnp.log(f)
    pos = lax.broadcasted_iota(jnp.int32, a_cum.shape, 0) % ch
    sh = 1
    while sh < ch:
        if rev:
            a_cum = a_cum + jnp.where(pos < ch - sh, pltpu.roll(a_cum, rows - sh, axis=0), 0.0)
        else:
            a_cum = a_cum + jnp.where(pos >= sh, pltpu.roll(a_cum, sh, axis=0), 0.0)
        sh *= 2
    n_chunks = rows // ch
    end_row = 0 if rev else ch - 1
    mid_row = ch // 2
    width = a_cum.shape[1]

    def per_chunk_row(row):
        return jnp.concatenate([jnp.broadcast_to(a_cum[ci * ch + row:ci * ch + row + 1, :], (ch, width))
                                for ci in range(n_chunks)], axis=0)

    a_mid = per_chunk_row(mid_row)
    a_end = per_chunk_row(end_row)
    v = i_ref[...].astype(BF16)
    q_in = (q * jnp.exp(a_cum)).astype(BF16)
    qh = (q * jnp.exp(a_cum - a_mid)).astype(BF16)
    kh = (kk * jnp.exp(a_mid - a_cum)).astype(BF16)
    ke = (kk * jnp.exp(a_end - a_cum)).astype(BF16)
    r = lax.broadcasted_iota(jnp.int32, (rows, rows), 0)
    c = lax.broadcasted_iota(jnp.int32, (rows, rows), 1)
    visible = (r // ch == c // ch) & ((c >= r) if rev else (c <= r))
    order = range(n_chunks - 1, -1, -1) if rev else range(n_chunks)
    for hd in range(HGRN_HEADS):
        cs = slice(hd * C_HEAD_DIM, (hd + 1) * C_HEAD_DIM)
        att = jnp.where(visible, lax.dot_general(qh[:, cs], kh[:, cs], NT_DIMS, preferred_element_type=F32), 0.0)
        o_local = jnp.dot(att.astype(BF16), v[:, cs], preferred_element_type=F32)
        updates = {ci: lax.dot_general(v[ci * ch:(ci + 1) * ch, cs], ke[ci * ch:(ci + 1) * ch, cs], TN_DIMS,
                                       preferred_element_type=F32) for ci in order}
        st = st_ref[d, hd]
        for ci in order:
            rs = slice(ci * ch, (ci + 1) * ch)
            o_ref[rs, cs] = o_local[rs, :] + lax.dot_general(q_in[rs, cs], st.astype(BF16), NT_DIMS,
                                                              preferred_element_type=F32)
            st = jnp.exp(a_cum[ci * ch + end_row:ci * ch + end_row + 1, cs]) * st + updates[ci]
        st_ref[d, hd] = st


def _hgrn_body(qf, vf, ff, qb, vb, fb, lb_ref, of, ob, st_ref):
    @pl.when(pl.program_id(2) == 0)
    def _():
        st_ref[...] = jnp.zeros_like(st_ref)

    _hgrn_direction(qf, vf, ff, lb_ref, of, st_ref, 0, False)
    _hgrn_direction(qb, vb, fb, lb_ref, ob, st_ref, 1, True)


def _hgrn_scan(proj, lb, *, batch, seq_all, ctx_len, d_model):
    blk = HGRN_BLOCK
    n_blocks = seq_all // blk
    fwd, bwd = _scan_orders(n_blocks, ctx_len // blk)
    wcols = HGRN_HEADS * C_HEAD_DIM
    per = d_model // wcols

    def spec(order, part):
        return pl.BlockSpec((blk, wcols), lambda b, h, j: (b * n_blocks + order(j), part * per + h))

    def out_spec(order):
        return pl.BlockSpec((blk, wcols), lambda b, h, j: (b * n_blocks + order(j), h))

    out = jax.ShapeDtypeStruct((batch * seq_all, d_model), F32)
    return pl.pallas_call(
        _hgrn_body,
        grid=(batch, per, n_blocks),
        in_specs=[spec(fwd, 0), spec(fwd, 1), spec(fwd, 3), spec(bwd, 0), spec(bwd, 1), spec(bwd, 4),
                  pl.BlockSpec((2, wcols), lambda b, h, j: (0, h))],
        out_specs=[out_spec(fwd), out_spec(bwd)],
        out_shape=[out, out],
        scratch_shapes=[pltpu.VMEM((2, HGRN_HEADS, C_HEAD_DIM, C_HEAD_DIM), F32)],
        compiler_params=_cp("arbitrary", "arbitrary", "arbitrary"),
        name="hgrn_scan",
    )(proj, proj, proj, proj, proj, proj, lb)


def _pack_halves(v):
    half = v.shape[1] // 2
    bits = lax.bitcast_convert_type(v.astype(BF16).astype(F32), jnp.uint32)
    return (bits[:, :half] >> 16) | (bits[:, half:] & jnp.uint32(0xFFFF0000))


def _unpack_halves(p):
    lo = lax.bitcast_convert_type(p << 16, F32)
    hi = lax.bitcast_convert_type(p & jnp.uint32(0xFFFF0000), F32)
    return lo, hi


def _ffn_router_body(x_ref, g_ref, mod_ref, rwt_ref, rb_ref, hb_ref, hp_ref, idx_ref, wgt_ref, rank_ref, cnt_ref):
    @pl.when(pl.program_id(0) == 0)
    def _():
        cnt_ref[...] = jnp.zeros_like(cnt_ref)

    y = _rms(x_ref[...], g_ref[...])
    h2 = y * (1.0 + mod_ref[4:5, :]) + mod_ref[3:4, :]
    hb = h2.astype(BF16)
    hb_ref[...] = hb
    hp_ref[...] = _pack_halves(hb)
    tm = h2.shape[0]
    logits = lax.dot_general(rwt_ref[...], h2, NT_DIMS, precision=lax.Precision.HIGHEST,
                             preferred_element_type=F32)
    s = _sigmoid(logits)
    sel = s + rb_ref[...]
    iota_g = lax.broadcasted_iota(jnp.int32, (GROUP_SIZE, tm), 0)
    group_scores = []
    for g in range(N_GROUPS):
        xg = sel[g * GROUP_SIZE:(g + 1) * GROUP_SIZE, :]
        m1 = jnp.max(xg, axis=0, keepdims=True)
        i1 = jnp.min(jnp.where(xg == m1, iota_g, GROUP_SIZE), axis=0, keepdims=True)
        m2 = jnp.max(jnp.where(iota_g == i1, NEG_INF, xg), axis=0, keepdims=True)
        group_scores.append(m1 + m2)
    gsc = jnp.concatenate(group_scores, axis=0)
    iota_n = lax.broadcasted_iota(jnp.int32, (N_GROUPS, tm), 0)
    gmask = jnp.zeros((N_GROUPS, tm), F32)
    for _ in range(TOPK_GROUPS):
        m = jnp.max(gsc, axis=0, keepdims=True)
        i = jnp.min(jnp.where(gsc == m, iota_n, N_GROUPS), axis=0, keepdims=True)
        hit = iota_n == i
        gmask = jnp.where(hit, 1.0, gmask)
        gsc = jnp.where(hit, NEG_INF, gsc)
    emask = jnp.concatenate([jnp.broadcast_to(gmask[g:g + 1, :], (GROUP_SIZE, tm)) for g in range(N_GROUPS)], axis=0)
    cur = jnp.where(emask > 0.0, sel, NEG_INF)
    iota_e = lax.broadcasted_iota(jnp.int32, (N_EXPERTS, tm), 0)
    idx_rows, w_rows, hits = [], [], []
    for _ in range(TOP_K):
        m = jnp.max(cur, axis=0, keepdims=True)
        i = jnp.min(jnp.where(cur == m, iota_e, N_EXPERTS), axis=0, keepdims=True)
        hit = iota_e == i
        idx_rows.append(i)
        hits.append(hit)
        w_rows.append(jnp.sum(jnp.where(hit, s, 0.0), axis=0, keepdims=True))
        cur = jnp.where(hit, NEG_INF, cur)
    w = jnp.concatenate(w_rows, axis=0)
    idx_ref[...] = jnp.concatenate(idx_rows, axis=0)
    wgt_ref[...] = ROUTED_SCALE * w / jnp.sum(w, axis=0, keepdims=True)
    chosen = jnp.zeros((N_EXPERTS, tm), F32)
    for hit in hits:
        chosen = jnp.where(hit, 1.0, chosen)
    earlier = (lax.broadcasted_iota(jnp.int32, (tm, tm), 0) < lax.broadcasted_iota(jnp.int32, (tm, tm), 1))
    before = cnt_ref[...] + jnp.dot(chosen.astype(BF16), earlier.astype(BF16), preferred_element_type=F32)
    rank_ref[...] = jnp.concatenate(
        [jnp.sum(jnp.where(hit, before, 0.0), axis=0, keepdims=True) for hit in hits], axis=0).astype(jnp.int32)
    cnt_ref[...] = cnt_ref[...] + jnp.sum(chosen, axis=1, keepdims=True)


def _ffn_router(x, g, mod, sel, router_wt, router_b, layer):
    rows, d = x.shape
    return pl.pallas_call(
        _ffn_router_body,
        grid=(rows // ROW_TILE,),
        in_specs=[pl.BlockSpec((ROW_TILE, d), lambda i: (i, 0)),
                  pl.BlockSpec((1, d), lambda i: (0, 0)),
                  pl.BlockSpec((None, 6, d), lambda i: (sel(i), 0, 0)),
                  pl.BlockSpec((None, N_EXPERTS, d), lambda i: (layer, 0, 0)),
                  pl.BlockSpec((None, N_EXPERTS, 1), lambda i: (layer, 0, 0))],
        out_specs=[pl.BlockSpec((ROW_TILE, d), lambda i: (i, 0)),
                   pl.BlockSpec((ROW_TILE, d // 2), lambda i: (i, 0)),
                   pl.BlockSpec((TOP_K, ROW_TILE), lambda i: (0, i)),
                   pl.BlockSpec((TOP_K, ROW_TILE), lambda i: (0, i)),
                   pl.BlockSpec((TOP_K, ROW_TILE), lambda i: (0, i)),
                   pl.BlockSpec((N_EXPERTS, 1), lambda i: (0, 0))],
        out_shape=[jax.ShapeDtypeStruct((rows, d), BF16),
                   jax.ShapeDtypeStruct((rows, d // 2), jnp.uint32),
                   jax.ShapeDtypeStruct((TOP_K, rows), jnp.int32),
                   jax.ShapeDtypeStruct((TOP_K, rows), F32),
                   jax.ShapeDtypeStruct((TOP_K, rows), jnp.int32),
                   jax.ShapeDtypeStruct((N_EXPERTS, 1), F32)],
        compiler_params=_cp("arbitrary"),
        name=f"ffn_router{layer}",
    )(x, g, mod, router_wt, router_b)


def _dispatch(idx_t, rank_t, counts):
    tm = MOE_TILE
    slots = idx_t.shape[0] * idx_t.shape[1]
    n_tiles = slots // tm
    counts = counts.reshape(-1).astype(jnp.int32)
    ends = jnp.cumsum(counts)
    starts = ends - counts
    is_e = idx_t[:, :, None] == jnp.arange(N_EXPERTS, dtype=jnp.int32)
    pos = (jnp.sum(jnp.where(is_e, starts, 0), axis=-1) + rank_t).T.reshape(-1)
    cuts = jnp.sort(jnp.concatenate([jnp.arange(n_tiles, dtype=jnp.int32) * tm, ends[:-1]]))
    nxt = jnp.concatenate([cuts[1:], jnp.full((1,), slots, jnp.int32)])
    v_tile = jnp.minimum(cuts // tm, n_tiles - 1)
    v_exp = jnp.minimum(jnp.sum((ends[None, :] <= cuts[:, None]).astype(jnp.int32), axis=1), N_EXPERTS - 1)
    v_lo = cuts - v_tile * tm
    v_hi = nxt - v_tile * tm
    n_vis = cuts.shape[0]
    v_first = (v_exp != jnp.concatenate([jnp.full((1,), -1, jnp.int32), v_exp[:-1]])).astype(jnp.int32)
    order = jnp.arange(n_vis, dtype=jnp.int32)
    next_first = lax.cummin(jnp.where(v_first == 1, order, n_vis)[::-1])[::-1]
    after = jnp.concatenate([next_first[1:], jnp.full((1,), n_vis, jnp.int32)])
    v_next = jnp.where(after < n_vis, v_exp[jnp.minimum(after, n_vis - 1)], -1)
    return (pos.astype(jnp.int32), v_tile.astype(jnp.int32), v_exp.astype(jnp.int32), v_lo, v_hi, v_first,
            v_next.astype(jnp.int32))


def _push_body(pos_ref, h_ref, xs_hbm, sem):
    i = pl.program_id(0)
    tt = h_ref.shape[0]
    base = i * tt * TOP_K

    def issue(r, carry):
        src = h_ref.at[pl.ds(r, 1)]
        for k in range(TOP_K):
            pltpu.make_async_copy(src, xs_hbm.at[pl.ds(pos_ref[base + r * TOP_K + k], 1)], sem).start()
        return carry

    lax.fori_loop(0, tt, issue, 0)
    for _ in range(TOP_K):
        pltpu.make_async_copy(h_ref, xs_hbm.at[pl.ds(0, tt)], sem).wait()


def _push(h2f, pos, layer):
    rows, d = h2f.shape
    tt = PUSH_TOKENS
    grid_spec = pltpu.PrefetchScalarGridSpec(
        num_scalar_prefetch=1,
        grid=(rows // tt,),
        in_specs=[pl.BlockSpec((tt, d), lambda i, ps: (i, 0))],
        out_specs=pl.BlockSpec(memory_space=pl.ANY),
        scratch_shapes=[pltpu.SemaphoreType.DMA(())],
    )
    return pl.pallas_call(
        _push_body,
        grid_spec=grid_spec,
        out_shape=jax.ShapeDtypeStruct((rows * TOP_K, d), h2f.dtype),
        compiler_params=_cp("arbitrary"),
        name=f"push{layer}",
    )(pos, h2f)


def _expert_body(vt_ref, ve_ref, lo_ref, hi_ref, first_ref, next_ref, x_ref, wg_hbm, wu_hbm, wd_hbm, y_ref,
                 wgf, wuf, wdf, wgb, wub, wdb, sem, *, layer):
    v = pl.program_id(0)
    lo = lo_ref[v]
    hi = hi_ref[v]

    def weight_copies(e):
        return (pltpu.make_async_copy(wg_hbm.at[layer, e], wgf, sem),
                pltpu.make_async_copy(wu_hbm.at[layer, e], wuf, sem),
                pltpu.make_async_copy(wd_hbm.at[layer, e], wdf, sem))

    @pl.when(v == 0)
    def _():
        for cp in weight_copies(ve_ref[0]):
            cp.start()

    @pl.when(first_ref[v] == 1)
    def _():
        for cp in weight_copies(ve_ref[v]):
            cp.wait()
        wgb[...] = wgf[...].astype(BF16)
        wub[...] = wuf[...].astype(BF16)
        wdb[...] = wdf[...].astype(BF16)

        @pl.when(next_ref[v] >= 0)
        def _():
            for cp in weight_copies(next_ref[v]):
                cp.start()

    @pl.when(hi > lo)
    def _():
        x_lo, x_hi = _unpack_halves(x_ref[...])
        x_lo = x_lo.astype(BF16)
        x_hi = x_hi.astype(BF16)
        half = x_lo.shape[1]
        g = (jnp.dot(x_lo, wgb[:half, :], preferred_element_type=F32)
             + jnp.dot(x_hi, wgb[half:, :], preferred_element_type=F32))
        u = (jnp.dot(x_lo, wub[:half, :], preferred_element_type=F32)
             + jnp.dot(x_hi, wub[half:, :], preferred_element_type=F32))
        hid = (_silu(g) * u).astype(BF16)
        y = _pack_halves(jnp.dot(hid, wdb[...], preferred_element_type=F32))
        r = lax.broadcasted_iota(jnp.int32, (y.shape[0], 1), 0)
        mine = (r >= lo) & (r < hi)

        @pl.when(lo == 0)
        def _():
            y_ref[...] = jnp.where(mine, y, jnp.uint32(0))

        @pl.when(lo > 0)
        def _():
            y_ref[...] = jnp.where(mine, y, y_ref[...])


def _experts(xs, visits, wg, wu, wd, layer):
    slots, dp = xs.shape
    d = wg.shape[2]
    de = wg.shape[3]
    tm = MOE_TILE
    hbm = pl.BlockSpec(memory_space=pl.ANY)
    grid_spec = pltpu.PrefetchScalarGridSpec(
        num_scalar_prefetch=len(visits),
        grid=(visits[0].shape[0],),
        in_specs=[pl.BlockSpec((tm, dp), lambda v, vt, *_: (vt[v], 0)), hbm, hbm, hbm],
        out_specs=pl.BlockSpec((tm, dp), lambda v, vt, *_: (vt[v], 0)),
        scratch_shapes=[pltpu.VMEM((d, de), F32), pltpu.VMEM((d, de), F32), pltpu.VMEM((de, d), F32),
                        pltpu.VMEM((d, de), BF16), pltpu.VMEM((d, de), BF16), pltpu.VMEM((de, d), BF16),
                        pltpu.SemaphoreType.DMA(())],
    )
    return pl.pallas_call(
        functools.partial(_expert_body, layer=layer),
        grid_spec=grid_spec,
        out_shape=jax.ShapeDtypeStruct((slots, dp), jnp.uint32),
        compiler_params=_cp("arbitrary"),
        name=f"experts{layer}",
    )(*visits, xs, wg, wu, wd)


def _combine_body(pos_ref, w_ref, y_hbm, o_ref, buf, sem):
    i = pl.program_id(0)
    tt = o_ref.shape[0]
    n = TOP_K * tt
    slot = i % 2
    half = buf.shape[3]
    group = 8

    def issue_rows(step, into, r0, rows):
        for j in range(rows):
            for k in range(TOP_K):
                pltpu.make_async_copy(y_hbm.at[pl.ds(pos_ref[step * n + (r0 + j) * TOP_K + k], 1)],
                                      buf.at[into, k, pl.ds(r0 + j, 1)], sem.at[into]).start()

    @pl.when(i == 0)
    def _():
        def first(g, carry):
            issue_rows(0, 0, g * group, group)
            return carry

        lax.fori_loop(0, tt // group, first, 0)

    for k in range(TOP_K):
        pltpu.make_async_copy(y_hbm.at[pl.ds(0, tt)], buf.at[slot, k], sem.at[slot]).wait()
    last = pl.num_programs(0) - 1
    nxt = jnp.minimum(i + 1, last)

    def body(g, carry):
        r0 = pl.multiple_of(g * group, group)
        issue_rows(nxt, 1 - slot, r0, group)
        w = w_ref[pl.ds(r0, group), :]
        acc_lo, acc_hi = _unpack_halves(buf[slot, 0, pl.ds(r0, group), :])
        acc_lo = w[:, 0:1] * acc_lo
        acc_hi = w[:, 0:1] * acc_hi
        for k in range(1, TOP_K):
            y_lo, y_hi = _unpack_halves(buf[slot, k, pl.ds(r0, group), :])
            acc_lo = acc_lo + w[:, k:k + 1] * y_lo
            acc_hi = acc_hi + w[:, k:k + 1] * y_hi
        o_ref[pl.ds(r0, group), :half] = acc_lo
        o_ref[pl.ds(r0, group), half:] = acc_hi
        return carry

    lax.fori_loop(0, tt // group, body, 0)

    @pl.when(i == last)
    def _():
        for k in range(TOP_K):
            pltpu.make_async_copy(y_hbm.at[pl.ds(0, tt)], buf.at[1 - slot, k], sem.at[1 - slot]).wait()


def _combine(ys, pos_flat, wgt, rows, layer):
    dp = ys.shape[1]
    d = 2 * dp
    tt = COMBINE_TOKENS
    grid_spec = pltpu.PrefetchScalarGridSpec(
        num_scalar_prefetch=1,
        grid=(rows // tt,),
        in_specs=[pl.BlockSpec((tt, TOP_K), lambda i, ps: (i, 0)),
                  pl.BlockSpec(memory_space=pl.ANY)],
        out_specs=pl.BlockSpec((tt, d), lambda i, ps: (i, 0)),
        scratch_shapes=[pltpu.VMEM((2, TOP_K, tt, dp), jnp.uint32), pltpu.SemaphoreType.DMA((2,))],
    )
    return pl.pallas_call(
        _combine_body,
        grid_spec=grid_spec,
        out_shape=jax.ShapeDtypeStruct((rows, d), F32),
        compiler_params=_cp("arbitrary"),
        name=f"combine{layer}",
    )(pos_flat, wgt, ys)


def _moe(x1, mod, sel, layer, norm_g, router_w, router_b, exp_wg, exp_wu, exp_wd, s_gate, s_up, s_down, tm_big):
    rows, d = x1.shape
    router_wt = jnp.swapaxes(router_w, 1, 2)
    h2b, h2p, idx_t, wgt_t, rank_t, counts = _ffn_router(x1, norm_g[layer][None], mod, sel, router_wt,
                                                         router_b[:, :, None], layer)
    pos, *visits = _dispatch(idx_t, rank_t, counts)
    xs = _push(h2p, pos, layer)
    ys = _experts(xs, visits, exp_wg, exp_wu, exp_wd, layer)
    routed = _combine(ys, pos, wgt_t.T, rows, layer)
    hs = _glu(h2b, s_gate, s_up, layer, tm=tm_big, name=f"shared_glu{layer}")
    return _linear([hs], s_down, layer, n_cols=d, tm=ROW_TILE, out_dtype=F32, name=f"shared_down{layer}",
                   res=(x1, mod, 5, routed, sel), tn=4 * COL_TILE)


def _rope_tables(ctx_len, lat_len):
    rows = lat_len // GRID_W
    pos_r = jnp.repeat(jnp.arange(rows), GRID_W).astype(F32)
    pos_c = jnp.tile(jnp.arange(GRID_W), rows).astype(F32)
    n = A_HEAD_DIM // 4
    inv = ROPE_BASE ** (-jnp.arange(n, dtype=F32) / n)
    ang = jnp.concatenate([pos_r[:, None] * inv, pos_c[:, None] * inv], axis=-1)
    cos, sin = jnp.cos(ang), jnp.sin(ang)
    cos_t = jnp.concatenate([jnp.ones((ctx_len, A_HEAD_DIM), F32), jnp.concatenate([cos, cos], axis=-1)], axis=0)
    sin_t = jnp.concatenate([jnp.zeros((ctx_len, A_HEAD_DIM), F32), jnp.concatenate([-sin, sin], axis=-1)], axis=0)
    return cos_t, sin_t


def kernel(x, c, ctx, c_ctx, ada_w, ada_b, norm_mix_g, norm_ffn_g, ab_w_in, ab_w_out, a_q_norm_g, a_k_norm_g,
           a_sink, b_conv_w, b_gate_b, b_norm_g, c_w_in, c_w_out, c_lb_logits, c_norm_g, router_w, router_b,
           exp_w_gate, exp_w_up, exp_w_down, shared_w_gate, shared_w_up, shared_w_down):
    batch, lat_len, d = x.shape
    ctx_len = ctx.shape[1]
    depth = ada_w.shape[0]
    seq_all = ctx_len + lat_len
    rows = batch * seq_all
    assert ctx_len % ROW_TILE == 0 and lat_len % ROW_TILE == 0 and ctx_len % MLSTM_CHUNK == 0
    assert lat_len % GRID_W == 0 and d % (HGRN_HEADS * C_HEAD_DIM) == 0
    tiles_per_sample = seq_all // ROW_TILE
    ctx_tiles = ctx_len // ROW_TILE
    tm_big = _row_tile_size(rows, 1152)

    def sel(i):
        return jnp.where(i % tiles_per_sample < ctx_tiles, batch, i // tiles_per_sample)

    cos_t, sin_t = _rope_tables(ctx_len, lat_len)
    lb_soft = jax.nn.softmax(c_lb_logits.astype(F32), axis=0)
    lower_bounds = jnp.cumsum(lb_soft, axis=0) - lb_soft[0:1]
    cond = jnp.concatenate([jax.nn.silu(c), jax.nn.silu(c_ctx)[None]], axis=0)
    cond = jnp.pad(cond, ((0, 16 - cond.shape[0]), (0, 0))).astype(BF16)

    lat_tiles = lat_len // ROW_TILE

    def lat_map(i):
        return (i // lat_tiles) * tiles_per_sample + ctx_tiles + i % lat_tiles

    xa = jnp.concatenate([ctx, x], axis=1).reshape(rows, d)
    for layer in range(depth):
        last = layer == depth - 1
        tail = dict(row_map=lat_map, out_rows=batch * lat_len) if last else {}
        tail_sel = (lambda i: i // lat_tiles) if last else sel
        mod = _ada(cond, ada_w, ada_b, layer)[:batch + 1].reshape(batch + 1, 6, d)
        h = _modulate(xa, norm_mix_g[layer][None], mod, sel, shift_row=0, scale_row=1, name=f"mod_mix{layer}")
        if layer % 2 == 0:
            e = layer // 2
            proj = _linear([h], ab_w_in, e, n_cols=AB_MAIN, tm=tm_big, out_dtype=F32, name=f"ab_in{layer}")
            w_gates = jnp.pad(ab_w_in[e][:, AB_MAIN:], ((0, 0), (0, LANE_BLOCK - AB_GATES)))[None]
            graw = _linear([h], w_gates, 0, n_cols=LANE_BLOCK, tm=tm_big, out_dtype=F32, name=f"ab_gates{layer}")
            qn, kn, vb = _attn_prep(proj, cos_t, sin_t, a_q_norm_g[e][None], a_k_norm_g[e][None], tiles_per_sample)
            ya = _attention(qn, kn, vb, a_sink[e], batch=batch, seq_all=seq_all, ctx_len=ctx_len)
            qk = _mlstm_conv(proj, b_conv_w[e], batch=batch, seq_all=seq_all, ctx_len=ctx_len)
            gate_b = jnp.pad(b_gate_b[e], (0, LANE_BLOCK - AB_GATES))[None]
            gates = _mlstm_gates(graw, gate_b)[:, :AB_GATES]
            gates_t = gates.reshape(batch, seq_all, 4, B_HEADS).transpose(0, 3, 2, 1)
            hf, hb = _mlstm_scan(qk, proj, gates_t, batch=batch, seq_all=seq_all, ctx_len=ctx_len)
            yb = _headnorm_gate(hf, hb, proj, AB_MAIN - B_V, b_norm_g[e][None], head_dim=B_V_DIM,
                                gate_fn=_sigmoid, name="mlstm_out")
            xa = _linear([ya, yb], ab_w_out, e, n_cols=d, tm=ROW_TILE, out_dtype=F32, name=f"ab_out{layer}",
                         res=(xa, mod, 2, None, tail_sel), tn=2 * COL_TILE, **tail)
        else:
            o = layer // 2
            proj = _linear([h], c_w_in, o, n_cols=5 * d, tm=tm_big, out_dtype=F32, name=f"c_in{layer}")
            of, ob = _hgrn_scan(proj, lower_bounds[layer], batch=batch, seq_all=seq_all, ctx_len=ctx_len, d_model=d)
            yc = _headnorm_gate(of, ob, proj, 2 * d, c_norm_g[o][None], head_dim=C_HEAD_DIM, gate_fn=_silu,
                                name="hgrn_out")
            xa = _linear([yc], c_w_out, o, n_cols=d, tm=ROW_TILE, out_dtype=F32, name=f"c_out{layer}",
                         res=(xa, mod, 2, None, tail_sel), tn=2 * COL_TILE, **tail)
        xa = _moe(xa, mod, tail_sel, layer, norm_ffn_g, router_w, router_b, exp_w_gate, exp_w_up, exp_w_down,
                  shared_w_gate, shared_w_up, shared_w_down, _row_tile_size(xa.shape[0], 1152))
    return xa.reshape(batch, lat_len, d)
```

```python
import functools

import jax
import jax.numpy as jnp
import numpy as np
from jax import lax
from jax.experimental import pallas as pl
from jax.experimental.pallas import tpu as pltpu

F32 = jnp.float32
BF16 = jnp.bfloat16
NEG_INF = float("-inf")

EPS = 1e-6
GRID_W = 64
ROPE_BASE = 10000.0

A_HEADS = 16
A_KV_HEADS = 4
A_GROUP = A_HEADS // A_KV_HEADS
A_HEAD_DIM = 128
WINDOW = 128
A_Q = A_HEADS * A_HEAD_DIM
A_KV = A_KV_HEADS * A_HEAD_DIM

B_HEADS = 8
B_QK_DIM = 128
B_V_DIM = 256
B_CONV_W = 5
GATE_CAP = 15.0
B_QK = B_HEADS * B_QK_DIM
B_V = B_HEADS * B_V_DIM
AB_MAIN = A_Q + 2 * A_KV + 2 * B_QK + 2 * B_V
AB_GATES = 4 * B_HEADS

C_HEAD_DIM = 128

N_EXPERTS = 64
TOP_K = 8
N_GROUPS = 8
TOPK_GROUPS = 4
GROUP_SIZE = N_EXPERTS // N_GROUPS
ROUTED_SCALE = 2.5

VMEM_LIMIT_BYTES = 60 * 1024 * 1024
LANE_BLOCK = 128
ROW_TILE = 256
COL_TILE = 512
MLSTM_CHUNK = 256
HGRN_BLOCK = 256
HGRN_CHUNK = 32
HGRN_HEADS = 4
MOE_TILE = 256
DMA_PRIORITIES = 2
PUSH_TOKENS = 128
COMBINE_TOKENS = 64

NT_DIMS = (((1,), (1,)), ((), ()))
TN_DIMS = (((0,), (0,)), ((), ()))


def _cp(*sem):
    return pltpu.CompilerParams(dimension_semantics=sem, vmem_limit_bytes=VMEM_LIMIT_BYTES)


def _sigmoid(x):
    return jax.nn.sigmoid(x)


def _silu(x):
    return x * jax.nn.sigmoid(x)


def _rms(x, g):
    return x * lax.rsqrt(jnp.mean(x * x, axis=-1, keepdims=True) + EPS) * g


def _row_tile_size(rows, limit):
    best = 16
    for t in range(16, limit + 1, 16):
        if rows % t == 0:
            best = t
    return best


def _ada_body(a_ref, w_ref, b_ref, o_ref):
    o_ref[...] = jnp.dot(a_ref[...], w_ref[...].astype(BF16), preferred_element_type=F32) + b_ref[...]


def _ada(a, ada_w, ada_b, layer):
    depth, d, d6 = ada_w.shape
    rows = a.shape[0]
    return pl.pallas_call(
        _ada_body,
        grid=(d6 // COL_TILE,),
        in_specs=[pl.BlockSpec((rows, d), lambda j: (0, 0)),
                  pl.BlockSpec((None, d, COL_TILE), lambda j: (layer, 0, j)),
                  pl.BlockSpec((None, 1, COL_TILE), lambda j: (layer, 0, j))],
        out_specs=pl.BlockSpec((rows, COL_TILE), lambda j: (0, j)),
        out_shape=jax.ShapeDtypeStruct((rows, d6), F32),
        compiler_params=_cp("arbitrary"),
        name=f"ada{layer}",
    )(a, ada_w, ada_b.reshape(depth, 1, d6))


def _modulate_body(x_ref, g_ref, mod_ref, o_ref, *, shift_row, scale_row):
    y = _rms(x_ref[...], g_ref[...])
    o_ref[...] = (y * (1.0 + mod_ref[scale_row:scale_row + 1, :])
                  + mod_ref[shift_row:shift_row + 1, :]).astype(o_ref.dtype)


def _modulate(x, g, mod, sel, *, shift_row, scale_row, name):
    rows, d = x.shape
    return pl.pallas_call(
        functools.partial(_modulate_body, shift_row=shift_row, scale_row=scale_row),
        grid=(rows // ROW_TILE,),
        in_specs=[pl.BlockSpec((ROW_TILE, d), lambda i: (i, 0)),
                  pl.BlockSpec((1, d), lambda i: (0, 0)),
                  pl.BlockSpec((None, 6, d), lambda i: (sel(i), 0, 0))],
        out_specs=pl.BlockSpec((ROW_TILE, d), lambda i: (i, 0)),
        out_shape=jax.ShapeDtypeStruct((rows, d), BF16),
        compiler_params=_cp("arbitrary"),
        name=name,
    )(x, g, mod)


def _linear_body(*refs, n_a, gate_row, has_extra):
    a_refs = refs[:n_a]
    w_refs = refs[n_a:2 * n_a]
    p = 2 * n_a
    x_ref = mod_ref = e_ref = None
    if gate_row is not None:
        x_ref, mod_ref = refs[p], refs[p + 1]
        p += 2
        if has_extra:
            e_ref = refs[p]
            p += 1
    o_ref = refs[p]
    wb_refs = refs[p + 1:]

    @pl.when(pl.program_id(1) == 0)
    def _():
        for w_ref, wb in zip(w_refs, wb_refs):
            wb[...] = w_ref[...].astype(BF16)

    acc = jnp.dot(a_refs[0][...], wb_refs[0][...], preferred_element_type=F32)
    for a_ref, wb in zip(a_refs[1:], wb_refs[1:]):
        acc = acc + jnp.dot(a_ref[...], wb[...], preferred_element_type=F32)
    if gate_row is not None:
        if e_ref is not None:
            acc = acc + e_ref[...]
        acc = x_ref[...] + mod_ref[gate_row:gate_row + 1, :] * acc
    o_ref[...] = acc.astype(o_ref.dtype)


def _linear(a_list, w, layer, *, n_cols, tm, out_dtype, name, res=None, tn=COL_TILE, row_map=None, out_rows=None):
    n_a = len(a_list)
    rows = a_list[0].shape[0]
    if row_map is None:
        row_map = lambda i: i
    else:
        rows = out_rows
    k_each = w.shape[1] // n_a
    tn = min(tn, n_cols)
    while n_cols % tn:
        tn //= 2
    in_specs = [pl.BlockSpec((tm, k_each), lambda j, i: (row_map(i), 0)) for _ in a_list]
    for idx in range(n_a):
        in_specs.append(pl.BlockSpec((None, k_each, tn), lambda j, i, idx=idx: (layer, idx, j)))
    args = list(a_list) + [w] * n_a
    gate_row, has_extra = None, False
    if res is not None:
        x, mod, gate_row, extra, sel = res
        in_specs.append(pl.BlockSpec((tm, tn), lambda j, i: (row_map(i), j)))
        in_specs.append(pl.BlockSpec((None, 6, tn), lambda j, i: (sel(i), 0, j)))
        args += [x, mod]
        if extra is not None:
            has_extra = True
            in_specs.append(pl.BlockSpec((tm, tn), lambda j, i: (i, j)))
            args.append(extra)
    return pl.pallas_call(
        functools.partial(_linear_body, n_a=n_a, gate_row=gate_row, has_extra=has_extra),
        grid=(n_cols // tn, rows // tm),
        in_specs=in_specs,
        out_specs=pl.BlockSpec((tm, tn), lambda j, i: (i, j)),
        out_shape=jax.ShapeDtypeStruct((rows, n_cols), out_dtype),
        scratch_shapes=[pltpu.VMEM((k_each, tn), BF16) for _ in range(n_a)],
        compiler_params=_cp("arbitrary", "arbitrary"),
        name=name,
    )(*args)


def _glu_body(a_ref, wg_ref, wu_ref, o_ref, wgb, wub):
    @pl.when(pl.program_id(1) == 0)
    def _():
        wgb[...] = wg_ref[...].astype(BF16)
        wub[...] = wu_ref[...].astype(BF16)

    a = a_ref[...]
    g = jnp.dot(a, wgb[...], preferred_element_type=F32)
    u = jnp.dot(a, wub[...], preferred_element_type=F32)
    o_ref[...] = (_silu(g) * u).astype(o_ref.dtype)


def _glu(a, wg, wu, layer, *, tm, name):
    rows, k = a.shape
    n_cols = wg.shape[2]
    tn = min(COL_TILE // 2, n_cols)
    wspec = pl.BlockSpec((None, k, tn), lambda j, i: (layer, 0, j))
    return pl.pallas_call(
        _glu_body,
        grid=(n_cols // tn, rows // tm),
        in_specs=[pl.BlockSpec((tm, k), lambda j, i: (i, 0)), wspec, wspec],
        out_specs=pl.BlockSpec((tm, tn), lambda j, i: (i, j)),
        out_shape=jax.ShapeDtypeStruct((rows, n_cols), BF16),
        scratch_shapes=[pltpu.VMEM((k, tn), BF16), pltpu.VMEM((k, tn), BF16)],
        compiler_params=_cp("arbitrary", "arbitrary"),
        name=name,
    )(a, wg, wu)


def _attn_prep_body(p_ref, cos_ref, sin_ref, qg_ref, kg_ref, q_ref, k_ref, v_ref):
    cos = cos_ref[...]
    sin = sin_ref[...]

    def norm_rope(xh, g):
        y = _rms(xh, g)
        return y * cos + pltpu.roll(y, A_HEAD_DIM // 2, axis=1) * sin

    for hd in range(A_HEADS):
        sl = slice(hd * A_HEAD_DIM, (hd + 1) * A_HEAD_DIM)
        q_ref[:, sl] = norm_rope(p_ref[:, sl], qg_ref[...]).astype(BF16)
    for hd in range(A_KV_HEADS):
        sl = slice(hd * A_HEAD_DIM, (hd + 1) * A_HEAD_DIM)
        k_ref[:, sl] = norm_rope(p_ref[:, A_Q + hd * A_HEAD_DIM:A_Q + (hd + 1) * A_HEAD_DIM], kg_ref[...]).astype(BF16)
    v_ref[...] = p_ref[:, A_Q + A_KV:A_Q + 2 * A_KV].astype(BF16)


def _attn_prep(proj, cos_t, sin_t, qg, kg, tiles_per_sample):
    rows = proj.shape[0]
    width = A_Q + 2 * A_KV
    return pl.pallas_call(
        _attn_prep_body,
        grid=(rows // ROW_TILE,),
        in_specs=[pl.BlockSpec((ROW_TILE, width), lambda i: (i, 0)),
                  pl.BlockSpec((ROW_TILE, A_HEAD_DIM), lambda i: (i % tiles_per_sample, 0)),
                  pl.BlockSpec((ROW_TILE, A_HEAD_DIM), lambda i: (i % tiles_per_sample, 0)),
                  pl.BlockSpec((1, A_HEAD_DIM), lambda i: (0, 0)),
                  pl.BlockSpec((1, A_HEAD_DIM), lambda i: (0, 0))],
        out_specs=[pl.BlockSpec((ROW_TILE, A_Q), lambda i: (i, 0)),
                   pl.BlockSpec((ROW_TILE, A_KV), lambda i: (i, 0)),
                   pl.BlockSpec((ROW_TILE, A_KV), lambda i: (i, 0))],
        out_shape=[jax.ShapeDtypeStruct((rows, A_Q), BF16),
                   jax.ShapeDtypeStruct((rows, A_KV), BF16),
                   jax.ShapeDtypeStruct((rows, A_KV), BF16)],
        compiler_params=_cp("arbitrary"),
        name="attn_prep",
    )(proj, cos_t, sin_t, qg, kg)


def _attn_body(sink_ref, q_ref, kp_ref, ko_ref, kn_ref, kc_ref, vp_ref, vo_ref, vn_ref, vc_ref, o_ref,
               *, ctx_blocks, lat_len):
    h = pl.program_id(1)
    j = pl.program_id(2)
    n = j - ctx_blocks
    nw = 3 * WINDOW
    m_ctx = kc_ref.shape[0]
    kw = jnp.concatenate([kp_ref[...], ko_ref[...], kn_ref[...], kc_ref[...]], axis=0)
    vw = jnp.concatenate([vp_ref[...], vo_ref[...], vn_ref[...], vc_ref[...]], axis=0)
    qi = lax.broadcasted_iota(jnp.int32, (WINDOW, nw + m_ctx), 0)
    wi = lax.broadcasted_iota(jnp.int32, (WINDOW, nw + m_ctx), 1)
    key_pos = (n - 1) * WINDOW + wi
    in_win = (jnp.abs(wi - WINDOW - qi) <= WINDOW) & (key_pos >= 0) & (key_pos < lat_len) & (n >= 0)
    valid = (wi >= nw) | in_win
    bias = jnp.where(valid, 0.0, NEG_INF).astype(F32)
    scale = A_HEAD_DIM ** -0.5
    for g in range(A_GROUP):
        sl = slice(g * A_HEAD_DIM, (g + 1) * A_HEAD_DIM)
        s = lax.dot_general(q_ref[:, sl], kw, NT_DIMS, preferred_element_type=F32) * scale + bias
        sink = sink_ref[h * A_GROUP + g]
        m = jnp.maximum(jnp.max(s, axis=-1, keepdims=True), sink)
        p = jnp.exp(s - m)
        denom = jnp.sum(p, axis=-1, keepdims=True) + jnp.exp(sink - m)
        o = jnp.dot(p.astype(BF16), vw, preferred_element_type=F32)
        o_ref[:, sl] = (o / denom).astype(o_ref.dtype)


def _attention(qn, kn, vb, sink, *, batch, seq_all, ctx_len):
    blocks = seq_all // WINDOW
    ctx_blocks = ctx_len // WINDOW
    last = blocks - 1
    qw = A_GROUP * A_HEAD_DIM

    def kv_spec(shift):
        def imap(b, h, j):
            return (b * blocks + jnp.clip(j + shift, ctx_blocks, last), h)
        return pl.BlockSpec((WINDOW, A_HEAD_DIM), imap)

    ctx_spec = pl.BlockSpec((ctx_len, A_HEAD_DIM), lambda b, h, j: (b * (seq_all // ctx_len), h))
    return pl.pallas_call(
        functools.partial(_attn_body, ctx_blocks=ctx_blocks, lat_len=seq_all - ctx_len),
        grid=(batch, A_KV_HEADS, blocks),
        in_specs=[pl.BlockSpec(memory_space=pltpu.SMEM),
                  pl.BlockSpec((WINDOW, qw), lambda b, h, j: (b * blocks + j, h)),
                  kv_spec(-1), kv_spec(0), kv_spec(1), ctx_spec,
                  kv_spec(-1), kv_spec(0), kv_spec(1), ctx_spec],
        out_specs=pl.BlockSpec((WINDOW, qw), lambda b, h, j: (b * blocks + j, h)),
        out_shape=jax.ShapeDtypeStruct((batch * seq_all, A_Q), BF16),
        compiler_params=_cp("arbitrary", "arbitrary", "arbitrary"),
        name="window_attention",
    )(sink, qn, kn, kn, kn, kn, vb, vb, vb, vb)


def _conv_body(x_ref, w_ref, o_ref, *, ctx_len, k_first_block):
    x = x_ref[...]
    t_len = x.shape[0]
    t = lax.broadcasted_iota(jnp.int32, x.shape, 0)
    half = B_CONV_W // 2
    acc = x * w_ref[half:half + 1, :]
    for d in range(-half, half + 1):
        if d == 0:
            continue
        xs = pltpu.roll(x, (-d) % t_len, axis=0)
        u = t + d
        ok = ((t < ctx_len) & (u >= 0) & (u < ctx_len)) | ((t >= ctx_len) & (u >= ctx_len) & (u < t_len))
        acc = acc + jnp.where(ok, xs, 0.0) * w_ref[half + d:half + d + 1, :]
    k_scale = jnp.where(pl.program_id(1) >= k_first_block, B_QK_DIM ** -0.5, 1.0).astype(F32)
    o_ref[...] = (_silu(acc) * k_scale).astype(o_ref.dtype)


def _mlstm_conv(proj, conv_w, *, batch, seq_all, ctx_len):
    first = (A_Q + 2 * A_KV) // LANE_BLOCK
    nblk = 2 * B_QK // LANE_BLOCK
    return pl.pallas_call(
        functools.partial(_conv_body, ctx_len=ctx_len, k_first_block=B_QK // LANE_BLOCK),
        grid=(batch, nblk),
        in_specs=[pl.BlockSpec((seq_all, LANE_BLOCK), lambda b, c: (b, first + c)),
                  pl.BlockSpec((B_CONV_W, LANE_BLOCK), lambda b, c: (0, c))],
        out_specs=pl.BlockSpec((seq_all, LANE_BLOCK), lambda b, c: (b, c)),
        out_shape=jax.ShapeDtypeStruct((batch * seq_all, 2 * B_QK), BF16),
        compiler_params=_cp("arbitrary", "arbitrary"),
        name="mlstm_conv",
    )(proj, conv_w)


def _gates_body(raw_ref, b_ref, o_ref):
    g = raw_ref[...] + b_ref[...]
    g = GATE_CAP * jnp.tanh(g / GATE_CAP)
    lane = lax.broadcasted_iota(jnp.int32, g.shape, 1)
    is_forget = (lane // B_HEADS) % 2 == 1
    log_sig = jnp.minimum(g, 0.0) - jnp.log(1.0 + jnp.exp(-jnp.abs(g)))
    o_ref[...] = jnp.where(is_forget, log_sig, g)


def _mlstm_gates(raw, gate_b):
    rows, width = raw.shape
    tr = _row_tile_size(rows, 2048)
    return pl.pallas_call(
        _gates_body,
        grid=(rows // tr,),
        in_specs=[pl.BlockSpec((tr, width), lambda i: (i, 0)),
                  pl.BlockSpec((1, width), lambda i: (0, 0))],
        out_specs=pl.BlockSpec((tr, width), lambda i: (i, 0)),
        out_shape=jax.ShapeDtypeStruct((rows, width), F32),
        compiler_params=_cp("arbitrary"),
        name="mlstm_gates",
    )(raw, gate_b)


def _mlstm_direction(q_ref, k_ref, v_ref, g_ref, o_ref, st_ref, n_ref, m_ref, d, rev):
    c_len = q_ref.shape[0]
    q = q_ref[...]
    k = k_ref[...]
    v = v_ref[...].astype(BF16)
    ig = g_ref[2 * d:2 * d + 1, :]
    lf = g_ref[2 * d + 1:2 * d + 2, :]
    r = lax.broadcasted_iota(jnp.int32, (c_len, c_len), 0)
    c = lax.broadcasted_iota(jnp.int32, (c_len, c_len), 1)
    eye = r == c
    tri = (c >= r) if rev else (c <= r)
    tri_t = (r >= c) if rev else (r <= c)
    lf_col = jnp.sum(jnp.where(eye, lf, 0.0), axis=1, keepdims=True)
    ig_col = jnp.sum(jnp.where(eye, ig, 0.0), axis=1, keepdims=True)
    b_col = jnp.sum(jnp.where(tri, lf, 0.0), axis=1, keepdims=True)
    b_row = jnp.sum(jnp.where(tri_t, lf_col, 0.0), axis=0, keepdims=True)
    m_prev = m_ref[d]
    d_st = b_col + m_prev
    d_in = jnp.where(tri, b_col - b_row + ig, NEG_INF)
    m_t = jnp.maximum(d_st, jnp.max(d_in, axis=1, keepdims=True))
    s = jnp.exp(d_in - m_t) * lax.dot_general(q, k, NT_DIMS, preferred_element_type=F32)
    w_st = jnp.exp(d_st - m_t)
    st = st_ref[d]
    num = (jnp.dot(s.astype(BF16), v, preferred_element_type=F32)
           + w_st * jnp.dot(q, st.astype(BF16), preferred_element_type=F32))
    qn = jnp.sum(q.astype(F32) * n_ref[d], axis=1, keepdims=True)
    den = jnp.sum(s, axis=1, keepdims=True) + w_st * qn
    o_ref[...] = num / jnp.maximum(jnp.abs(den), jnp.exp(-m_t))
    b_end = jnp.sum(lf, axis=1, keepdims=True)
    d_up = b_end - b_col + ig_col
    m_new = jnp.maximum(b_end + m_prev, jnp.max(d_up, axis=0, keepdims=True))
    w_up = jnp.exp(d_up - m_new)
    a = jnp.exp(b_end + m_prev - m_new)
    kw = k.astype(F32) * w_up
    st_ref[d] = a * st + lax.dot_general(kw.astype(BF16), v, TN_DIMS, preferred_element_type=F32)
    n_ref[d] = a * n_ref[d] + jnp.sum(kw, axis=0, keepdims=True)
    m_ref[d] = m_new


def _mlstm_body(qf, kf, vf, gf, qb, kb, vb, gb, of, ob, st_ref, n_ref, m_ref):
    @pl.when(pl.program_id(2) == 0)
    def _():
        st_ref[...] = jnp.zeros_like(st_ref)
        n_ref[...] = jnp.zeros_like(n_ref)
        m_ref[...] = jnp.zeros_like(m_ref)

    _mlstm_direction(qf, kf, vf, gf, of, st_ref, n_ref, m_ref, 0, False)
    _mlstm_direction(qb, kb, vb, gb, ob, st_ref, n_ref, m_ref, 1, True)


def _scan_orders(n_chunks, ctx_chunks):
    fwd = lambda j: j
    bwd = lambda j: jnp.where(j < ctx_chunks, ctx_chunks - 1 - j, n_chunks - 1 - (j - ctx_chunks))
    return fwd, bwd


def _mlstm_scan(qk, proj, gates_t, *, batch, seq_all, ctx_len):
    ch = MLSTM_CHUNK
    n_chunks = seq_all // ch
    fwd, bwd = _scan_orders(n_chunks, ctx_len // ch)
    v_first = (A_Q + 2 * A_KV + 2 * B_QK) // B_V_DIM

    def specs(order):
        return [pl.BlockSpec((ch, B_QK_DIM), lambda b, h, j: (b * n_chunks + order(j), h)),
                pl.BlockSpec((ch, B_QK_DIM), lambda b, h, j: (b * n_chunks + order(j), B_HEADS + h)),
                pl.BlockSpec((ch, B_V_DIM), lambda b, h, j: (b * n_chunks + order(j), v_first + h)),
                pl.BlockSpec((None, None, 4, ch), lambda b, h, j: (b, h, 0, order(j)))]

    def out_spec(order):
        return pl.BlockSpec((ch, B_V_DIM), lambda b, h, j: (b * n_chunks + order(j), h))

    out = jax.ShapeDtypeStruct((batch * seq_all, B_V), F32)
    return pl.pallas_call(
        _mlstm_body,
        grid=(batch, B_HEADS, n_chunks),
        in_specs=specs(fwd) + specs(bwd),
        out_specs=[out_spec(fwd), out_spec(bwd)],
        out_shape=[out, out],
        scratch_shapes=[pltpu.VMEM((2, B_QK_DIM, B_V_DIM), F32),
                        pltpu.VMEM((2, 1, B_QK_DIM), F32),
                        pltpu.VMEM((2, 1, 1), F32)],
        compiler_params=_cp("arbitrary", "arbitrary", "arbitrary"),
        name="mlstm_scan",
    )(qk, qk, proj, gates_t, qk, qk, proj, gates_t)


def _headnorm_gate_body(a_ref, b_ref, gate_ref, g_ref, o_ref, *, head_dim, gate_fn):
    for hd in range(a_ref.shape[1] // head_dim):
        sl = slice(hd * head_dim, (hd + 1) * head_dim)
        y = _rms(a_ref[:, sl] + b_ref[:, sl], g_ref[:, sl])
        o_ref[:, sl] = (y * gate_fn(gate_ref[:, sl])).astype(o_ref.dtype)


def _headnorm_gate(a, b, proj, gate_col0, norm_g, *, head_dim, gate_fn, name):
    rows, width = a.shape
    tc = min(1024, width)
    return pl.pallas_call(
        functools.partial(_headnorm_gate_body, head_dim=head_dim, gate_fn=gate_fn),
        grid=(rows // ROW_TILE, width // tc),
        in_specs=[pl.BlockSpec((ROW_TILE, tc), lambda i, c: (i, c)),
                  pl.BlockSpec((ROW_TILE, tc), lambda i, c: (i, c)),
                  pl.BlockSpec((ROW_TILE, tc), lambda i, c: (i, gate_col0 // tc + c)),
                  pl.BlockSpec((1, tc), lambda i, c: (0, c))],
        out_specs=pl.BlockSpec((ROW_TILE, tc), lambda i, c: (i, c)),
        out_shape=jax.ShapeDtypeStruct((rows, width), BF16),
        compiler_params=_cp("arbitrary", "arbitrary"),
        name=name,
    )(a, b, proj, norm_g)


def _hgrn_direction(q_ref, i_ref, f_ref, lb_ref, o_ref, st_ref, d, rev):
    rows = q_ref.shape[0]
    ch = HGRN_CHUNK
    lb = lb_ref[d:d + 1, :]
    q = _silu(q_ref[...])
    f = lb + (1.0 - lb) * _sigmoid(f_ref[...])
    kk = 1.0 - f
    r = lax.broadcasted_iota(jnp.int32, (rows, rows), 0)
    c = lax.broadcasted_iota(jnp.int32, (rows, rows), 1)
    visible = (r // ch == c // ch) & ((c >= r) if rev else (c <= r))
    a_cum = jnp.log(f)
    pos = lax.broadcasted_iota(jnp.int32, a_cum.shape, 0) % ch
    sh = 1
    while sh < ch:
        if rev:
            a_cum = a_cum + jnp.where(pos < ch - sh, pltpu.roll(a_cum, rows - sh, axis=0), 0.0)
        else:
            a_cum = a_cum + jnp.where(pos >= sh, pltpu.roll(a_cum, sh, axis=0), 0.0)
        sh *= 2
    n_chunks = rows // ch
    end_row = 0 if rev else ch - 1
    mid_row = ch // 2
    width = a_cum.shape[1]

    def per_chunk_row(row):
        return jnp.concatenate([jnp.broadcast_to(a_cum[ci * ch + row:ci * ch + row + 1, :], (ch, width))
                                for ci in range(n_chunks)], axis=0)

    a_mid = per_chunk_row(mid_row)
    a_end = per_chunk_row(end_row)
    v = i_ref[...].astype(BF16)
    q_in = (q * jnp.exp(a_cum)).astype(BF16)
    qh = (q * jnp.exp(a_cum - a_mid)).astype(BF16)
    kh = (kk * jnp.exp(a_mid - a_cum)).astype(BF16)
    ke = (kk * jnp.exp(a_end - a_cum)).astype(BF16)
    order = range(n_chunks - 1, -1, -1) if rev else range(n_chunks)
    for hd in range(HGRN_HEADS):
        cs = slice(hd * C_HEAD_DIM, (hd + 1) * C_HEAD_DIM)
        att = jnp.where(visible, lax.dot_general(qh[:, cs], kh[:, cs], NT_DIMS, preferred_element_type=F32), 0.0)
        o_local = jnp.dot(att.astype(BF16), v[:, cs], preferred_element_type=F32)
        updates = {ci: lax.dot_general(v[ci * ch:(ci + 1) * ch, cs], ke[ci * ch:(ci + 1) * ch, cs], TN_DIMS,
                                       preferred_element_type=F32) for ci in order}
        st = st_ref[d, hd]
        for ci in order:
            rs = slice(ci * ch, (ci + 1) * ch)
            o_ref[rs, cs] = o_local[rs, :] + lax.dot_general(q_in[rs, cs], st.astype(BF16), NT_DIMS,
                                                              preferred_element_type=F32)
            st = jnp.exp(a_cum[ci * ch + end_row:ci * ch + end_row + 1, cs]) * st + updates[ci]
        st_ref[d, hd] = st


def _hgrn_body(qf, vf, ff, qb, vb, fb, lb_ref, of, ob, st_ref):
    @pl.when(pl.program_id(2) == 0)
    def _():
        st_ref[...] = jnp.zeros_like(st_ref)

    _hgrn_direction(qf, vf, ff, lb_ref, of, st_ref, 0, False)
    _hgrn_direction(qb, vb, fb, lb_ref, ob, st_ref, 1, True)


def _hgrn_scan(proj, lb, *, batch, seq_all, ctx_len, d_model):
    blk = HGRN_BLOCK
    n_blocks = seq_all // blk
    fwd, bwd = _scan_orders(n_blocks, ctx_len // blk)
    wcols = HGRN_HEADS * C_HEAD_DIM
    per = d_model // wcols

    def spec(order, part):
        return pl.BlockSpec((blk, wcols), lambda b, h, j: (b * n_blocks + order(j), part * per + h))

    def out_spec(order):
        return pl.BlockSpec((blk, wcols), lambda b, h, j: (b * n_blocks + order(j), h))

    out = jax.ShapeDtypeStruct((batch * seq_all, d_model), F32)
    return pl.pallas_call(
        _hgrn_body,
        grid=(batch, per, n_blocks),
        in_specs=[spec(fwd, 0), spec(fwd, 1), spec(fwd, 3), spec(bwd, 0), spec(bwd, 1), spec(bwd, 4),
                  pl.BlockSpec((2, wcols), lambda b, h, j: (0, h))],
        out_specs=[out_spec(fwd), out_spec(bwd)],
        out_shape=[out, out],
        scratch_shapes=[pltpu.VMEM((2, HGRN_HEADS, C_HEAD_DIM, C_HEAD_DIM), F32)],
        compiler_params=_cp("arbitrary", "arbitrary", "arbitrary"),
        name="hgrn_scan",
    )(proj, proj, proj, proj, proj, proj, lb)


def _pack_halves(v):
    half = v.shape[1] // 2
    bits = lax.bitcast_convert_type(v.astype(BF16).astype(F32), jnp.uint32)
    return (bits[:, :half] >> 16) | (bits[:, half:] & jnp.uint32(0xFFFF0000))


def _unpack_halves(p):
    lo = lax.bitcast_convert_type(p << 16, F32)
    hi = lax.bitcast_convert_type(p & jnp.uint32(0xFFFF0000), F32)
    return lo, hi


def _ffn_router_body(x_ref, g_ref, mod_ref, rwt_ref, rb_ref, hb_ref, hp_ref, idx_ref, wgt_ref, rank_ref, cnt_ref):
    @pl.when(pl.program_id(0) == 0)
    def _():
        cnt_ref[...] = jnp.zeros_like(cnt_ref)

    y = _rms(x_ref[...], g_ref[...])
    h2 = y * (1.0 + mod_ref[4:5, :]) + mod_ref[3:4, :]
    hb = h2.astype(BF16)
    hb_ref[...] = hb
    hp_ref[...] = _pack_halves(hb)
    tm = h2.shape[0]
    logits = lax.dot_general(rwt_ref[...], h2, NT_DIMS, precision=lax.Precision.HIGHEST,
                             preferred_element_type=F32)
    s = _sigmoid(logits)
    sel = s + rb_ref[...]
    iota_g = lax.broadcasted_iota(jnp.int32, (GROUP_SIZE, tm), 0)
    group_scores = []
    for g in range(N_GROUPS):
        xg = sel[g * GROUP_SIZE:(g + 1) * GROUP_SIZE, :]
        m1 = jnp.max(xg, axis=0, keepdims=True)
        i1 = jnp.min(jnp.where(xg == m1, iota_g, GROUP_SIZE), axis=0, keepdims=True)
        m2 = jnp.max(jnp.where(iota_g == i1, NEG_INF, xg), axis=0, keepdims=True)
        group_scores.append(m1 + m2)
    gsc = jnp.concatenate(group_scores, axis=0)
    iota_n = lax.broadcasted_iota(jnp.int32, (N_GROUPS, tm), 0)
    gmask = jnp.zeros((N_GROUPS, tm), F32)
    for _ in range(TOPK_GROUPS):
        m = jnp.max(gsc, axis=0, keepdims=True)
        i = jnp.min(jnp.where(gsc == m, iota_n, N_GROUPS), axis=0, keepdims=True)
        hit = iota_n == i
        gmask = jnp.where(hit, 1.0, gmask)
        gsc = jnp.where(hit, NEG_INF, gsc)
    emask = jnp.concatenate([jnp.broadcast_to(gmask[g:g + 1, :], (GROUP_SIZE, tm)) for g in range(N_GROUPS)], axis=0)
    cur = jnp.where(emask > 0.0, sel, NEG_INF)
    iota_e = lax.broadcasted_iota(jnp.int32, (N_EXPERTS, tm), 0)
    idx_rows, w_rows, hits = [], [], []
    for _ in range(TOP_K):
        m = jnp.max(cur, axis=0, keepdims=True)
        i = jnp.min(jnp.where(cur == m, iota_e, N_EXPERTS), axis=0, keepdims=True)
        hit = iota_e == i
        idx_rows.append(i)
        hits.append(hit)
        w_rows.append(jnp.sum(jnp.where(hit, s, 0.0), axis=0, keepdims=True))
        cur = jnp.where(hit, NEG_INF, cur)
    w = jnp.concatenate(w_rows, axis=0)
    idx_ref[...] = jnp.concatenate(idx_rows, axis=0)
    wgt_ref[...] = ROUTED_SCALE * w / jnp.sum(w, axis=0, keepdims=True)
    chosen = jnp.zeros((N_EXPERTS, tm), F32)
    for hit in hits:
        chosen = jnp.where(hit, 1.0, chosen)
    earlier = (lax.broadcasted_iota(jnp.int32, (tm, tm), 0) < lax.broadcasted_iota(jnp.int32, (tm, tm), 1))
    before = cnt_ref[...] + jnp.dot(chosen.astype(BF16), earlier.astype(BF16), preferred_element_type=F32)
    rank_ref[...] = jnp.concatenate(
        [jnp.sum(jnp.where(hit, before, 0.0), axis=0, keepdims=True) for hit in hits], axis=0).astype(jnp.int32)
    cnt_ref[...] = cnt_ref[...] + jnp.sum(chosen, axis=1, keepdims=True)


def _ffn_router(x, g, mod, sel, router_wt, router_b, layer):
    rows, d = x.shape
    return pl.pallas_call(
        _ffn_router_body,
        grid=(rows // ROW_TILE,),
        in_specs=[pl.BlockSpec((ROW_TILE, d), lambda i: (i, 0)),
                  pl.BlockSpec((1, d), lambda i: (0, 0)),
                  pl.BlockSpec((None, 6, d), lambda i: (sel(i), 0, 0)),
                  pl.BlockSpec((None, N_EXPERTS, d), lambda i: (layer, 0, 0)),
                  pl.BlockSpec((None, N_EXPERTS, 1), lambda i: (layer, 0, 0))],
        out_specs=[pl.BlockSpec((ROW_TILE, d), lambda i: (i, 0)),
                   pl.BlockSpec((ROW_TILE, d // 2), lambda i: (i, 0)),
                   pl.BlockSpec((TOP_K, ROW_TILE), lambda i: (0, i)),
                   pl.BlockSpec((TOP_K, ROW_TILE), lambda i: (0, i)),
                   pl.BlockSpec((TOP_K, ROW_TILE), lambda i: (0, i)),
                   pl.BlockSpec((N_EXPERTS, 1), lambda i: (0, 0))],
        out_shape=[jax.ShapeDtypeStruct((rows, d), BF16),
                   jax.ShapeDtypeStruct((rows, d // 2), jnp.uint32),
                   jax.ShapeDtypeStruct((TOP_K, rows), jnp.int32),
                   jax.ShapeDtypeStruct((TOP_K, rows), F32),
                   jax.ShapeDtypeStruct((TOP_K, rows), jnp.int32),
                   jax.ShapeDtypeStruct((N_EXPERTS, 1), F32)],
        compiler_params=_cp("arbitrary"),
        name=f"ffn_router{layer}",
    )(x, g, mod, router_wt, router_b)


def _dispatch(idx_t, rank_t, counts):
    tm = MOE_TILE
    slots = idx_t.shape[0] * idx_t.shape[1]
    n_tiles = slots // tm
    counts = counts.reshape(-1).astype(jnp.int32)
    ends = jnp.cumsum(counts)
    starts = ends - counts
    is_e = idx_t[:, :, None] == jnp.arange(N_EXPERTS, dtype=jnp.int32)
    pos = (jnp.sum(jnp.where(is_e, starts, 0), axis=-1) + rank_t).T.reshape(-1)
    cuts = jnp.sort(jnp.concatenate([jnp.arange(n_tiles, dtype=jnp.int32) * tm, ends[:-1]]))
    nxt = jnp.concatenate([cuts[1:], jnp.full((1,), slots, jnp.int32)])
    v_tile = jnp.minimum(cuts // tm, n_tiles - 1)
    v_exp = jnp.minimum(jnp.sum((ends[None, :] <= cuts[:, None]).astype(jnp.int32), axis=1), N_EXPERTS - 1)
    v_lo = cuts - v_tile * tm
    v_hi = nxt - v_tile * tm
    n_vis = cuts.shape[0]
    v_first = (v_exp != jnp.concatenate([jnp.full((1,), -1, jnp.int32), v_exp[:-1]])).astype(jnp.int32)
    order = jnp.arange(n_vis, dtype=jnp.int32)
    next_first = lax.cummin(jnp.where(v_first == 1, order, n_vis)[::-1])[::-1]
    after = jnp.concatenate([next_first[1:], jnp.full((1,), n_vis, jnp.int32)])
    v_next = jnp.where(after < n_vis, v_exp[jnp.minimum(after, n_vis - 1)], -1)
    return (pos.astype(jnp.int32), v_tile.astype(jnp.int32), v_exp.astype(jnp.int32), v_lo, v_hi, v_first,
            v_next.astype(jnp.int32))


def _push_body(pos_ref, h_ref, xs_hbm, sem):
    i = pl.program_id(0)
    tt = h_ref.shape[0]
    base = i * tt * TOP_K

    def issue(r, carry):
        src = h_ref.at[pl.ds(r, 1)]
        for k in range(TOP_K):
            pltpu.make_async_copy(src, xs_hbm.at[pl.ds(pos_ref[base + r * TOP_K + k], 1)], sem).start(
                priority=k % DMA_PRIORITIES)
        return carry

    lax.fori_loop(0, tt, issue, 0)
    for _ in range(TOP_K):
        pltpu.make_async_copy(h_ref, xs_hbm.at[pl.ds(0, tt)], sem).wait()


def _push(h2f, pos, layer):
    rows, d = h2f.shape
    tt = PUSH_TOKENS
    grid_spec = pltpu.PrefetchScalarGridSpec(
        num_scalar_prefetch=1,
        grid=(rows // tt,),
        in_specs=[pl.BlockSpec((tt, d), lambda i, ps: (i, 0))],
        out_specs=pl.BlockSpec(memory_space=pl.ANY),
        scratch_shapes=[pltpu.SemaphoreType.DMA(())],
    )
    return pl.pallas_call(
        _push_body,
        grid_spec=grid_spec,
        out_shape=jax.ShapeDtypeStruct((rows * TOP_K, d), h2f.dtype),
        compiler_params=_cp("arbitrary"),
        name=f"push{layer}",
    )(pos, h2f)


def _expert_body(vt_ref, ve_ref, lo_ref, hi_ref, first_ref, next_ref, x_ref, wg_hbm, wu_hbm, wd_hbm, y_ref,
                 wgf, wuf, wdf, wgb, wub, wdb, sem, *, layer):
    v = pl.program_id(0)
    lo = lo_ref[v]
    hi = hi_ref[v]

    def weight_copies(e):
        return (pltpu.make_async_copy(wg_hbm.at[layer, e], wgf, sem),
                pltpu.make_async_copy(wu_hbm.at[layer, e], wuf, sem),
                pltpu.make_async_copy(wd_hbm.at[layer, e], wdf, sem))

    @pl.when(v == 0)
    def _():
        for cp in weight_copies(ve_ref[0]):
            cp.start()

    @pl.when(first_ref[v] == 1)
    def _():
        for cp in weight_copies(ve_ref[v]):
            cp.wait()
        wgb[...] = wgf[...].astype(BF16)
        wub[...] = wuf[...].astype(BF16)
        wdb[...] = wdf[...].astype(BF16)

        @pl.when(next_ref[v] >= 0)
        def _():
            for cp in weight_copies(next_ref[v]):
                cp.start()

    @pl.when(hi > lo)
    def _():
        x_lo, x_hi = _unpack_halves(x_ref[...])
        x_lo = x_lo.astype(BF16)
        x_hi = x_hi.astype(BF16)
        half = x_lo.shape[1]
        g = (jnp.dot(x_lo, wgb[:half, :], preferred_element_type=F32)
             + jnp.dot(x_hi, wgb[half:, :], preferred_element_type=F32))
        u = (jnp.dot(x_lo, wub[:half, :], preferred_element_type=F32)
             + jnp.dot(x_hi, wub[half:, :], preferred_element_type=F32))
        hid = (_silu(g) * u).astype(BF16)
        y = _pack_halves(jnp.dot(hid, wdb[...], preferred_element_type=F32))
        r = lax.broadcasted_iota(jnp.int32, (y.shape[0], 1), 0)
        mine = (r >= lo) & (r < hi)

        @pl.when(lo == 0)
        def _():
            y_ref[...] = jnp.where(mine, y, jnp.uint32(0))

        @pl.when(lo > 0)
        def _():
            y_ref[...] = jnp.where(mine, y, y_ref[...])


def _experts(xs, visits, wg, wu, wd, layer):
    slots, dp = xs.shape
    d = wg.shape[2]
    de = wg.shape[3]
    tm = MOE_TILE
    hbm = pl.BlockSpec(memory_space=pl.ANY)
    grid_spec = pltpu.PrefetchScalarGridSpec(
        num_scalar_prefetch=len(visits),
        grid=(visits[0].shape[0],),
        in_specs=[pl.BlockSpec((tm, dp), lambda v, vt, *_: (vt[v], 0)), hbm, hbm, hbm],
        out_specs=pl.BlockSpec((tm, dp), lambda v, vt, *_: (vt[v], 0)),
        scratch_shapes=[pltpu.VMEM((d, de), F32), pltpu.VMEM((d, de), F32), pltpu.VMEM((de, d), F32),
                        pltpu.VMEM((d, de), BF16), pltpu.VMEM((d, de), BF16), pltpu.VMEM((de, d), BF16),
                        pltpu.SemaphoreType.DMA(())],
    )
    return pl.pallas_call(
        functools.partial(_expert_body, layer=layer),
        grid_spec=grid_spec,
        out_shape=jax.ShapeDtypeStruct((slots, dp), jnp.uint32),
        compiler_params=_cp("arbitrary"),
        name=f"experts{layer}",
    )(*visits, xs, wg, wu, wd)


def _combine_body(pos_ref, w_ref, y_hbm, o_ref, buf, sem):
    i = pl.program_id(0)
    tt = o_ref.shape[0]
    n = TOP_K * tt
    slot = i % 2
    half = buf.shape[3]
    group = 8

    def issue_rows(step, into, r0, rows):
        for j in range(rows):
            for k in range(TOP_K):
                pltpu.make_async_copy(y_hbm.at[pl.ds(pos_ref[step * n + (r0 + j) * TOP_K + k], 1)],
                                      buf.at[into, k, pl.ds(r0 + j, 1)], sem.at[into]).start(
                    priority=k % DMA_PRIORITIES)

    @pl.when(i == 0)
    def _():
        def first(g, carry):
            issue_rows(0, 0, g * group, group)
            return carry

        lax.fori_loop(0, tt // group, first, 0)

    for k in range(TOP_K):
        pltpu.make_async_copy(y_hbm.at[pl.ds(0, tt)], buf.at[slot, k], sem.at[slot]).wait()
    last = pl.num_programs(0) - 1
    nxt = jnp.minimum(i + 1, last)

    def body(g, carry):
        r0 = pl.multiple_of(g * group, group)
        issue_rows(nxt, 1 - slot, r0, group)
        w = w_ref[pl.ds(r0, group), :]
        acc_lo, acc_hi = _unpack_halves(buf[slot, 0, pl.ds(r0, group), :])
        acc_lo = w[:, 0:1] * acc_lo
        acc_hi = w[:, 0:1] * acc_hi
        for k in range(1, TOP_K):
            y_lo, y_hi = _unpack_halves(buf[slot, k, pl.ds(r0, group), :])
            acc_lo = acc_lo + w[:, k:k + 1] * y_lo
            acc_hi = acc_hi + w[:, k:k + 1] * y_hi
        o_ref[pl.ds(r0, group), :half] = acc_lo
        o_ref[pl.ds(r0, group), half:] = acc_hi
        return carry

    lax.fori_loop(0, tt // group, body, 0)

    @pl.when(i == last)
    def _():
        for k in range(TOP_K):
            pltpu.make_async_copy(y_hbm.at[pl.ds(0, tt)], buf.at[1 - slot, k], sem.at[1 - slot]).wait()


def _combine(ys, pos_flat, wgt, rows, layer):
    dp = ys.shape[1]
    d = 2 * dp
    tt = COMBINE_TOKENS
    grid_spec = pltpu.PrefetchScalarGridSpec(
        num_scalar_prefetch=1,
        grid=(rows // tt,),
        in_specs=[pl.BlockSpec((tt, TOP_K), lambda i, ps: (i, 0)),
                  pl.BlockSpec(memory_space=pl.ANY)],
        out_specs=pl.BlockSpec((tt, d), lambda i, ps: (i, 0)),
        scratch_shapes=[pltpu.VMEM((2, TOP_K, tt, dp), jnp.uint32), pltpu.SemaphoreType.DMA((2,))],
    )
    return pl.pallas_call(
        _combine_body,
        grid_spec=grid_spec,
        out_shape=jax.ShapeDtypeStruct((rows, d), F32),
        compiler_params=_cp("arbitrary"),
        name=f"combine{layer}",
    )(pos_flat, wgt, ys)


def _moe(x1, mod, sel, layer, norm_g, router_w, router_b, exp_wg, exp_wu, exp_wd, s_gate, s_up, s_down, tm_big):
    rows, d = x1.shape
    router_wt = jnp.swapaxes(router_w, 1, 2)
    h2b, h2p, idx_t, wgt_t, rank_t, counts = _ffn_router(x1, norm_g[layer][None], mod, sel, router_wt,
                                                         router_b[:, :, None], layer)
    pos, *visits = _dispatch(idx_t, rank_t, counts)
    xs = _push(h2p, pos, layer)
    ys = _experts(xs, visits, exp_wg, exp_wu, exp_wd, layer)
    routed = _combine(ys, pos, wgt_t.T, rows, layer)
    hs = _glu(h2b, s_gate, s_up, layer, tm=tm_big, name=f"shared_glu{layer}")
    return _linear([hs], s_down, layer, n_cols=d, tm=ROW_TILE, out_dtype=F32, name=f"shared_down{layer}",
                   res=(x1, mod, 5, routed, sel), tn=4 * COL_TILE)


def _rope_tables(ctx_len, lat_len):
    rows = lat_len // GRID_W
    pos_r = np.repeat(np.arange(rows), GRID_W).astype(np.float32)
    pos_c = np.tile(np.arange(GRID_W), rows).astype(np.float32)
    n = A_HEAD_DIM // 4
    inv = (np.float32(ROPE_BASE) ** (-np.arange(n, dtype=np.float32) / np.float32(n))).astype(np.float32)
    ang = np.concatenate([pos_r[:, None] * inv, pos_c[:, None] * inv], axis=-1).astype(np.float32)
    cos, sin = np.cos(ang).astype(np.float32), np.sin(ang).astype(np.float32)
    cos_t = np.concatenate([np.ones((ctx_len, A_HEAD_DIM), np.float32), np.concatenate([cos, cos], axis=-1)], axis=0)
    sin_t = np.concatenate([np.zeros((ctx_len, A_HEAD_DIM), np.float32), np.concatenate([-sin, sin], axis=-1)], axis=0)
    return jnp.asarray(cos_t), jnp.asarray(sin_t)


def kernel(x, c, ctx, c_ctx, ada_w, ada_b, norm_mix_g, norm_ffn_g, ab_w_in, ab_w_out, a_q_norm_g, a_k_norm_g,
           a_sink, b_conv_w, b_gate_b, b_norm_g, c_w_in, c_w_out, c_lb_logits, c_norm_g, router_w, router_b,
           exp_w_gate, exp_w_up, exp_w_down, shared_w_gate, shared_w_up, shared_w_down):
    batch, lat_len, d = x.shape
    ctx_len = ctx.shape[1]
    depth = ada_w.shape[0]
    seq_all = ctx_len + lat_len
    rows = batch * seq_all
    assert ctx_len % ROW_TILE == 0 and lat_len % ROW_TILE == 0 and ctx_len % MLSTM_CHUNK == 0
    assert lat_len % GRID_W == 0 and d % (HGRN_HEADS * C_HEAD_DIM) == 0
    tiles_per_sample = seq_all // ROW_TILE
    ctx_tiles = ctx_len // ROW_TILE
    tm_big = _row_tile_size(rows, 1152)
    tm_in = _row_tile_size(rows, 576)

    def sel(i):
        return jnp.where(i % tiles_per_sample < ctx_tiles, batch, i // tiles_per_sample)

    cos_t, sin_t = _rope_tables(ctx_len, lat_len)
    lb_soft = jax.nn.softmax(c_lb_logits.astype(F32), axis=0)
    lower_bounds = jnp.cumsum(lb_soft, axis=0) - lb_soft[0:1]
    cond = jnp.concatenate([jax.nn.silu(c), jax.nn.silu(c_ctx)[None]], axis=0)
    cond = jnp.pad(cond, ((0, 16 - cond.shape[0]), (0, 0))).astype(BF16)

    lat_tiles = lat_len // ROW_TILE

    def lat_map(i):
        return (i // lat_tiles) * tiles_per_sample + ctx_tiles + i % lat_tiles

    xa = jnp.concatenate([ctx, x], axis=1).reshape(rows, d)
    for layer in range(depth):
        last = layer == depth - 1
        tail = dict(row_map=lat_map, out_rows=batch * lat_len) if last else {}
        tail_sel = (lambda i: i // lat_tiles) if last else sel
        mod = _ada(cond, ada_w, ada_b, layer)[:batch + 1].reshape(batch + 1, 6, d)
        h = _modulate(xa, norm_mix_g[layer][None], mod, sel, shift_row=0, scale_row=1, name=f"mod_mix{layer}")
        if layer % 2 == 0:
            e = layer // 2
            proj = _linear([h], ab_w_in, e, n_cols=AB_MAIN, tm=tm_in, out_dtype=F32, name=f"ab_in{layer}",
                           tn=2 * COL_TILE)
            w_gates = jnp.pad(ab_w_in[e][:, AB_MAIN:], ((0, 0), (0, LANE_BLOCK - AB_GATES)))[None]
            graw = _linear([h], w_gates, 0, n_cols=LANE_BLOCK, tm=tm_big, out_dtype=F32, name=f"ab_gates{layer}")
            qn, kn, vb = _attn_prep(proj, cos_t, sin_t, a_q_norm_g[e][None], a_k_norm_g[e][None], tiles_per_sample)
            ya = _attention(qn, kn, vb, a_sink[e], batch=batch, seq_all=seq_all, ctx_len=ctx_len)
            qk = _mlstm_conv(proj, b_conv_w[e], batch=batch, seq_all=seq_all, ctx_len=ctx_len)
            gate_b = jnp.pad(b_gate_b[e], (0, LANE_BLOCK - AB_GATES))[None]
            gates = _mlstm_gates(graw, gate_b)[:, :AB_GATES]
            gates_t = gates.reshape(batch, seq_all, 4, B_HEADS).transpose(0, 3, 2, 1)
            hf, hb = _mlstm_scan(qk, proj, gates_t, batch=batch, seq_all=seq_all, ctx_len=ctx_len)
            yb = _headnorm_gate(hf, hb, proj, AB_MAIN - B_V, b_norm_g[e][None], head_dim=B_V_DIM,
                                gate_fn=_sigmoid, name="mlstm_out")
            xa = _linear([ya, yb], ab_w_out, e, n_cols=d, tm=ROW_TILE, out_dtype=F32, name=f"ab_out{layer}",
                         res=(xa, mod, 2, None, tail_sel), tn=2 * COL_TILE, **tail)
        else:
            o = layer // 2
            proj = _linear([h], c_w_in, o, n_cols=5 * d, tm=tm_in, out_dtype=F32, name=f"c_in{layer}",
                           tn=2 * COL_TILE)
            of, ob = _hgrn_scan(proj, lower_bounds[layer], batch=batch, seq_all=seq_all, ctx_len=ctx_len, d_model=d)
            yc = _headnorm_gate(of, ob, proj, 2 * d, c_norm_g[o][None], head_dim=C_HEAD_DIM, gate_fn=_silu,
                                name="hgrn_out")
            xa = _linear([yc], c_w_out, o, n_cols=d, tm=ROW_TILE, out_dtype=F32, name=f"c_out{layer}",
                         res=(xa, mod, 2, None, tail_sel), tn=2 * COL_TILE, **tail)
        xa = _moe(xa, mod, tail_sel, layer, norm_ffn_g, router_w, router_b, exp_w_gate, exp_w_up, exp_w_down,
                  shared_w_gate, shared_w_up, shared_w_down, _row_tile_size(xa.shape[0], 1152))
    return xa.reshape(batch, lat_len, d)
```

```python
import functools

import jax
import jax.numpy as jnp
import numpy as np
from jax import lax
from jax.experimental import pallas as pl
from jax.experimental.pallas import tpu as pltpu

F32 = jnp.float32
BF16 = jnp.bfloat16
NEG_INF = float("-inf")

EPS = 1e-6
GRID_W = 64
ROPE_BASE = 10000.0

A_HEADS = 16
A_KV_HEADS = 4
A_GROUP = A_HEADS // A_KV_HEADS
A_HEAD_DIM = 128
WINDOW = 128
A_Q = A_HEADS * A_HEAD_DIM
A_KV = A_KV_HEADS * A_HEAD_DIM

B_HEADS = 8
B_QK_DIM = 128
B_V_DIM = 256
B_CONV_W = 5
GATE_CAP = 15.0
B_QK = B_HEADS * B_QK_DIM
B_V = B_HEADS * B_V_DIM
AB_MAIN = A_Q + 2 * A_KV + 2 * B_QK + 2 * B_V
AB_GATES = 4 * B_HEADS

C_HEAD_DIM = 128

N_EXPERTS = 64
TOP_K = 8
N_GROUPS = 8
TOPK_GROUPS = 4
GROUP_SIZE = N_EXPERTS // N_GROUPS
ROUTED_SCALE = 2.5

VMEM_LIMIT_BYTES = 60 * 1024 * 1024
LANE_BLOCK = 128
ROW_TILE = 256
COL_TILE = 512
MLSTM_CHUNK = 256
HGRN_BLOCK = 256
HGRN_CHUNK = 32
HGRN_HEADS = 4
MOE_TILE = 256
DMA_PRIORITIES = 2
PUSH_TOKENS = 128
COMBINE_TOKENS = 64

NT_DIMS = (((1,), (1,)), ((), ()))
TN_DIMS = (((0,), (0,)), ((), ()))


def _cp(*sem):
    return pltpu.CompilerParams(dimension_semantics=sem, vmem_limit_bytes=VMEM_LIMIT_BYTES)


def _sigmoid(x):
    return jax.nn.sigmoid(x)


def _silu(x):
    return x * jax.nn.sigmoid(x)


def _rms(x, g):
    return x * lax.rsqrt(jnp.mean(x * x, axis=-1, keepdims=True) + EPS) * g


def _row_tile_size(rows, limit):
    best = 16
    for t in range(16, limit + 1, 16):
        if rows % t == 0:
            best = t
    return best


def _ada_body(a_ref, w_ref, b_ref, o_ref):
    o_ref[...] = jnp.dot(a_ref[...], w_ref[...].astype(BF16), preferred_element_type=F32) + b_ref[...]


def _ada(a, ada_w, ada_b, layer):
    depth, d, d6 = ada_w.shape
    rows = a.shape[0]
    return pl.pallas_call(
        _ada_body,
        grid=(d6 // COL_TILE,),
        in_specs=[pl.BlockSpec((rows, d), lambda j: (0, 0)),
                  pl.BlockSpec((None, d, COL_TILE), lambda j: (layer, 0, j)),
                  pl.BlockSpec((None, 1, COL_TILE), lambda j: (layer, 0, j))],
        out_specs=pl.BlockSpec((rows, COL_TILE), lambda j: (0, j)),
        out_shape=jax.ShapeDtypeStruct((rows, d6), F32),
        compiler_params=_cp("arbitrary"),
        name=f"ada{layer}",
    )(a, ada_w, ada_b.reshape(depth, 1, d6))


class _SplitRows:
    def __init__(self, ctx, lat, ctx_map, lat_map, is_ctx):
        self.arrays = (ctx, lat)
        self.maps = (ctx_map, lat_map)
        self.is_ctx = is_ctx
        self.rows = ctx.shape[0] + lat.shape[0]

    def specs(self, block, col, tile_arg):
        return [pl.BlockSpec(block, lambda *g, m=m: (m(g[tile_arg]), col(*g))) for m in self.maps]


def _pick_rows(is_ctx, tile, ctx_ref, lat_ref):
    return jnp.where(is_ctx(tile), ctx_ref[...], lat_ref[...])


def _modulate_body(*refs, shift_row, scale_row, is_ctx):
    *x_refs, g_ref, mod_ref, o_ref = refs
    x = x_refs[0][...] if is_ctx is None else _pick_rows(is_ctx, pl.program_id(0), *x_refs)
    y = _rms(x, g_ref[...])
    o_ref[...] = (y * (1.0 + mod_ref[scale_row:scale_row + 1, :])
                  + mod_ref[shift_row:shift_row + 1, :]).astype(o_ref.dtype)


def _modulate(x, g, mod, sel, *, shift_row, scale_row, name):
    split = isinstance(x, _SplitRows)
    xs = x.arrays if split else (x,)
    rows, d = (x.rows if split else x.shape[0]), xs[0].shape[1]
    x_specs = (x.specs((ROW_TILE, d), lambda i: 0, 0) if split
               else [pl.BlockSpec((ROW_TILE, d), lambda i: (i, 0))])
    return pl.pallas_call(
        functools.partial(_modulate_body, shift_row=shift_row, scale_row=scale_row,
                          is_ctx=x.is_ctx if split else None),
        grid=(rows // ROW_TILE,),
        in_specs=x_specs + [pl.BlockSpec((1, d), lambda i: (0, 0)),
                            pl.BlockSpec((None, 6, d), lambda i: (sel(i), 0, 0))],
        out_specs=pl.BlockSpec((ROW_TILE, d), lambda i: (i, 0)),
        out_shape=jax.ShapeDtypeStruct((rows, d), BF16),
        compiler_params=_cp("arbitrary"),
        name=name,
    )(*xs, g, mod)


def _linear_body(*refs, n_a, gate_row, has_extra, w_t, is_ctx):
    def mm(a, wb):
        if w_t:
            return lax.dot_general(a, wb, NT_DIMS, preferred_element_type=F32)
        return jnp.dot(a, wb, preferred_element_type=F32)

    a_refs = refs[:n_a]
    w_refs = refs[n_a:2 * n_a]
    p = 2 * n_a
    x_refs = mod_ref = e_ref = None
    if gate_row is not None:
        n_x = 1 if is_ctx is None else 2
        x_refs, mod_ref = refs[p:p + n_x], refs[p + n_x]
        p += n_x + 1
        if has_extra:
            e_ref = refs[p]
            p += 1
    o_ref = refs[p]
    wb_refs = refs[p + 1:]

    @pl.when(pl.program_id(1) == 0)
    def _():
        for w_ref, wb in zip(w_refs, wb_refs):
            wb[...] = w_ref[...].astype(BF16)

    acc = mm(a_refs[0][...], wb_refs[0][...])
    for a_ref, wb in zip(a_refs[1:], wb_refs[1:]):
        acc = acc + mm(a_ref[...], wb[...])
    if gate_row is not None:
        if e_ref is not None:
            acc = acc + e_ref[...]
        x = x_refs[0][...] if is_ctx is None else _pick_rows(is_ctx, pl.program_id(1), *x_refs)
        acc = x + mod_ref[gate_row:gate_row + 1, :] * acc
    o_ref[...] = acc.astype(o_ref.dtype)


def _linear(a_list, w, layer, *, n_cols, tm, out_dtype, name, res=None, tn=COL_TILE, row_map=None, out_rows=None,
            w_t=False):
    n_a = len(a_list)
    rows = a_list[0].shape[0]
    if row_map is None:
        row_map = lambda i: i
    else:
        rows = out_rows
    k_each = w.shape[2 if w_t else 1] // n_a
    tn = min(tn, n_cols)
    while n_cols % tn:
        tn //= 2
    in_specs = [pl.BlockSpec((tm, k_each), lambda j, i: (row_map(i), 0)) for _ in a_list]
    for idx in range(n_a):
        if w_t:
            in_specs.append(pl.BlockSpec((None, tn, k_each), lambda j, i, idx=idx: (layer, j, idx)))
        else:
            in_specs.append(pl.BlockSpec((None, k_each, tn), lambda j, i, idx=idx: (layer, idx, j)))
    args = list(a_list) + [w] * n_a
    gate_row, has_extra, is_ctx = None, False, None
    if res is not None:
        x, mod, gate_row, extra, sel = res
        if isinstance(x, _SplitRows):
            is_ctx = x.is_ctx
            in_specs += x.specs((tm, tn), lambda j, i: j, 1)
            args += list(x.arrays)
        else:
            in_specs.append(pl.BlockSpec((tm, tn), lambda j, i: (row_map(i), j)))
            args.append(x)
        in_specs.append(pl.BlockSpec((None, 6, tn), lambda j, i: (sel(i), 0, j)))
        args.append(mod)
        if extra is not None:
            has_extra = True
            in_specs.append(pl.BlockSpec((tm, tn), lambda j, i: (i, j)))
            args.append(extra)
    return pl.pallas_call(
        functools.partial(_linear_body, n_a=n_a, gate_row=gate_row, has_extra=has_extra, w_t=w_t, is_ctx=is_ctx),
        grid=(n_cols // tn, rows // tm),
        in_specs=in_specs,
        out_specs=pl.BlockSpec((tm, tn), lambda j, i: (i, j)),
        out_shape=jax.ShapeDtypeStruct((rows, n_cols), out_dtype),
        scratch_shapes=[pltpu.VMEM((tn, k_each) if w_t else (k_each, tn), BF16) for _ in range(n_a)],
        compiler_params=_cp("arbitrary", "arbitrary"),
        name=name,
    )(*args)


def _glu_body(a_ref, wg_ref, wu_ref, o_ref, wgb, wub):
    @pl.when(pl.program_id(1) == 0)
    def _():
        wgb[...] = wg_ref[...].astype(BF16)
        wub[...] = wu_ref[...].astype(BF16)

    a = a_ref[...]
    g = jnp.dot(a, wgb[...], preferred_element_type=F32)
    u = jnp.dot(a, wub[...], preferred_element_type=F32)
    o_ref[...] = (_silu(g) * u).astype(o_ref.dtype)


def _glu(a, wg, wu, layer, *, tm, name):
    rows, k = a.shape
    n_cols = wg.shape[2]
    tn = min(COL_TILE // 2, n_cols)
    wspec = pl.BlockSpec((None, k, tn), lambda j, i: (layer, 0, j))
    return pl.pallas_call(
        _glu_body,
        grid=(n_cols // tn, rows // tm),
        in_specs=[pl.BlockSpec((tm, k), lambda j, i: (i, 0)), wspec, wspec],
        out_specs=pl.BlockSpec((tm, tn), lambda j, i: (i, j)),
        out_shape=jax.ShapeDtypeStruct((rows, n_cols), BF16),
        scratch_shapes=[pltpu.VMEM((k, tn), BF16), pltpu.VMEM((k, tn), BF16)],
        compiler_params=_cp("arbitrary", "arbitrary"),
        name=name,
    )(a, wg, wu)


def _attn_prep_body(p_ref, cos_ref, sin_ref, qg_ref, kg_ref, q_ref, k_ref, v_ref):
    cos = cos_ref[...]
    sin = sin_ref[...]

    def norm_rope(xh, g):
        y = _rms(xh, g)
        return y * cos + pltpu.roll(y, A_HEAD_DIM // 2, axis=1) * sin

    for hd in range(A_HEADS):
        sl = slice(hd * A_HEAD_DIM, (hd + 1) * A_HEAD_DIM)
        q_ref[:, sl] = norm_rope(p_ref[:, sl], qg_ref[...]).astype(BF16)
    for hd in range(A_KV_HEADS):
        sl = slice(hd * A_HEAD_DIM, (hd + 1) * A_HEAD_DIM)
        k_ref[:, sl] = norm_rope(p_ref[:, A_Q + hd * A_HEAD_DIM:A_Q + (hd + 1) * A_HEAD_DIM], kg_ref[...]).astype(BF16)
    v_ref[...] = p_ref[:, A_Q + A_KV:A_Q + 2 * A_KV].astype(BF16)


def _attn_prep(proj, cos_t, sin_t, qg, kg, tiles_per_sample):
    rows = proj.shape[0]
    width = A_Q + 2 * A_KV
    return pl.pallas_call(
        _attn_prep_body,
        grid=(rows // ROW_TILE,),
        in_specs=[pl.BlockSpec((ROW_TILE, width), lambda i: (i, 0)),
                  pl.BlockSpec((ROW_TILE, A_HEAD_DIM), lambda i: (i % tiles_per_sample, 0)),
                  pl.BlockSpec((ROW_TILE, A_HEAD_DIM), lambda i: (i % tiles_per_sample, 0)),
                  pl.BlockSpec((1, A_HEAD_DIM), lambda i: (0, 0)),
                  pl.BlockSpec((1, A_HEAD_DIM), lambda i: (0, 0))],
        out_specs=[pl.BlockSpec((ROW_TILE, A_Q), lambda i: (i, 0)),
                   pl.BlockSpec((ROW_TILE, A_KV), lambda i: (i, 0)),
                   pl.BlockSpec((ROW_TILE, A_KV), lambda i: (i, 0))],
        out_shape=[jax.ShapeDtypeStruct((rows, A_Q), BF16),
                   jax.ShapeDtypeStruct((rows, A_KV), BF16),
                   jax.ShapeDtypeStruct((rows, A_KV), BF16)],
        compiler_params=_cp("arbitrary"),
        name="attn_prep",
    )(proj, cos_t, sin_t, qg, kg)


def _attn_body(sink_ref, q_ref, kp_ref, ko_ref, kn_ref, kc_ref, vp_ref, vo_ref, vn_ref, vc_ref, o_ref,
               *, ctx_blocks, lat_len):
    h = pl.program_id(1)
    j = pl.program_id(2)
    n = j - ctx_blocks
    nw = 3 * WINDOW
    m_ctx = kc_ref.shape[0]
    kw = jnp.concatenate([kp_ref[...], ko_ref[...], kn_ref[...], kc_ref[...]], axis=0)
    vw = jnp.concatenate([vp_ref[...], vo_ref[...], vn_ref[...], vc_ref[...]], axis=0)
    rows = A_GROUP * WINDOW
    qi = lax.broadcasted_iota(jnp.int32, (rows, nw + m_ctx), 0) % WINDOW
    wi = lax.broadcasted_iota(jnp.int32, (rows, nw + m_ctx), 1)
    key_pos = (n - 1) * WINDOW + wi
    in_win = (jnp.abs(wi - WINDOW - qi) <= WINDOW) & (key_pos >= 0) & (key_pos < lat_len) & (n >= 0)
    valid = (wi >= nw) | in_win
    bias = jnp.where(valid, 0.0, NEG_INF).astype(F32)
    scale = A_HEAD_DIM ** -0.5
    heads = [slice(g * A_HEAD_DIM, (g + 1) * A_HEAD_DIM) for g in range(A_GROUP)]
    q = jnp.concatenate([q_ref[:, sl] for sl in heads], axis=0)
    sink = jnp.concatenate([jnp.full((WINDOW, 1), sink_ref[h * A_GROUP + g], F32) for g in range(A_GROUP)], axis=0)
    s = lax.dot_general(q, kw, NT_DIMS, preferred_element_type=F32) * scale + bias
    m = jnp.maximum(jnp.max(s, axis=-1, keepdims=True), sink)
    p = jnp.exp(s - m)
    denom = jnp.sum(p, axis=-1, keepdims=True) + jnp.exp(sink - m)
    o = jnp.dot(p.astype(BF16), vw, preferred_element_type=F32) / denom
    for g, sl in enumerate(heads):
        o_ref[:, sl] = o[g * WINDOW:(g + 1) * WINDOW, :].astype(o_ref.dtype)


def _attention(qn, kn, vb, sink, *, batch, seq_all, ctx_len):
    blocks = seq_all // WINDOW
    ctx_blocks = ctx_len // WINDOW
    last = blocks - 1
    qw = A_GROUP * A_HEAD_DIM

    def kv_spec(shift):
        def imap(b, h, j):
            return (b * blocks + jnp.clip(j + shift, ctx_blocks, last), h)
        return pl.BlockSpec((WINDOW, A_HEAD_DIM), imap)

    ctx_spec = pl.BlockSpec((ctx_len, A_HEAD_DIM), lambda b, h, j: (b * (seq_all // ctx_len), h))
    return pl.pallas_call(
        functools.partial(_attn_body, ctx_blocks=ctx_blocks, lat_len=seq_all - ctx_len),
        grid=(batch, A_KV_HEADS, blocks),
        in_specs=[pl.BlockSpec(memory_space=pltpu.SMEM),
                  pl.BlockSpec((WINDOW, qw), lambda b, h, j: (b * blocks + j, h)),
                  kv_spec(-1), kv_spec(0), kv_spec(1), ctx_spec,
                  kv_spec(-1), kv_spec(0), kv_spec(1), ctx_spec],
        out_specs=pl.BlockSpec((WINDOW, qw), lambda b, h, j: (b * blocks + j, h)),
        out_shape=jax.ShapeDtypeStruct((batch * seq_all, A_Q), BF16),
        compiler_params=_cp("arbitrary", "arbitrary", "arbitrary"),
        name="window_attention",
    )(sink, qn, kn, kn, kn, kn, vb, vb, vb, vb)


def _conv_body(x_ref, w_ref, o_ref, *, ctx_len, k_first_block):
    x = x_ref[...]
    t_len = x.shape[0]
    t = lax.broadcasted_iota(jnp.int32, x.shape, 0)
    half = B_CONV_W // 2
    acc = x * w_ref[half:half + 1, :]
    for d in range(-half, half + 1):
        if d == 0:
            continue
        xs = pltpu.roll(x, (-d) % t_len, axis=0)
        u = t + d
        ok = ((t < ctx_len) & (u >= 0) & (u < ctx_len)) | ((t >= ctx_len) & (u >= ctx_len) & (u < t_len))
        acc = acc + jnp.where(ok, xs, 0.0) * w_ref[half + d:half + d + 1, :]
    k_scale = jnp.where(pl.program_id(1) >= k_first_block, B_QK_DIM ** -0.5, 1.0).astype(F32)
    o_ref[...] = (_silu(acc) * k_scale).astype(o_ref.dtype)


def _mlstm_conv(proj, conv_w, *, batch, seq_all, ctx_len):
    first = (A_Q + 2 * A_KV) // LANE_BLOCK
    nblk = 2 * B_QK // LANE_BLOCK
    return pl.pallas_call(
        functools.partial(_conv_body, ctx_len=ctx_len, k_first_block=B_QK // LANE_BLOCK),
        grid=(batch, nblk),
        in_specs=[pl.BlockSpec((seq_all, LANE_BLOCK), lambda b, c: (b, first + c)),
                  pl.BlockSpec((B_CONV_W, LANE_BLOCK), lambda b, c: (0, c))],
        out_specs=pl.BlockSpec((seq_all, LANE_BLOCK), lambda b, c: (b, c)),
        out_shape=jax.ShapeDtypeStruct((batch * seq_all, 2 * B_QK), BF16),
        compiler_params=_cp("arbitrary", "arbitrary"),
        name="mlstm_conv",
    )(proj, conv_w)


def _gates_body(raw_ref, b_ref, o_ref):
    g = raw_ref[...] + b_ref[...]
    g = GATE_CAP * jnp.tanh(g / GATE_CAP)
    lane = lax.broadcasted_iota(jnp.int32, g.shape, 1)
    is_forget = (lane // B_HEADS) % 2 == 1
    log_sig = jnp.minimum(g, 0.0) - jnp.log(1.0 + jnp.exp(-jnp.abs(g)))
    o_ref[...] = jnp.where(is_forget, log_sig, g)


def _mlstm_gates(raw, gate_b):
    rows, width = raw.shape
    tr = _row_tile_size(rows, 2048)
    return pl.pallas_call(
        _gates_body,
        grid=(rows // tr,),
        in_specs=[pl.BlockSpec((tr, width), lambda i: (i, 0)),
                  pl.BlockSpec((1, width), lambda i: (0, 0))],
        out_specs=pl.BlockSpec((tr, width), lambda i: (i, 0)),
        out_shape=jax.ShapeDtypeStruct((rows, width), F32),
        compiler_params=_cp("arbitrary"),
        name="mlstm_gates",
    )(raw, gate_b)


def _mlstm_direction(q_ref, k_ref, v_ref, g_ref, o_ref, st_ref, n_ref, m_ref, d, rev):
    c_len = q_ref.shape[0]
    q = q_ref[...]
    k = k_ref[...]
    v = v_ref[...].astype(BF16)
    ig = g_ref[2 * d:2 * d + 1, :]
    lf = g_ref[2 * d + 1:2 * d + 2, :]
    r = lax.broadcasted_iota(jnp.int32, (c_len, c_len), 0)
    c = lax.broadcasted_iota(jnp.int32, (c_len, c_len), 1)
    eye = r == c
    tri = (c >= r) if rev else (c <= r)
    tri_t = (r >= c) if rev else (r <= c)
    lf_col = jnp.sum(jnp.where(eye, lf, 0.0), axis=1, keepdims=True)
    ig_col = jnp.sum(jnp.where(eye, ig, 0.0), axis=1, keepdims=True)
    b_col = jnp.sum(jnp.where(tri, lf, 0.0), axis=1, keepdims=True)
    b_row = jnp.sum(jnp.where(tri_t, lf_col, 0.0), axis=0, keepdims=True)
    m_prev = m_ref[d]
    d_st = b_col + m_prev
    d_in = jnp.where(tri, b_col - b_row + ig, NEG_INF)
    m_t = jnp.maximum(d_st, jnp.max(d_in, axis=1, keepdims=True))
    s = jnp.exp(d_in - m_t) * lax.dot_general(q, k, NT_DIMS, preferred_element_type=F32)
    w_st = jnp.exp(d_st - m_t)
    st = st_ref[d]
    num = (jnp.dot(s.astype(BF16), v, preferred_element_type=F32)
           + w_st * jnp.dot(q, st.astype(BF16), preferred_element_type=F32))
    qn = jnp.sum(q.astype(F32) * n_ref[d], axis=1, keepdims=True)
    den = jnp.sum(s, axis=1, keepdims=True) + w_st * qn
    o_ref[...] = num / jnp.maximum(jnp.abs(den), jnp.exp(-m_t))
    b_end = jnp.sum(lf, axis=1, keepdims=True)
    d_up = b_end - b_col + ig_col
    m_new = jnp.maximum(b_end + m_prev, jnp.max(d_up, axis=0, keepdims=True))
    w_up = jnp.exp(d_up - m_new)
    a = jnp.exp(b_end + m_prev - m_new)
    kw = k.astype(F32) * w_up
    st_ref[d] = a * st + lax.dot_general(kw.astype(BF16), v, TN_DIMS, preferred_element_type=F32)
    n_ref[d] = a * n_ref[d] + jnp.sum(kw, axis=0, keepdims=True)
    m_ref[d] = m_new


def _mlstm_body(qf, kf, vf, gf, qb, kb, vb, gb, of, ob, st_ref, n_ref, m_ref):
    @pl.when(pl.program_id(2) == 0)
    def _():
        st_ref[...] = jnp.zeros_like(st_ref)
        n_ref[...] = jnp.zeros_like(n_ref)
        m_ref[...] = jnp.zeros_like(m_ref)

    _mlstm_direction(qf, kf, vf, gf, of, st_ref, n_ref, m_ref, 0, False)
    _mlstm_direction(qb, kb, vb, gb, ob, st_ref, n_ref, m_ref, 1, True)


def _scan_orders(n_chunks, ctx_chunks):
    fwd = lambda j: j
    bwd = lambda j: jnp.where(j < ctx_chunks, ctx_chunks - 1 - j, n_chunks - 1 - (j - ctx_chunks))
    return fwd, bwd


def _mlstm_scan(qk, proj, gates_t, *, batch, seq_all, ctx_len):
    ch = MLSTM_CHUNK
    n_chunks = seq_all // ch
    fwd, bwd = _scan_orders(n_chunks, ctx_len // ch)
    v_first = (A_Q + 2 * A_KV + 2 * B_QK) // B_V_DIM

    def specs(order):
        return [pl.BlockSpec((ch, B_QK_DIM), lambda b, h, j: (b * n_chunks + order(j), h)),
                pl.BlockSpec((ch, B_QK_DIM), lambda b, h, j: (b * n_chunks + order(j), B_HEADS + h)),
                pl.BlockSpec((ch, B_V_DIM), lambda b, h, j: (b * n_chunks + order(j), v_first + h)),
                pl.BlockSpec((None, None, 4, ch), lambda b, h, j: (b, h, 0, order(j)))]

    def out_spec(order):
        return pl.BlockSpec((ch, B_V_DIM), lambda b, h, j: (b * n_chunks + order(j), h))

    out = jax.ShapeDtypeStruct((batch * seq_all, B_V), F32)
    return pl.pallas_call(
        _mlstm_body,
        grid=(batch, B_HEADS, n_chunks),
        in_specs=specs(fwd) + specs(bwd),
        out_specs=[out_spec(fwd), out_spec(bwd)],
        out_shape=[out, out],
        scratch_shapes=[pltpu.VMEM((2, B_QK_DIM, B_V_DIM), F32),
                        pltpu.VMEM((2, 1, B_QK_DIM), F32),
                        pltpu.VMEM((2, 1, 1), F32)],
        compiler_params=_cp("arbitrary", "arbitrary", "arbitrary"),
        name="mlstm_scan",
    )(qk, qk, proj, gates_t, qk, qk, proj, gates_t)


def _headnorm_gate_body(a_ref, b_ref, gate_ref, g_ref, o_ref, *, head_dim, gate_fn):
    for hd in range(a_ref.shape[1] // head_dim):
        sl = slice(hd * head_dim, (hd + 1) * head_dim)
        y = _rms(a_ref[:, sl] + b_ref[:, sl], g_ref[:, sl])
        o_ref[:, sl] = (y * gate_fn(gate_ref[:, sl])).astype(o_ref.dtype)


def _headnorm_gate(a, b, proj, gate_col0, norm_g, *, head_dim, gate_fn, name):
    rows, width = a.shape
    tc = min(1024, width)
    return pl.pallas_call(
        functools.partial(_headnorm_gate_body, head_dim=head_dim, gate_fn=gate_fn),
        grid=(rows // ROW_TILE, width // tc),
        in_specs=[pl.BlockSpec((ROW_TILE, tc), lambda i, c: (i, c)),
                  pl.BlockSpec((ROW_TILE, tc), lambda i, c: (i, c)),
                  pl.BlockSpec((ROW_TILE, tc), lambda i, c: (i, gate_col0 // tc + c)),
                  pl.BlockSpec((1, tc), lambda i, c: (0, c))],
        out_specs=pl.BlockSpec((ROW_TILE, tc), lambda i, c: (i, c)),
        out_shape=jax.ShapeDtypeStruct((rows, width), BF16),
        compiler_params=_cp("arbitrary", "arbitrary"),
        name=name,
    )(a, b, proj, norm_g)


def _hgrn_direction(q_ref, i_ref, f_ref, lb_ref, o_ref, st_ref, d, rev):
    rows = q_ref.shape[0]
    ch = HGRN_CHUNK
    lb = lb_ref[d:d + 1, :]
    q = _silu(q_ref[...])
    f = lb + (1.0 - lb) * _sigmoid(f_ref[...])
    kk = 1.0 - f
    r = lax.broadcasted_iota(jnp.int32, (rows, rows), 0)
    c = lax.broadcasted_iota(jnp.int32, (rows, rows), 1)
    visible = (r // ch == c // ch) & ((c >= r) if rev else (c <= r))
    a_cum = jnp.log(f)
    pos = lax.broadcasted_iota(jnp.int32, a_cum.shape, 0) % ch
    sh = 1
    while sh < ch:
        if rev:
            a_cum = a_cum + jnp.where(pos < ch - sh, pltpu.roll(a_cum, rows - sh, axis=0), 0.0)
        else:
            a_cum = a_cum + jnp.where(pos >= sh, pltpu.roll(a_cum, sh, axis=0), 0.0)
        sh *= 2
    n_chunks = rows // ch
    end_row = 0 if rev else ch - 1
    mid_row = ch // 2
    width = a_cum.shape[1]

    def per_chunk_row(row):
        return jnp.concatenate([jnp.broadcast_to(a_cum[ci * ch + row:ci * ch + row + 1, :], (ch, width))
                                for ci in range(n_chunks)], axis=0)

    a_mid = per_chunk_row(mid_row)
    a_end = per_chunk_row(end_row)
    v = i_ref[...].astype(BF16)
    q_in = (q * jnp.exp(a_cum)).astype(BF16)
    qh = (q * jnp.exp(a_cum - a_mid)).astype(BF16)
    kh = (kk * jnp.exp(a_mid - a_cum)).astype(BF16)
    ke = (kk * jnp.exp(a_end - a_cum)).astype(BF16)
    order = range(n_chunks - 1, -1, -1) if rev else range(n_chunks)
    for hd in range(HGRN_HEADS):
        cs = slice(hd * C_HEAD_DIM, (hd + 1) * C_HEAD_DIM)
        att = jnp.where(visible, lax.dot_general(qh[:, cs], kh[:, cs], NT_DIMS, preferred_element_type=F32), 0.0)
        o_local = jnp.dot(att.astype(BF16), v[:, cs], preferred_element_type=F32)
        updates = {ci: lax.dot_general(v[ci * ch:(ci + 1) * ch, cs], ke[ci * ch:(ci + 1) * ch, cs], TN_DIMS,
                                       preferred_element_type=F32) for ci in order}
        st = st_ref[d, hd]
        for ci in order:
            rs = slice(ci * ch, (ci + 1) * ch)
            o_ref[rs, cs] = o_local[rs, :] + lax.dot_general(q_in[rs, cs], st.astype(BF16), NT_DIMS,
                                                              preferred_element_type=F32)
            st = jnp.exp(a_cum[ci * ch + end_row:ci * ch + end_row + 1, cs]) * st + updates[ci]
        st_ref[d, hd] = st


def _hgrn_body(qf, vf, ff, qb, vb, fb, lb_ref, of, ob, st_ref):
    @pl.when(pl.program_id(2) == 0)
    def _():
        st_ref[...] = jnp.zeros_like(st_ref)

    _hgrn_direction(qf, vf, ff, lb_ref, of, st_ref, 0, False)
    _hgrn_direction(qb, vb, fb, lb_ref, ob, st_ref, 1, True)


def _hgrn_scan(proj, lb, *, batch, seq_all, ctx_len, d_model):
    blk = HGRN_BLOCK
    n_blocks = seq_all // blk
    fwd, bwd = _scan_orders(n_blocks, ctx_len // blk)
    wcols = HGRN_HEADS * C_HEAD_DIM
    per = d_model // wcols

    def spec(order, part):
        return pl.BlockSpec((blk, wcols), lambda b, h, j: (b * n_blocks + order(j), part * per + h))

    def out_spec(order):
        return pl.BlockSpec((blk, wcols), lambda b, h, j: (b * n_blocks + order(j), h))

    out = jax.ShapeDtypeStruct((batch * seq_all, d_model), F32)
    return pl.pallas_call(
        _hgrn_body,
        grid=(batch, per, n_blocks),
        in_specs=[spec(fwd, 0), spec(fwd, 1), spec(fwd, 3), spec(bwd, 0), spec(bwd, 1), spec(bwd, 4),
                  pl.BlockSpec((2, wcols), lambda b, h, j: (0, h))],
        out_specs=[out_spec(fwd), out_spec(bwd)],
        out_shape=[out, out],
        scratch_shapes=[pltpu.VMEM((2, HGRN_HEADS, C_HEAD_DIM, C_HEAD_DIM), F32)],
        compiler_params=_cp("arbitrary", "arbitrary", "arbitrary"),
        name="hgrn_scan",
    )(proj, proj, proj, proj, proj, proj, lb)


def _pack_halves(v):
    half = v.shape[1] // 2
    bits = lax.bitcast_convert_type(v.astype(BF16).astype(F32), jnp.uint32)
    return (bits[:, :half] >> 16) | (bits[:, half:] & jnp.uint32(0xFFFF0000))


def _unpack_halves(p):
    lo = lax.bitcast_convert_type(p << 16, F32)
    hi = lax.bitcast_convert_type(p & jnp.uint32(0xFFFF0000), F32)
    return lo, hi


def _ffn_router_body(x_ref, g_ref, mod_ref, rwt_ref, rb_ref, hb_ref, hp_ref, idx_ref, wgt_ref, rank_ref, cnt_ref):
    @pl.when(pl.program_id(0) == 0)
    def _():
        cnt_ref[...] = jnp.zeros_like(cnt_ref)

    y = _rms(x_ref[...], g_ref[...])
    h2 = y * (1.0 + mod_ref[4:5, :]) + mod_ref[3:4, :]
    hb = h2.astype(BF16)
    hb_ref[...] = hb
    hp_ref[...] = _pack_halves(hb)
    tm = h2.shape[0]
    logits = lax.dot_general(rwt_ref[...], h2, NT_DIMS, precision=lax.Precision.HIGHEST,
                             preferred_element_type=F32)
    s = _sigmoid(logits)
    sel = s + rb_ref[...]
    iota_g = lax.broadcasted_iota(jnp.int32, (GROUP_SIZE, tm), 0)
    group_scores = []
    for g in range(N_GROUPS):
        xg = sel[g * GROUP_SIZE:(g + 1) * GROUP_SIZE, :]
        m1 = jnp.max(xg, axis=0, keepdims=True)
        i1 = jnp.min(jnp.where(xg == m1, iota_g, GROUP_SIZE), axis=0, keepdims=True)
        m2 = jnp.max(jnp.where(iota_g == i1, NEG_INF, xg), axis=0, keepdims=True)
        group_scores.append(m1 + m2)
    gsc = jnp.concatenate(group_scores, axis=0)
    iota_n = lax.broadcasted_iota(jnp.int32, (N_GROUPS, tm), 0)
    gmask = jnp.zeros((N_GROUPS, tm), F32)
    for _ in range(TOPK_GROUPS):
        m = jnp.max(gsc, axis=0, keepdims=True)
        i = jnp.min(jnp.where(gsc == m, iota_n, N_GROUPS), axis=0, keepdims=True)
        hit = iota_n == i
        gmask = jnp.where(hit, 1.0, gmask)
        gsc = jnp.where(hit, NEG_INF, gsc)
    emask = jnp.concatenate([jnp.broadcast_to(gmask[g:g + 1, :], (GROUP_SIZE, tm)) for g in range(N_GROUPS)], axis=0)
    cur = jnp.where(emask > 0.0, sel, NEG_INF)
    iota_e = lax.broadcasted_iota(jnp.int32, (N_EXPERTS, tm), 0)
    idx_rows, w_rows, hits = [], [], []
    for _ in range(TOP_K):
        m = jnp.max(cur, axis=0, keepdims=True)
        i = jnp.min(jnp.where(cur == m, iota_e, N_EXPERTS), axis=0, keepdims=True)
        hit = iota_e == i
        idx_rows.append(i)
        hits.append(hit)
        w_rows.append(jnp.sum(jnp.where(hit, s, 0.0), axis=0, keepdims=True))
        cur = jnp.where(hit, NEG_INF, cur)
    w = jnp.concatenate(w_rows, axis=0)
    idx_ref[...] = jnp.concatenate(idx_rows, axis=0)
    wgt_ref[...] = ROUTED_SCALE * w / jnp.sum(w, axis=0, keepdims=True)
    chosen = jnp.zeros((N_EXPERTS, tm), F32)
    for hit in hits:
        chosen = jnp.where(hit, 1.0, chosen)
    earlier = (lax.broadcasted_iota(jnp.int32, (tm, tm), 0) < lax.broadcasted_iota(jnp.int32, (tm, tm), 1))
    before = cnt_ref[...] + jnp.dot(chosen.astype(BF16), earlier.astype(BF16), preferred_element_type=F32)
    rank_ref[...] = jnp.concatenate(
        [jnp.sum(jnp.where(hit, before, 0.0), axis=0, keepdims=True) for hit in hits], axis=0).astype(jnp.int32)
    cnt_ref[...] = cnt_ref[...] + jnp.sum(chosen, axis=1, keepdims=True)


def _ffn_router(x, g, mod, sel, router_wt, router_b, layer):
    rows, d = x.shape
    return pl.pallas_call(
        _ffn_router_body,
        grid=(rows // ROW_TILE,),
        in_specs=[pl.BlockSpec((ROW_TILE, d), lambda i: (i, 0)),
                  pl.BlockSpec((1, d), lambda i: (0, 0)),
                  pl.BlockSpec((None, 6, d), lambda i: (sel(i), 0, 0)),
                  pl.BlockSpec((None, N_EXPERTS, d), lambda i: (layer, 0, 0)),
                  pl.BlockSpec((None, N_EXPERTS, 1), lambda i: (layer, 0, 0))],
        out_specs=[pl.BlockSpec((ROW_TILE, d), lambda i: (i, 0)),
                   pl.BlockSpec((ROW_TILE, d // 2), lambda i: (i, 0)),
                   pl.BlockSpec((TOP_K, ROW_TILE), lambda i: (0, i)),
                   pl.BlockSpec((TOP_K, ROW_TILE), lambda i: (0, i)),
                   pl.BlockSpec((TOP_K, ROW_TILE), lambda i: (0, i)),
                   pl.BlockSpec((N_EXPERTS, 1), lambda i: (0, 0))],
        out_shape=[jax.ShapeDtypeStruct((rows, d), BF16),
                   jax.ShapeDtypeStruct((rows, d // 2), jnp.uint32),
                   jax.ShapeDtypeStruct((TOP_K, rows), jnp.int32),
                   jax.ShapeDtypeStruct((TOP_K, rows), F32),
                   jax.ShapeDtypeStruct((TOP_K, rows), jnp.int32),
                   jax.ShapeDtypeStruct((N_EXPERTS, 1), F32)],
        compiler_params=_cp("arbitrary"),
        name=f"ffn_router{layer}",
    )(x, g, mod, router_wt, router_b)


def _dispatch(idx_t, rank_t, counts):
    tm = MOE_TILE
    slots = idx_t.shape[0] * idx_t.shape[1]
    n_tiles = slots // tm
    counts = counts.reshape(-1).astype(jnp.int32)
    ends = jnp.cumsum(counts)
    starts = ends - counts
    is_e = idx_t[:, :, None] == jnp.arange(N_EXPERTS, dtype=jnp.int32)
    pos = (jnp.sum(jnp.where(is_e, starts, 0), axis=-1) + rank_t).T.reshape(-1)
    cuts = jnp.sort(jnp.concatenate([jnp.arange(n_tiles, dtype=jnp.int32) * tm, ends[:-1]]))
    nxt = jnp.concatenate([cuts[1:], jnp.full((1,), slots, jnp.int32)])
    v_tile = jnp.minimum(cuts // tm, n_tiles - 1)
    v_exp = jnp.minimum(jnp.sum((ends[None, :] <= cuts[:, None]).astype(jnp.int32), axis=1), N_EXPERTS - 1)
    v_lo = cuts - v_tile * tm
    v_hi = nxt - v_tile * tm
    n_vis = cuts.shape[0]
    v_first = (v_exp != jnp.concatenate([jnp.full((1,), -1, jnp.int32), v_exp[:-1]])).astype(jnp.int32)
    order = jnp.arange(n_vis, dtype=jnp.int32)
    next_first = lax.cummin(jnp.where(v_first == 1, order, n_vis)[::-1])[::-1]
    after = jnp.concatenate([next_first[1:], jnp.full((1,), n_vis, jnp.int32)])
    v_next = jnp.where(after < n_vis, v_exp[jnp.minimum(after, n_vis - 1)], -1)
    return (pos.astype(jnp.int32), v_tile.astype(jnp.int32), v_exp.astype(jnp.int32), v_lo, v_hi, v_first,
            v_next.astype(jnp.int32))


def _push_body(pos_ref, h_ref, xs_hbm, sem):
    i = pl.program_id(0)
    tt = h_ref.shape[0]
    base = i * tt * TOP_K

    def issue(r, carry):
        src = h_ref.at[pl.ds(r, 1)]
        for k in range(TOP_K):
            pltpu.make_async_copy(src, xs_hbm.at[pl.ds(pos_ref[base + r * TOP_K + k], 1)], sem).start(
                priority=k % DMA_PRIORITIES)
        return carry

    lax.fori_loop(0, tt, issue, 0)
    for _ in range(TOP_K):
        pltpu.make_async_copy(h_ref, xs_hbm.at[pl.ds(0, tt)], sem).wait()


def _push(h2f, pos, layer):
    rows, d = h2f.shape
    tt = PUSH_TOKENS
    grid_spec = pltpu.PrefetchScalarGridSpec(
        num_scalar_prefetch=1,
        grid=(rows // tt,),
        in_specs=[pl.BlockSpec((tt, d), lambda i, ps: (i, 0))],
        out_specs=pl.BlockSpec(memory_space=pl.ANY),
        scratch_shapes=[pltpu.SemaphoreType.DMA(())],
    )
    return pl.pallas_call(
        _push_body,
        grid_spec=grid_spec,
        out_shape=jax.ShapeDtypeStruct((rows * TOP_K, d), h2f.dtype),
        compiler_params=_cp("arbitrary"),
        name=f"push{layer}",
    )(pos, h2f)


def _expert_body(vt_ref, ve_ref, lo_ref, hi_ref, first_ref, next_ref, x_ref, wg_hbm, wu_hbm, wd_hbm, y_ref,
                 wgf, wuf, wdf, wgb, wub, wdb, sem, *, layer):
    v = pl.program_id(0)
    lo = lo_ref[v]
    hi = hi_ref[v]

    def weight_copies(e):
        return (pltpu.make_async_copy(wg_hbm.at[layer, e], wgf, sem),
                pltpu.make_async_copy(wu_hbm.at[layer, e], wuf, sem),
                pltpu.make_async_copy(wd_hbm.at[layer, e], wdf, sem))

    @pl.when(v == 0)
    def _():
        for cp in weight_copies(ve_ref[0]):
            cp.start()

    @pl.when(first_ref[v] == 1)
    def _():
        for cp in weight_copies(ve_ref[v]):
            cp.wait()
        wgb[...] = wgf[...].astype(BF16)
        wub[...] = wuf[...].astype(BF16)
        wdb[...] = wdf[...].astype(BF16)

        @pl.when(next_ref[v] >= 0)
        def _():
            for cp in weight_copies(next_ref[v]):
                cp.start()

    @pl.when(hi > lo)
    def _():
        x_lo, x_hi = _unpack_halves(x_ref[...])
        x_lo = x_lo.astype(BF16)
        x_hi = x_hi.astype(BF16)
        half = x_lo.shape[1]
        g = (jnp.dot(x_lo, wgb[:half, :], preferred_element_type=F32)
             + jnp.dot(x_hi, wgb[half:, :], preferred_element_type=F32))
        u = (jnp.dot(x_lo, wub[:half, :], preferred_element_type=F32)
             + jnp.dot(x_hi, wub[half:, :], preferred_element_type=F32))
        hid = (_silu(g) * u).astype(BF16)
        y = _pack_halves(jnp.dot(hid, wdb[...], preferred_element_type=F32))
        r = lax.broadcasted_iota(jnp.int32, (y.shape[0], 1), 0)
        mine = (r >= lo) & (r < hi)

        @pl.when(lo == 0)
        def _():
            y_ref[...] = jnp.where(mine, y, jnp.uint32(0))

        @pl.when(lo > 0)
        def _():
            y_ref[...] = jnp.where(mine, y, y_ref[...])


def _experts(xs, visits, wg, wu, wd, layer):
    slots, dp = xs.shape
    d = wg.shape[2]
    de = wg.shape[3]
    tm = MOE_TILE
    hbm = pl.BlockSpec(memory_space=pl.ANY)
    grid_spec = pltpu.PrefetchScalarGridSpec(
        num_scalar_prefetch=len(visits),
        grid=(visits[0].shape[0],),
        in_specs=[pl.BlockSpec((tm, dp), lambda v, vt, *_: (vt[v], 0)), hbm, hbm, hbm],
        out_specs=pl.BlockSpec((tm, dp), lambda v, vt, *_: (vt[v], 0)),
        scratch_shapes=[pltpu.VMEM((d, de), F32), pltpu.VMEM((d, de), F32), pltpu.VMEM((de, d), F32),
                        pltpu.VMEM((d, de), BF16), pltpu.VMEM((d, de), BF16), pltpu.VMEM((de, d), BF16),
                        pltpu.SemaphoreType.DMA(())],
    )
    return pl.pallas_call(
        functools.partial(_expert_body, layer=layer),
        grid_spec=grid_spec,
        out_shape=jax.ShapeDtypeStruct((slots, dp), jnp.uint32),
        compiler_params=_cp("arbitrary"),
        name=f"experts{layer}",
    )(*visits, xs, wg, wu, wd)


def _combine_body(pos_ref, w_ref, y_hbm, o_ref, buf, sem):
    i = pl.program_id(0)
    tt = o_ref.shape[0]
    n = TOP_K * tt
    slot = i % 2
    half = buf.shape[3]
    group = 8

    def issue_rows(step, into, r0, rows):
        for j in range(rows):
            for k in range(TOP_K):
                pltpu.make_async_copy(y_hbm.at[pl.ds(pos_ref[step * n + (r0 + j) * TOP_K + k], 1)],
                                      buf.at[into, k, pl.ds(r0 + j, 1)], sem.at[into]).start(
                    priority=k % DMA_PRIORITIES)

    @pl.when(i == 0)
    def _():
        def first(g, carry):
            issue_rows(0, 0, g * group, group)
            return carry

        lax.fori_loop(0, tt // group, first, 0)

    for k in range(TOP_K):
        pltpu.make_async_copy(y_hbm.at[pl.ds(0, tt)], buf.at[slot, k], sem.at[slot]).wait()
    last = pl.num_programs(0) - 1
    nxt = jnp.minimum(i + 1, last)

    def body(g, carry):
        r0 = pl.multiple_of(g * group, group)
        issue_rows(nxt, 1 - slot, r0, group)
        w = w_ref[pl.ds(r0, group), :]
        acc_lo, acc_hi = _unpack_halves(buf[slot, 0, pl.ds(r0, group), :])
        acc_lo = w[:, 0:1] * acc_lo
        acc_hi = w[:, 0:1] * acc_hi
        for k in range(1, TOP_K):
            y_lo, y_hi = _unpack_halves(buf[slot, k, pl.ds(r0, group), :])
            acc_lo = acc_lo + w[:, k:k + 1] * y_lo
            acc_hi = acc_hi + w[:, k:k + 1] * y_hi
        o_ref[pl.ds(r0, group), :half] = acc_lo
        o_ref[pl.ds(r0, group), half:] = acc_hi
        return carry

    lax.fori_loop(0, tt // group, body, 0)

    @pl.when(i == last)
    def _():
        for k in range(TOP_K):
            pltpu.make_async_copy(y_hbm.at[pl.ds(0, tt)], buf.at[1 - slot, k], sem.at[1 - slot]).wait()


def _combine(ys, pos_flat, wgt, rows, layer):
    dp = ys.shape[1]
    d = 2 * dp
    tt = COMBINE_TOKENS
    grid_spec = pltpu.PrefetchScalarGridSpec(
        num_scalar_prefetch=1,
        grid=(rows // tt,),
        in_specs=[pl.BlockSpec((tt, TOP_K), lambda i, ps: (i, 0)),
                  pl.BlockSpec(memory_space=pl.ANY)],
        out_specs=pl.BlockSpec((tt, d), lambda i, ps: (i, 0)),
        scratch_shapes=[pltpu.VMEM((2, TOP_K, tt, dp), jnp.uint32), pltpu.SemaphoreType.DMA((2,))],
    )
    return pl.pallas_call(
        _combine_body,
        grid_spec=grid_spec,
        out_shape=jax.ShapeDtypeStruct((rows, d), F32),
        compiler_params=_cp("arbitrary"),
        name=f"combine{layer}",
    )(pos_flat, wgt, ys)


def _moe(x1, mod, sel, layer, norm_g, router_w, router_b, exp_wg, exp_wu, exp_wd, s_gate, s_up, s_down, tm_big):
    rows, d = x1.shape
    router_wt = jnp.swapaxes(router_w, 1, 2)
    h2b, h2p, idx_t, wgt_t, rank_t, counts = _ffn_router(x1, norm_g[layer][None], mod, sel, router_wt,
                                                         router_b[:, :, None], layer)
    pos, *visits = _dispatch(idx_t, rank_t, counts)
    xs = _push(h2p, pos, layer)
    ys = _experts(xs, visits, exp_wg, exp_wu, exp_wd, layer)
    routed = _combine(ys, pos, wgt_t.T, rows, layer)
    hs = _glu(h2b, s_gate, s_up, layer, tm=tm_big, name=f"shared_glu{layer}")
    return _linear([hs], s_down, layer, n_cols=d, tm=ROW_TILE, out_dtype=F32, name=f"shared_down{layer}",
                   res=(x1, mod, 5, routed, sel), tn=4 * COL_TILE)


def _rope_tables(ctx_len, lat_len):
    rows = lat_len // GRID_W
    pos_r = np.repeat(np.arange(rows), GRID_W).astype(np.float32)
    pos_c = np.tile(np.arange(GRID_W), rows).astype(np.float32)
    n = A_HEAD_DIM // 4
    inv = (np.float32(ROPE_BASE) ** (-np.arange(n, dtype=np.float32) / np.float32(n))).astype(np.float32)
    ang = np.concatenate([pos_r[:, None] * inv, pos_c[:, None] * inv], axis=-1).astype(np.float32)
    cos, sin = np.cos(ang).astype(np.float32), np.sin(ang).astype(np.float32)
    cos_t = np.concatenate([np.ones((ctx_len, A_HEAD_DIM), np.float32), np.concatenate([cos, cos], axis=-1)], axis=0)
    sin_t = np.concatenate([np.zeros((ctx_len, A_HEAD_DIM), np.float32), np.concatenate([-sin, sin], axis=-1)], axis=0)
    return jnp.asarray(cos_t), jnp.asarray(sin_t)


def kernel(x, c, ctx, c_ctx, ada_w, ada_b, norm_mix_g, norm_ffn_g, ab_w_in, ab_w_out, a_q_norm_g, a_k_norm_g,
           a_sink, b_conv_w, b_gate_b, b_norm_g, c_w_in, c_w_out, c_lb_logits, c_norm_g, router_w, router_b,
           exp_w_gate, exp_w_up, exp_w_down, shared_w_gate, shared_w_up, shared_w_down):
    batch, lat_len, d = x.shape
    ctx_len = ctx.shape[1]
    depth = ada_w.shape[0]
    seq_all = ctx_len + lat_len
    rows = batch * seq_all
    assert ctx_len % ROW_TILE == 0 and lat_len % ROW_TILE == 0 and ctx_len % MLSTM_CHUNK == 0
    assert lat_len % GRID_W == 0 and d % (HGRN_HEADS * C_HEAD_DIM) == 0
    tiles_per_sample = seq_all // ROW_TILE
    ctx_tiles = ctx_len // ROW_TILE
    tm_big = _row_tile_size(rows, 1152)
    tm_in = _row_tile_size(rows, 576)

    def sel(i):
        return jnp.where(i % tiles_per_sample < ctx_tiles, batch, i // tiles_per_sample)

    cos_t, sin_t = _rope_tables(ctx_len, lat_len)
    lb_soft = jax.nn.softmax(c_lb_logits.astype(F32), axis=0)
    lower_bounds = jnp.cumsum(lb_soft, axis=0) - lb_soft[0:1]
    cond = jnp.concatenate([jax.nn.silu(c), jax.nn.silu(c_ctx)[None]], axis=0)
    cond = jnp.pad(cond, ((0, 16 - cond.shape[0]), (0, 0))).astype(BF16)

    lat_tiles = lat_len // ROW_TILE

    def lat_map(i):
        return (i // lat_tiles) * tiles_per_sample + ctx_tiles + i % lat_tiles

    if depth > 1:
        xa = _SplitRows(
            ctx.reshape(batch * ctx_len, d), x.reshape(batch * lat_len, d),
            lambda i: (i // tiles_per_sample) * ctx_tiles + jnp.minimum(i % tiles_per_sample, ctx_tiles - 1),
            lambda i: (i // tiles_per_sample) * lat_tiles + jnp.clip(i % tiles_per_sample - ctx_tiles, 0,
                                                                    lat_tiles - 1),
            lambda i: i % tiles_per_sample < ctx_tiles)
    else:
        xa = jnp.concatenate([ctx, x], axis=1).reshape(rows, d)
    for layer in range(depth):
        last = layer == depth - 1
        tail = dict(row_map=lat_map, out_rows=batch * lat_len) if last else {}
        tail_sel = (lambda i: i // lat_tiles) if last else sel
        mod = _ada(cond, ada_w, ada_b, layer)[:batch + 1].reshape(batch + 1, 6, d)
        h = _modulate(xa, norm_mix_g[layer][None], mod, sel, shift_row=0, scale_row=1, name=f"mod_mix{layer}")
        if layer % 2 == 0:
            e = layer // 2
            w_in_t = jnp.swapaxes(ab_w_in, 1, 2)
            proj = _linear([h], w_in_t, e, n_cols=AB_MAIN, tm=tm_in, out_dtype=F32, name=f"ab_in{layer}",
                           tn=2 * COL_TILE, w_t=True)
            w_gates_t = jnp.pad(w_in_t[e, AB_MAIN:, :], ((0, LANE_BLOCK - AB_GATES), (0, 0)))[None]
            graw = _linear([h], w_gates_t, 0, n_cols=LANE_BLOCK, tm=tm_big, out_dtype=F32, name=f"ab_gates{layer}",
                           w_t=True)
            qn, kn, vb = _attn_prep(proj, cos_t, sin_t, a_q_norm_g[e][None], a_k_norm_g[e][None], tiles_per_sample)
            ya = _attention(qn, kn, vb, a_sink[e], batch=batch, seq_all=seq_all, ctx_len=ctx_len)
            qk = _mlstm_conv(proj, b_conv_w[e], batch=batch, seq_all=seq_all, ctx_len=ctx_len)
            gate_b = jnp.pad(b_gate_b[e], (0, LANE_BLOCK - AB_GATES))[None]
            gates = _mlstm_gates(graw, gate_b)[:, :AB_GATES]
            gates_t = gates.reshape(batch, seq_all, 4, B_HEADS).transpose(0, 3, 2, 1)
            hf, hb = _mlstm_scan(qk, proj, gates_t, batch=batch, seq_all=seq_all, ctx_len=ctx_len)
            yb = _headnorm_gate(hf, hb, proj, AB_MAIN - B_V, b_norm_g[e][None], head_dim=B_V_DIM,
                                gate_fn=_sigmoid, name="mlstm_out")
            xa = _linear([ya, yb], ab_w_out, e, n_cols=d, tm=ROW_TILE, out_dtype=F32, name=f"ab_out{layer}",
                         res=(xa, mod, 2, None, tail_sel), tn=2 * COL_TILE, **tail)
        else:
            o = layer // 2
            proj = _linear([h], c_w_in, o, n_cols=5 * d, tm=tm_in, out_dtype=F32, name=f"c_in{layer}",
                           tn=2 * COL_TILE)
            of, ob = _hgrn_scan(proj, lower_bounds[layer], batch=batch, seq_all=seq_all, ctx_len=ctx_len, d_model=d)
            yc = _headnorm_gate(of, ob, proj, 2 * d, c_norm_g[o][None], head_dim=C_HEAD_DIM, gate_fn=_silu,
                                name="hgrn_out")
            xa = _linear([yc], c_w_out, o, n_cols=d, tm=ROW_TILE, out_dtype=F32, name=f"c_out{layer}",
                         res=(xa, mod, 2, None, tail_sel), tn=2 * COL_TILE, **tail)
        xa = _moe(xa, mod, tail_sel, layer, norm_ffn_g, router_w, router_b, exp_w_gate, exp_w_up, exp_w_down,
                  shared_w_gate, shared_w_up, shared_w_down, _row_tile_size(xa.shape[0], 1152))
    return xa.reshape(batch, lat_len, d)
```

```python
import functools

import jax
import jax.numpy as jnp
import numpy as np
from jax import lax
from jax.experimental import pallas as pl
from jax.experimental.pallas import tpu as pltpu

F32 = jnp.float32
BF16 = jnp.bfloat16
NEG_INF = float("-inf")

EPS = 1e-6
GRID_W = 64
ROPE_BASE = 10000.0

A_HEADS = 16
A_KV_HEADS = 4
A_GROUP = A_HEADS // A_KV_HEADS
A_HEAD_DIM = 128
WINDOW = 128
A_Q = A_HEADS * A_HEAD_DIM
A_KV = A_KV_HEADS * A_HEAD_DIM

B_HEADS = 8
B_QK_DIM = 128
B_V_DIM = 256
B_CONV_W = 5
GATE_CAP = 15.0
B_QK = B_HEADS * B_QK_DIM
B_V = B_HEADS * B_V_DIM
AB_MAIN = A_Q + 2 * A_KV + 2 * B_QK + 2 * B_V
AB_GATES = 4 * B_HEADS

C_HEAD_DIM = 128

N_EXPERTS = 64
TOP_K = 8
N_GROUPS = 8
TOPK_GROUPS = 4
GROUP_SIZE = N_EXPERTS // N_GROUPS
ROUTED_SCALE = 2.5

VMEM_LIMIT_BYTES = 60 * 1024 * 1024
LANE_BLOCK = 128
ROW_TILE = 256
COL_TILE = 512
MLSTM_CHUNK = 256
MLSTM_HEADS = 2
HGRN_BLOCK = 256
HGRN_CHUNK = 32
HGRN_HEADS = 8
MOE_TILE = 256
DMA_PRIORITIES = 2
PUSH_TOKENS = 128
COMBINE_TOKENS = 64

NT_DIMS = (((1,), (1,)), ((), ()))
TN_DIMS = (((0,), (0,)), ((), ()))


def _cp(*sem):
    return pltpu.CompilerParams(dimension_semantics=sem, vmem_limit_bytes=VMEM_LIMIT_BYTES)


def _sigmoid(x):
    return jax.nn.sigmoid(x)


def _silu(x):
    return x * jax.nn.sigmoid(x)


def _rms(x, g):
    return x * lax.rsqrt(jnp.mean(x * x, axis=-1, keepdims=True) + EPS) * g


def _row_tile_size(rows, limit):
    best = 16
    for t in range(16, limit + 1, 16):
        if rows % t == 0:
            best = t
    return best


def _ada_body(a_ref, w_ref, b_ref, o_ref):
    o_ref[...] = jnp.dot(a_ref[...], w_ref[...].astype(BF16), preferred_element_type=F32) + b_ref[...]


def _ada(a, ada_w, ada_b, layer):
    depth, d, d6 = ada_w.shape
    rows = a.shape[0]
    return pl.pallas_call(
        _ada_body,
        grid=(d6 // COL_TILE,),
        in_specs=[pl.BlockSpec((rows, d), lambda j: (0, 0)),
                  pl.BlockSpec((None, d, COL_TILE), lambda j: (layer, 0, j)),
                  pl.BlockSpec((None, 1, COL_TILE), lambda j: (layer, 0, j))],
        out_specs=pl.BlockSpec((rows, COL_TILE), lambda j: (0, j)),
        out_shape=jax.ShapeDtypeStruct((rows, d6), F32),
        compiler_params=_cp("arbitrary"),
        name=f"ada{layer}",
    )(a, ada_w, ada_b.reshape(depth, 1, d6))


class _SplitRows:
    def __init__(self, ctx, lat, ctx_map, lat_map, is_ctx):
        self.arrays = (ctx, lat)
        self.maps = (ctx_map, lat_map)
        self.is_ctx = is_ctx
        self.rows = ctx.shape[0] + lat.shape[0]

    def specs(self, block, col, tile_arg):
        return [pl.BlockSpec(block, lambda *g, m=m: (m(g[tile_arg]), col(*g))) for m in self.maps]


def _pick_rows(is_ctx, tile, ctx_ref, lat_ref):
    return jnp.where(is_ctx(tile), ctx_ref[...], lat_ref[...])


def _modulate_body(*refs, shift_row, scale_row, is_ctx):
    *x_refs, g_ref, mod_ref, o_ref = refs
    x = x_refs[0][...] if is_ctx is None else _pick_rows(is_ctx, pl.program_id(0), *x_refs)
    y = _rms(x, g_ref[...])
    o_ref[...] = (y * (1.0 + mod_ref[scale_row:scale_row + 1, :])
                  + mod_ref[shift_row:shift_row + 1, :]).astype(o_ref.dtype)


def _modulate(x, g, mod, sel, *, shift_row, scale_row, name):
    split = isinstance(x, _SplitRows)
    xs = x.arrays if split else (x,)
    rows, d = (x.rows if split else x.shape[0]), xs[0].shape[1]
    x_specs = (x.specs((ROW_TILE, d), lambda i: 0, 0) if split
               else [pl.BlockSpec((ROW_TILE, d), lambda i: (i, 0))])
    return pl.pallas_call(
        functools.partial(_modulate_body, shift_row=shift_row, scale_row=scale_row,
                          is_ctx=x.is_ctx if split else None),
        grid=(rows // ROW_TILE,),
        in_specs=x_specs + [pl.BlockSpec((1, d), lambda i: (0, 0)),
                            pl.BlockSpec((None, 6, d), lambda i: (sel(i), 0, 0))],
        out_specs=pl.BlockSpec((ROW_TILE, d), lambda i: (i, 0)),
        out_shape=jax.ShapeDtypeStruct((rows, d), BF16),
        compiler_params=_cp("arbitrary"),
        name=name,
    )(*xs, g, mod)


def _linear_body(*refs, n_a, gate_row, has_extra, w_t, is_ctx):
    def mm(a, wb):
        if w_t:
            return lax.dot_general(a, wb, NT_DIMS, preferred_element_type=F32)
        return jnp.dot(a, wb, preferred_element_type=F32)

    a_refs = refs[:n_a]
    w_refs = refs[n_a:2 * n_a]
    p = 2 * n_a
    x_refs = mod_ref = e_ref = None
    if gate_row is not None:
        n_x = 1 if is_ctx is None else 2
        x_refs, mod_ref = refs[p:p + n_x], refs[p + n_x]
        p += n_x + 1
        if has_extra:
            e_ref = refs[p]
            p += 1
    o_ref = refs[p]
    wb_refs = refs[p + 1:]

    @pl.when(pl.program_id(1) == 0)
    def _():
        for w_ref, wb in zip(w_refs, wb_refs):
            wb[...] = w_ref[...].astype(BF16)

    acc = mm(a_refs[0][...], wb_refs[0][...])
    for a_ref, wb in zip(a_refs[1:], wb_refs[1:]):
        acc = acc + mm(a_ref[...], wb[...])
    if gate_row is not None:
        if e_ref is not None:
            acc = acc + e_ref[...]
        x = x_refs[0][...] if is_ctx is None else _pick_rows(is_ctx, pl.program_id(1), *x_refs)
        acc = x + mod_ref[gate_row:gate_row + 1, :] * acc
    o_ref[...] = acc.astype(o_ref.dtype)


def _linear(a_list, w, layer, *, n_cols, tm, out_dtype, name, res=None, tn=COL_TILE, row_map=None, out_rows=None,
            w_t=False):
    n_a = len(a_list)
    rows = a_list[0].shape[0]
    if row_map is None:
        row_map = lambda i: i
    else:
        rows = out_rows
    k_each = w.shape[2 if w_t else 1] // n_a
    tn = min(tn, n_cols)
    while n_cols % tn:
        tn //= 2
    in_specs = [pl.BlockSpec((tm, k_each), lambda j, i: (row_map(i), 0)) for _ in a_list]
    for idx in range(n_a):
        if w_t:
            in_specs.append(pl.BlockSpec((None, tn, k_each), lambda j, i, idx=idx: (layer, j, idx)))
        else:
            in_specs.append(pl.BlockSpec((None, k_each, tn), lambda j, i, idx=idx: (layer, idx, j)))
    args = list(a_list) + [w] * n_a
    gate_row, has_extra, is_ctx = None, False, None
    if res is not None:
        x, mod, gate_row, extra, sel = res
        if isinstance(x, _SplitRows):
            is_ctx = x.is_ctx
            in_specs += x.specs((tm, tn), lambda j, i: j, 1)
            args += list(x.arrays)
        else:
            in_specs.append(pl.BlockSpec((tm, tn), lambda j, i: (row_map(i), j)))
            args.append(x)
        in_specs.append(pl.BlockSpec((None, 6, tn), lambda j, i: (sel(i), 0, j)))
        args.append(mod)
        if extra is not None:
            has_extra = True
            in_specs.append(pl.BlockSpec((tm, tn), lambda j, i: (i, j)))
            args.append(extra)
    return pl.pallas_call(
        functools.partial(_linear_body, n_a=n_a, gate_row=gate_row, has_extra=has_extra, w_t=w_t, is_ctx=is_ctx),
        grid=(n_cols // tn, rows // tm),
        in_specs=in_specs,
        out_specs=pl.BlockSpec((tm, tn), lambda j, i: (i, j)),
        out_shape=jax.ShapeDtypeStruct((rows, n_cols), out_dtype),
        scratch_shapes=[pltpu.VMEM((tn, k_each) if w_t else (k_each, tn), BF16) for _ in range(n_a)],
        compiler_params=_cp("arbitrary", "arbitrary"),
        name=name,
    )(*args)


def _glu_body(a_ref, wg_ref, wu_ref, o_ref, wgb, wub):
    @pl.when(pl.program_id(1) == 0)
    def _():
        wgb[...] = wg_ref[...].astype(BF16)
        wub[...] = wu_ref[...].astype(BF16)

    a = a_ref[...]
    g = jnp.dot(a, wgb[...], preferred_element_type=F32)
    u = jnp.dot(a, wub[...], preferred_element_type=F32)
    o_ref[...] = (_silu(g) * u).astype(o_ref.dtype)


def _glu(a, wg, wu, layer, *, tm, name):
    rows, k = a.shape
    n_cols = wg.shape[2]
    tn = min(COL_TILE // 2, n_cols)
    wspec = pl.BlockSpec((None, k, tn), lambda j, i: (layer, 0, j))
    return pl.pallas_call(
        _glu_body,
        grid=(n_cols // tn, rows // tm),
        in_specs=[pl.BlockSpec((tm, k), lambda j, i: (i, 0)), wspec, wspec],
        out_specs=pl.BlockSpec((tm, tn), lambda j, i: (i, j)),
        out_shape=jax.ShapeDtypeStruct((rows, n_cols), BF16),
        scratch_shapes=[pltpu.VMEM((k, tn), BF16), pltpu.VMEM((k, tn), BF16)],
        compiler_params=_cp("arbitrary", "arbitrary"),
        name=name,
    )(a, wg, wu)


def _attn_prep_body(p_ref, cos_ref, sin_ref, qg_ref, kg_ref, q_ref, k_ref, v_ref):
    cos = cos_ref[...]
    sin = sin_ref[...]

    def norm_rope(xh, g):
        y = _rms(xh, g)
        return y * cos + pltpu.roll(y, A_HEAD_DIM // 2, axis=1) * sin

    for hd in range(A_HEADS):
        sl = slice(hd * A_HEAD_DIM, (hd + 1) * A_HEAD_DIM)
        q_ref[:, sl] = norm_rope(p_ref[:, sl], qg_ref[...]).astype(BF16)
    for hd in range(A_KV_HEADS):
        sl = slice(hd * A_HEAD_DIM, (hd + 1) * A_HEAD_DIM)
        k_ref[:, sl] = norm_rope(p_ref[:, A_Q + hd * A_HEAD_DIM:A_Q + (hd + 1) * A_HEAD_DIM], kg_ref[...]).astype(BF16)
    v_ref[...] = p_ref[:, A_Q + A_KV:A_Q + 2 * A_KV].astype(BF16)


def _attn_prep(proj, cos_t, sin_t, qg, kg, tiles_per_sample):
    rows = proj.shape[0]
    width = A_Q + 2 * A_KV
    return pl.pallas_call(
        _attn_prep_body,
        grid=(rows // ROW_TILE,),
        in_specs=[pl.BlockSpec((ROW_TILE, width), lambda i: (i, 0)),
                  pl.BlockSpec((ROW_TILE, A_HEAD_DIM), lambda i: (i % tiles_per_sample, 0)),
                  pl.BlockSpec((ROW_TILE, A_HEAD_DIM), lambda i: (i % tiles_per_sample, 0)),
                  pl.BlockSpec((1, A_HEAD_DIM), lambda i: (0, 0)),
                  pl.BlockSpec((1, A_HEAD_DIM), lambda i: (0, 0))],
        out_specs=[pl.BlockSpec((ROW_TILE, A_Q), lambda i: (i, 0)),
                   pl.BlockSpec((ROW_TILE, A_KV), lambda i: (i, 0)),
                   pl.BlockSpec((ROW_TILE, A_KV), lambda i: (i, 0))],
        out_shape=[jax.ShapeDtypeStruct((rows, A_Q), BF16),
                   jax.ShapeDtypeStruct((rows, A_KV), BF16),
                   jax.ShapeDtypeStruct((rows, A_KV), BF16)],
        compiler_params=_cp("arbitrary"),
        name="attn_prep",
    )(proj, cos_t, sin_t, qg, kg)


def _attn_body(sink_ref, q_ref, kp_ref, ko_ref, kn_ref, kc_ref, vp_ref, vo_ref, vn_ref, vc_ref, o_ref,
               *, ctx_blocks, lat_len):
    h = pl.program_id(1)
    j = pl.program_id(2)
    n = j - ctx_blocks
    nw = 3 * WINDOW
    m_ctx = kc_ref.shape[0]
    kw = jnp.concatenate([kp_ref[...], ko_ref[...], kn_ref[...], kc_ref[...]], axis=0)
    vw = jnp.concatenate([vp_ref[...], vo_ref[...], vn_ref[...], vc_ref[...]], axis=0)
    qi = lax.broadcasted_iota(jnp.int32, (WINDOW, nw + m_ctx), 0)
    wi = lax.broadcasted_iota(jnp.int32, (WINDOW, nw + m_ctx), 1)
    key_pos = (n - 1) * WINDOW + wi
    in_win = (jnp.abs(wi - WINDOW - qi) <= WINDOW) & (key_pos >= 0) & (key_pos < lat_len) & (n >= 0)
    valid = (wi >= nw) | in_win
    bias = jnp.where(valid, 0.0, NEG_INF).astype(F32)
    bias = jnp.concatenate([bias] * A_GROUP, axis=0)
    scale = A_HEAD_DIM ** -0.5
    heads = [slice(g * A_HEAD_DIM, (g + 1) * A_HEAD_DIM) for g in range(A_GROUP)]
    q = jnp.concatenate([q_ref[:, sl] for sl in heads], axis=0)
    sink = jnp.concatenate([jnp.full((WINDOW, 1), sink_ref[h * A_GROUP + g], F32) for g in range(A_GROUP)], axis=0)
    s = lax.dot_general(q, kw, NT_DIMS, preferred_element_type=F32) * scale + bias
    m = jnp.maximum(jnp.max(s, axis=-1, keepdims=True), sink)
    p = jnp.exp(s - m)
    denom = jnp.sum(p, axis=-1, keepdims=True) + jnp.exp(sink - m)
    o = jnp.dot(p.astype(BF16), vw, preferred_element_type=F32) / denom
    for g, sl in enumerate(heads):
        o_ref[:, sl] = o[g * WINDOW:(g + 1) * WINDOW, :].astype(o_ref.dtype)


def _attention(qn, kn, vb, sink, *, batch, seq_all, ctx_len):
    blocks = seq_all // WINDOW
    ctx_blocks = ctx_len // WINDOW
    last = blocks - 1
    qw = A_GROUP * A_HEAD_DIM

    def kv_spec(shift):
        def imap(b, h, j):
            return (b * blocks + jnp.clip(j + shift, ctx_blocks, last), h)
        return pl.BlockSpec((WINDOW, A_HEAD_DIM), imap)

    ctx_spec = pl.BlockSpec((ctx_len, A_HEAD_DIM), lambda b, h, j: (b * (seq_all // ctx_len), h))
    return pl.pallas_call(
        functools.partial(_attn_body, ctx_blocks=ctx_blocks, lat_len=seq_all - ctx_len),
        grid=(batch, A_KV_HEADS, blocks),
        in_specs=[pl.BlockSpec(memory_space=pltpu.SMEM),
                  pl.BlockSpec((WINDOW, qw), lambda b, h, j: (b * blocks + j, h)),
                  kv_spec(-1), kv_spec(0), kv_spec(1), ctx_spec,
                  kv_spec(-1), kv_spec(0), kv_spec(1), ctx_spec],
        out_specs=pl.BlockSpec((WINDOW, qw), lambda b, h, j: (b * blocks + j, h)),
        out_shape=jax.ShapeDtypeStruct((batch * seq_all, A_Q), BF16),
        compiler_params=_cp("arbitrary", "arbitrary", "arbitrary"),
        name="window_attention",
    )(sink, qn, kn, kn, kn, kn, vb, vb, vb, vb)


def _conv_body(x_ref, w_ref, o_ref, *, ctx_len, k_first_block):
    x = x_ref[...]
    t_len = x.shape[0]
    t = lax.broadcasted_iota(jnp.int32, x.shape, 0)
    half = B_CONV_W // 2
    acc = x * w_ref[half:half + 1, :]
    for d in range(-half, half + 1):
        if d == 0:
            continue
        xs = pltpu.roll(x, (-d) % t_len, axis=0)
        u = t + d
        ok = ((t < ctx_len) & (u >= 0) & (u < ctx_len)) | ((t >= ctx_len) & (u >= ctx_len) & (u < t_len))
        acc = acc + jnp.where(ok, xs, 0.0) * w_ref[half + d:half + d + 1, :]
    k_scale = jnp.where(pl.program_id(1) >= k_first_block, B_QK_DIM ** -0.5, 1.0).astype(F32)
    o_ref[...] = (_silu(acc) * k_scale).astype(o_ref.dtype)


def _mlstm_conv(proj, conv_w, *, batch, seq_all, ctx_len):
    first = (A_Q + 2 * A_KV) // LANE_BLOCK
    nblk = 2 * B_QK // LANE_BLOCK
    return pl.pallas_call(
        functools.partial(_conv_body, ctx_len=ctx_len, k_first_block=B_QK // LANE_BLOCK),
        grid=(batch, nblk),
        in_specs=[pl.BlockSpec((seq_all, LANE_BLOCK), lambda b, c: (b, first + c)),
                  pl.BlockSpec((B_CONV_W, LANE_BLOCK), lambda b, c: (0, c))],
        out_specs=pl.BlockSpec((seq_all, LANE_BLOCK), lambda b, c: (b, c)),
        out_shape=jax.ShapeDtypeStruct((batch * seq_all, 2 * B_QK), BF16),
        compiler_params=_cp("arbitrary", "arbitrary"),
        name="mlstm_conv",
    )(proj, conv_w)


def _gates_body(raw_ref, b_ref, o_ref):
    g = raw_ref[...] + b_ref[...]
    g = GATE_CAP * jnp.tanh(g / GATE_CAP)
    lane = lax.broadcasted_iota(jnp.int32, g.shape, 1)
    is_forget = (lane // B_HEADS) % 2 == 1
    log_sig = jnp.minimum(g, 0.0) - jnp.log(1.0 + jnp.exp(-jnp.abs(g)))
    o_ref[...] = jnp.where(is_forget, log_sig, g)


def _mlstm_gates(raw, gate_b):
    rows, width = raw.shape
    tr = _row_tile_size(rows, 2048)
    return pl.pallas_call(
        _gates_body,
        grid=(rows // tr,),
        in_specs=[pl.BlockSpec((tr, width), lambda i: (i, 0)),
                  pl.BlockSpec((1, width), lambda i: (0, 0))],
        out_specs=pl.BlockSpec((tr, width), lambda i: (i, 0)),
        out_shape=jax.ShapeDtypeStruct((rows, width), F32),
        compiler_params=_cp("arbitrary"),
        name="mlstm_gates",
    )(raw, gate_b)


def _mlstm_direction(q_ref, k_ref, v_ref, g_ref, o_ref, st_ref, n_ref, m_ref, d, rev, hd):
    c_len = q_ref.shape[0]
    qk_cols = slice(hd * B_QK_DIM, (hd + 1) * B_QK_DIM)
    v_cols = slice(hd * B_V_DIM, (hd + 1) * B_V_DIM)
    q = q_ref[:, qk_cols]
    k = k_ref[:, qk_cols]
    v = v_ref[:, v_cols].astype(BF16)
    ig = g_ref[hd, 2 * d:2 * d + 1, :]
    lf = g_ref[hd, 2 * d + 1:2 * d + 2, :]
    d = d * MLSTM_HEADS + hd
    r = lax.broadcasted_iota(jnp.int32, (c_len, c_len), 0)
    c = lax.broadcasted_iota(jnp.int32, (c_len, c_len), 1)
    eye = r == c
    tri = (c >= r) if rev else (c <= r)
    tri_t = (r >= c) if rev else (r <= c)
    lf_col = jnp.sum(jnp.where(eye, lf, 0.0), axis=1, keepdims=True)
    ig_col = jnp.sum(jnp.where(eye, ig, 0.0), axis=1, keepdims=True)
    b_col = jnp.sum(jnp.where(tri, lf, 0.0), axis=1, keepdims=True)
    b_row = jnp.sum(jnp.where(tri_t, lf_col, 0.0), axis=0, keepdims=True)
    m_prev = m_ref[d]
    d_st = b_col + m_prev
    d_in = jnp.where(tri, b_col - b_row + ig, NEG_INF)
    m_t = jnp.maximum(d_st, jnp.max(d_in, axis=1, keepdims=True))
    s = jnp.exp(d_in - m_t) * lax.dot_general(q, k, NT_DIMS, preferred_element_type=F32)
    w_st = jnp.exp(d_st - m_t)
    st = st_ref[d]
    num = (jnp.dot(s.astype(BF16), v, preferred_element_type=F32)
           + w_st * jnp.dot(q, st.astype(BF16), preferred_element_type=F32))
    qn = jnp.sum(q.astype(F32) * n_ref[d], axis=1, keepdims=True)
    den = jnp.sum(s, axis=1, keepdims=True) + w_st * qn
    o_ref[:, v_cols] = num / jnp.maximum(jnp.abs(den), jnp.exp(-m_t))
    b_end = jnp.sum(lf, axis=1, keepdims=True)
    d_up = b_end - b_col + ig_col
    m_new = jnp.maximum(b_end + m_prev, jnp.max(d_up, axis=0, keepdims=True))
    w_up = jnp.exp(d_up - m_new)
    a = jnp.exp(b_end + m_prev - m_new)
    kw = k.astype(F32) * w_up
    st_ref[d] = a * st + lax.dot_general(kw.astype(BF16), v, TN_DIMS, preferred_element_type=F32)
    n_ref[d] = a * n_ref[d] + jnp.sum(kw, axis=0, keepdims=True)
    m_ref[d] = m_new


def _mlstm_body(qf, kf, vf, gf, qb, kb, vb, gb, of, ob, st_ref, n_ref, m_ref):
    @pl.when(pl.program_id(2) == 0)
    def _():
        st_ref[...] = jnp.zeros_like(st_ref)
        n_ref[...] = jnp.zeros_like(n_ref)
        m_ref[...] = jnp.zeros_like(m_ref)

    for hd in range(MLSTM_HEADS):
        _mlstm_direction(qf, kf, vf, gf, of, st_ref, n_ref, m_ref, 0, False, hd)
        _mlstm_direction(qb, kb, vb, gb, ob, st_ref, n_ref, m_ref, 1, True, hd)


def _scan_orders(n_chunks, ctx_chunks):
    fwd = lambda j: j
    bwd = lambda j: jnp.where(j < ctx_chunks, ctx_chunks - 1 - j, n_chunks - 1 - (j - ctx_chunks))
    return fwd, bwd


def _mlstm_scan(qk, proj, gates_t, *, batch, seq_all, ctx_len):
    ch = MLSTM_CHUNK
    n_chunks = seq_all // ch
    fwd, bwd = _scan_orders(n_chunks, ctx_len // ch)
    v_first = (A_Q + 2 * A_KV + 2 * B_QK) // B_V_DIM

    nh = MLSTM_HEADS
    groups = B_HEADS // nh

    def specs(order):
        return [pl.BlockSpec((ch, nh * B_QK_DIM), lambda b, h, j: (b * n_chunks + order(j), h)),
                pl.BlockSpec((ch, nh * B_QK_DIM), lambda b, h, j: (b * n_chunks + order(j), groups + h)),
                pl.BlockSpec((ch, nh * B_V_DIM), lambda b, h, j: (b * n_chunks + order(j), v_first // nh + h)),
                pl.BlockSpec((None, nh, 4, ch), lambda b, h, j: (b, h, 0, order(j)))]

    def out_spec(order):
        return pl.BlockSpec((ch, nh * B_V_DIM), lambda b, h, j: (b * n_chunks + order(j), h))

    out = jax.ShapeDtypeStruct((batch * seq_all, B_V), F32)
    return pl.pallas_call(
        _mlstm_body,
        grid=(batch, groups, n_chunks),
        in_specs=specs(fwd) + specs(bwd),
        out_specs=[out_spec(fwd), out_spec(bwd)],
        out_shape=[out, out],
        scratch_shapes=[pltpu.VMEM((2 * nh, B_QK_DIM, B_V_DIM), F32),
                        pltpu.VMEM((2 * nh, 1, B_QK_DIM), F32),
                        pltpu.VMEM((2 * nh, 1, 1), F32)],
        compiler_params=_cp("arbitrary", "arbitrary", "arbitrary"),
        name="mlstm_scan",
    )(qk, qk, proj, gates_t, qk, qk, proj, gates_t)


def _headnorm_gate_body(a_ref, b_ref, gate_ref, g_ref, o_ref, *, head_dim, gate_fn):
    for hd in range(a_ref.shape[1] // head_dim):
        sl = slice(hd * head_dim, (hd + 1) * head_dim)
        y = _rms(a_ref[:, sl] + b_ref[:, sl], g_ref[:, sl])
        o_ref[:, sl] = (y * gate_fn(gate_ref[:, sl])).astype(o_ref.dtype)


def _headnorm_gate(a, b, proj, gate_col0, norm_g, *, head_dim, gate_fn, name):
    rows, width = a.shape
    tc = min(1024, width)
    return pl.pallas_call(
        functools.partial(_headnorm_gate_body, head_dim=head_dim, gate_fn=gate_fn),
        grid=(rows // ROW_TILE, width // tc),
        in_specs=[pl.BlockSpec((ROW_TILE, tc), lambda i, c: (i, c)),
                  pl.BlockSpec((ROW_TILE, tc), lambda i, c: (i, c)),
                  pl.BlockSpec((ROW_TILE, tc), lambda i, c: (i, gate_col0 // tc + c)),
                  pl.BlockSpec((1, tc), lambda i, c: (0, c))],
        out_specs=pl.BlockSpec((ROW_TILE, tc), lambda i, c: (i, c)),
        out_shape=jax.ShapeDtypeStruct((rows, width), BF16),
        compiler_params=_cp("arbitrary", "arbitrary"),
        name=name,
    )(a, b, proj, norm_g)


def _hgrn_direction(q_ref, i_ref, f_ref, lb_ref, o_ref, st_ref, d, rev):
    rows = q_ref.shape[0]
    ch = HGRN_CHUNK
    lb = lb_ref[d:d + 1, :]
    q = _silu(q_ref[...])
    f = lb + (1.0 - lb) * _sigmoid(f_ref[...])
    kk = 1.0 - f
    r = lax.broadcasted_iota(jnp.int32, (rows, rows), 0)
    c = lax.broadcasted_iota(jnp.int32, (rows, rows), 1)
    visible = (r // ch == c // ch) & ((c >= r) if rev else (c <= r))
    a_cum = jnp.log(f)
    pos = lax.broadcasted_iota(jnp.int32, a_cum.shape, 0) % ch
    sh = 1
    while sh < ch:
        if rev:
            a_cum = a_cum + jnp.where(pos < ch - sh, pltpu.roll(a_cum, rows - sh, axis=0), 0.0)
        else:
            a_cum = a_cum + jnp.where(pos >= sh, pltpu.roll(a_cum, sh, axis=0), 0.0)
        sh *= 2
    n_chunks = rows // ch
    end_row = 0 if rev else ch - 1
    mid_row = ch // 2
    width = a_cum.shape[1]

    def per_chunk_row(row):
        return jnp.concatenate([jnp.broadcast_to(a_cum[ci * ch + row:ci * ch + row + 1, :], (ch, width))
                                for ci in range(n_chunks)], axis=0)

    a_mid = per_chunk_row(mid_row)
    a_end = per_chunk_row(end_row)
    v = i_ref[...].astype(BF16)
    q_in = (q * jnp.exp(a_cum)).astype(BF16)
    qh = (q * jnp.exp(a_cum - a_mid)).astype(BF16)
    kh = (kk * jnp.exp(a_mid - a_cum)).astype(BF16)
    ke = (kk * jnp.exp(a_end - a_cum)).astype(BF16)
    order = range(n_chunks - 1, -1, -1) if rev else range(n_chunks)
    for hd in range(width // C_HEAD_DIM):
        cs = slice(hd * C_HEAD_DIM, (hd + 1) * C_HEAD_DIM)
        att = jnp.where(visible, lax.dot_general(qh[:, cs], kh[:, cs], NT_DIMS, preferred_element_type=F32), 0.0)
        o_local = jnp.dot(att.astype(BF16), v[:, cs], preferred_element_type=F32)
        updates = {ci: lax.dot_general(v[ci * ch:(ci + 1) * ch, cs], ke[ci * ch:(ci + 1) * ch, cs], TN_DIMS,
                                       preferred_element_type=F32) for ci in order}
        st = st_ref[d, hd]
        for ci in order:
            rs = slice(ci * ch, (ci + 1) * ch)
            o_ref[rs, cs] = o_local[rs, :] + lax.dot_general(q_in[rs, cs], st.astype(BF16), NT_DIMS,
                                                              preferred_element_type=F32)
            st = jnp.exp(a_cum[ci * ch + end_row:ci * ch + end_row + 1, cs]) * st + updates[ci]
        st_ref[d, hd] = st


def _hgrn_body(qf, vf, ff, qb, vb, fb, lb_ref, of, ob, st_ref):
    @pl.when(pl.program_id(2) == 0)
    def _():
        st_ref[...] = jnp.zeros_like(st_ref)

    _hgrn_direction(qf, vf, ff, lb_ref, of, st_ref, 0, False)
    _hgrn_direction(qb, vb, fb, lb_ref, ob, st_ref, 1, True)


def _hgrn_scan(proj, lb, *, batch, seq_all, ctx_len, d_model):
    blk = HGRN_BLOCK
    n_blocks = seq_all // blk
    fwd, bwd = _scan_orders(n_blocks, ctx_len // blk)
    heads = min(HGRN_HEADS, d_model // C_HEAD_DIM)
    wcols = heads * C_HEAD_DIM
    per = d_model // wcols

    def spec(order, part):
        return pl.BlockSpec((blk, wcols), lambda b, h, j: (b * n_blocks + order(j), part * per + h))

    def out_spec(order):
        return pl.BlockSpec((blk, wcols), lambda b, h, j: (b * n_blocks + order(j), h))

    out = jax.ShapeDtypeStruct((batch * seq_all, d_model), F32)
    return pl.pallas_call(
        _hgrn_body,
        grid=(batch, per, n_blocks),
        in_specs=[spec(fwd, 0), spec(fwd, 1), spec(fwd, 3), spec(bwd, 0), spec(bwd, 1), spec(bwd, 4),
                  pl.BlockSpec((2, wcols), lambda b, h, j: (0, h))],
        out_specs=[out_spec(fwd), out_spec(bwd)],
        out_shape=[out, out],
        scratch_shapes=[pltpu.VMEM((2, heads, C_HEAD_DIM, C_HEAD_DIM), F32)],
        compiler_params=_cp("arbitrary", "arbitrary", "arbitrary"),
        name="hgrn_scan",
    )(proj, proj, proj, proj, proj, proj, lb)


def _pack_halves(v):
    half = v.shape[1] // 2
    bits = lax.bitcast_convert_type(v.astype(BF16).astype(F32), jnp.uint32)
    return (bits[:, :half] >> 16) | (bits[:, half:] & jnp.uint32(0xFFFF0000))


def _unpack_halves(p):
    lo = lax.bitcast_convert_type(p << 16, F32)
    hi = lax.bitcast_convert_type(p & jnp.uint32(0xFFFF0000), F32)
    return lo, hi


def _ffn_router_body(x_ref, g_ref, mod_ref, rwt_ref, rb_ref, hb_ref, hp_ref, idx_ref, wgt_ref, rank_ref, cnt_ref):
    @pl.when(pl.program_id(0) == 0)
    def _():
        cnt_ref[...] = jnp.zeros_like(cnt_ref)

    y = _rms(x_ref[...], g_ref[...])
    h2 = y * (1.0 + mod_ref[4:5, :]) + mod_ref[3:4, :]
    hb = h2.astype(BF16)
    hb_ref[...] = hb
    hp_ref[...] = _pack_halves(hb)
    tm = h2.shape[0]
    logits = lax.dot_general(rwt_ref[...], h2, NT_DIMS, precision=lax.Precision.HIGHEST,
                             preferred_element_type=F32)
    s = _sigmoid(logits)
    sel = s + rb_ref[...]
    iota_g = lax.broadcasted_iota(jnp.int32, (GROUP_SIZE, tm), 0)
    group_scores = []
    for g in range(N_GROUPS):
        xg = sel[g * GROUP_SIZE:(g + 1) * GROUP_SIZE, :]
        m1 = jnp.max(xg, axis=0, keepdims=True)
        i1 = jnp.min(jnp.where(xg == m1, iota_g, GROUP_SIZE), axis=0, keepdims=True)
        m2 = jnp.max(jnp.where(iota_g == i1, NEG_INF, xg), axis=0, keepdims=True)
        group_scores.append(m1 + m2)
    gsc = jnp.concatenate(group_scores, axis=0)
    iota_n = lax.broadcasted_iota(jnp.int32, (N_GROUPS, tm), 0)
    gmask = jnp.zeros((N_GROUPS, tm), F32)
    for _ in range(TOPK_GROUPS):
        m = jnp.max(gsc, axis=0, keepdims=True)
        i = jnp.min(jnp.where(gsc == m, iota_n, N_GROUPS), axis=0, keepdims=True)
        hit = iota_n == i
        gmask = jnp.where(hit, 1.0, gmask)
        gsc = jnp.where(hit, NEG_INF, gsc)
    emask = jnp.concatenate([jnp.broadcast_to(gmask[g:g + 1, :], (GROUP_SIZE, tm)) for g in range(N_GROUPS)], axis=0)
    cur = jnp.where(emask > 0.0, sel, NEG_INF)
    iota_e = lax.broadcasted_iota(jnp.int32, (N_EXPERTS, tm), 0)
    idx_rows, w_rows, hits = [], [], []
    for _ in range(TOP_K):
        m = jnp.max(cur, axis=0, keepdims=True)
        i = jnp.min(jnp.where(cur == m, iota_e, N_EXPERTS), axis=0, keepdims=True)
        hit = iota_e == i
        idx_rows.append(i)
        hits.append(hit)
        w_rows.append(jnp.sum(jnp.where(hit, s, 0.0), axis=0, keepdims=True))
        cur = jnp.where(hit, NEG_INF, cur)
    w = jnp.concatenate(w_rows, axis=0)
    idx_ref[...] = jnp.concatenate(idx_rows, axis=0)
    wgt_ref[...] = ROUTED_SCALE * w / jnp.sum(w, axis=0, keepdims=True)
    chosen = jnp.zeros((N_EXPERTS, tm), F32)
    for hit in hits:
        chosen = jnp.where(hit, 1.0, chosen)
    earlier = (lax.broadcasted_iota(jnp.int32, (tm, tm), 0) < lax.broadcasted_iota(jnp.int32, (tm, tm), 1))
    before = cnt_ref[...] + jnp.dot(chosen.astype(BF16), earlier.astype(BF16), preferred_element_type=F32)
    rank_ref[...] = jnp.concatenate(
        [jnp.sum(jnp.where(hit, before, 0.0), axis=0, keepdims=True) for hit in hits], axis=0).astype(jnp.int32)
    cnt_ref[...] = cnt_ref[...] + jnp.sum(chosen, axis=1, keepdims=True)


def _ffn_router(x, g, mod, sel, router_wt, router_b, layer):
    rows, d = x.shape
    return pl.pallas_call(
        _ffn_router_body,
        grid=(rows // ROW_TILE,),
        in_specs=[pl.BlockSpec((ROW_TILE, d), lambda i: (i, 0)),
                  pl.BlockSpec((1, d), lambda i: (0, 0)),
                  pl.BlockSpec((None, 6, d), lambda i: (sel(i), 0, 0)),
                  pl.BlockSpec((None, N_EXPERTS, d), lambda i: (layer, 0, 0)),
                  pl.BlockSpec((None, N_EXPERTS, 1), lambda i: (layer, 0, 0))],
        out_specs=[pl.BlockSpec((ROW_TILE, d), lambda i: (i, 0)),
                   pl.BlockSpec((ROW_TILE, d // 2), lambda i: (i, 0)),
                   pl.BlockSpec((TOP_K, ROW_TILE), lambda i: (0, i)),
                   pl.BlockSpec((TOP_K, ROW_TILE), lambda i: (0, i)),
                   pl.BlockSpec((TOP_K, ROW_TILE), lambda i: (0, i)),
                   pl.BlockSpec((N_EXPERTS, 1), lambda i: (0, 0))],
        out_shape=[jax.ShapeDtypeStruct((rows, d), BF16),
                   jax.ShapeDtypeStruct((rows, d // 2), jnp.uint32),
                   jax.ShapeDtypeStruct((TOP_K, rows), jnp.int32),
                   jax.ShapeDtypeStruct((TOP_K, rows), F32),
                   jax.ShapeDtypeStruct((TOP_K, rows), jnp.int32),
                   jax.ShapeDtypeStruct((N_EXPERTS, 1), F32)],
        compiler_params=_cp("arbitrary"),
        name=f"ffn_router{layer}",
    )(x, g, mod, router_wt, router_b)


def _dispatch(idx_t, rank_t, counts):
    tm = MOE_TILE
    slots = idx_t.shape[0] * idx_t.shape[1]
    n_tiles = slots // tm
    counts = counts.reshape(-1).astype(jnp.int32)
    ends = jnp.cumsum(counts)
    starts = ends - counts
    is_e = idx_t[:, :, None] == jnp.arange(N_EXPERTS, dtype=jnp.int32)
    pos = (jnp.sum(jnp.where(is_e, starts, 0), axis=-1) + rank_t).T.reshape(-1)
    cuts = jnp.sort(jnp.concatenate([jnp.arange(n_tiles, dtype=jnp.int32) * tm, ends[:-1]]))
    nxt = jnp.concatenate([cuts[1:], jnp.full((1,), slots, jnp.int32)])
    v_tile = jnp.minimum(cuts // tm, n_tiles - 1)
    v_exp = jnp.minimum(jnp.sum((ends[None, :] <= cuts[:, None]).astype(jnp.int32), axis=1), N_EXPERTS - 1)
    v_lo = cuts - v_tile * tm
    v_hi = nxt - v_tile * tm
    n_vis = cuts.shape[0]
    v_first = (v_exp != jnp.concatenate([jnp.full((1,), -1, jnp.int32), v_exp[:-1]])).astype(jnp.int32)
    order = jnp.arange(n_vis, dtype=jnp.int32)
    next_first = lax.cummin(jnp.where(v_first == 1, order, n_vis)[::-1])[::-1]
    after = jnp.concatenate([next_first[1:], jnp.full((1,), n_vis, jnp.int32)])
    v_next = jnp.where(after < n_vis, v_exp[jnp.minimum(after, n_vis - 1)], -1)
    return (pos.astype(jnp.int32), v_tile.astype(jnp.int32), v_exp.astype(jnp.int32), v_lo, v_hi, v_first,
            v_next.astype(jnp.int32))


def _push_body(pos_ref, h_ref, xs_hbm, sem):
    i = pl.program_id(0)
    tt = h_ref.shape[0]
    base = i * tt * TOP_K

    def issue(r, carry):
        src = h_ref.at[pl.ds(r, 1)]
        for k in range(TOP_K):
            pltpu.make_async_copy(src, xs_hbm.at[pl.ds(pos_ref[base + r * TOP_K + k], 1)], sem).start(
                priority=k % DMA_PRIORITIES)
        return carry

    lax.fori_loop(0, tt, issue, 0)
    for _ in range(TOP_K):
        pltpu.make_async_copy(h_ref, xs_hbm.at[pl.ds(0, tt)], sem).wait()


def _push(h2f, pos, layer):
    rows, d = h2f.shape
    tt = PUSH_TOKENS
    grid_spec = pltpu.PrefetchScalarGridSpec(
        num_scalar_prefetch=1,
        grid=(rows // tt,),
        in_specs=[pl.BlockSpec((tt, d), lambda i, ps: (i, 0))],
        out_specs=pl.BlockSpec(memory_space=pl.ANY),
        scratch_shapes=[pltpu.SemaphoreType.DMA(())],
    )
    return pl.pallas_call(
        _push_body,
        grid_spec=grid_spec,
        out_shape=jax.ShapeDtypeStruct((rows * TOP_K, d), h2f.dtype),
        compiler_params=_cp("arbitrary"),
        name=f"push{layer}",
    )(pos, h2f)


def _expert_body(vt_ref, ve_ref, lo_ref, hi_ref, first_ref, next_ref, x_ref, wg_hbm, wu_hbm, wd_hbm, y_ref,
                 wgf, wuf, wdf, wgb, wub, wdb, sem, *, layer):
    v = pl.program_id(0)
    lo = lo_ref[v]
    hi = hi_ref[v]

    def weight_copies(e):
        return (pltpu.make_async_copy(wg_hbm.at[layer, e], wgf, sem),
                pltpu.make_async_copy(wu_hbm.at[layer, e], wuf, sem),
                pltpu.make_async_copy(wd_hbm.at[layer, e], wdf, sem))

    @pl.when(v == 0)
    def _():
        for cp in weight_copies(ve_ref[0]):
            cp.start()

    @pl.when(first_ref[v] == 1)
    def _():
        for cp in weight_copies(ve_ref[v]):
            cp.wait()
        wgb[...] = wgf[...].astype(BF16)
        wub[...] = wuf[...].astype(BF16)
        wdb[...] = wdf[...].astype(BF16)

        @pl.when(next_ref[v] >= 0)
        def _():
            for cp in weight_copies(next_ref[v]):
                cp.start()

    @pl.when(hi > lo)
    def _():
        x_lo, x_hi = _unpack_halves(x_ref[...])
        x_lo = x_lo.astype(BF16)
        x_hi = x_hi.astype(BF16)
        half = x_lo.shape[1]
        g = (jnp.dot(x_lo, wgb[:half, :], preferred_element_type=F32)
             + jnp.dot(x_hi, wgb[half:, :], preferred_element_type=F32))
        u = (jnp.dot(x_lo, wub[:half, :], preferred_element_type=F32)
             + jnp.dot(x_hi, wub[half:, :], preferred_element_type=F32))
        hid = (_silu(g) * u).astype(BF16)
        y = _pack_halves(jnp.dot(hid, wdb[...], preferred_element_type=F32))
        r = lax.broadcasted_iota(jnp.int32, (y.shape[0], 1), 0)
        mine = (r >= lo) & (r < hi)

        @pl.when(lo == 0)
        def _():
            y_ref[...] = jnp.where(mine, y, jnp.uint32(0))

        @pl.when(lo > 0)
        def _():
            y_ref[...] = jnp.where(mine, y, y_ref[...])


def _experts(xs, visits, wg, wu, wd, layer):
    slots, dp = xs.shape
    d = wg.shape[2]
    de = wg.shape[3]
    tm = MOE_TILE
    hbm = pl.BlockSpec(memory_space=pl.ANY)
    grid_spec = pltpu.PrefetchScalarGridSpec(
        num_scalar_prefetch=len(visits),
        grid=(visits[0].shape[0],),
        in_specs=[pl.BlockSpec((tm, dp), lambda v, vt, *_: (vt[v], 0)), hbm, hbm, hbm],
        out_specs=pl.BlockSpec((tm, dp), lambda v, vt, *_: (vt[v], 0)),
        scratch_shapes=[pltpu.VMEM((d, de), F32), pltpu.VMEM((d, de), F32), pltpu.VMEM((de, d), F32),
                        pltpu.VMEM((d, de), BF16), pltpu.VMEM((d, de), BF16), pltpu.VMEM((de, d), BF16),
                        pltpu.SemaphoreType.DMA(())],
    )
    return pl.pallas_call(
        functools.partial(_expert_body, layer=layer),
        grid_spec=grid_spec,
        out_shape=jax.ShapeDtypeStruct((slots, dp), jnp.uint32),
        compiler_params=_cp("arbitrary"),
        name=f"experts{layer}",
    )(*visits, xs, wg, wu, wd)


def _combine_body(pos_ref, w_ref, y_hbm, o_ref, buf, sem):
    i = pl.program_id(0)
    tt = o_ref.shape[0]
    n = TOP_K * tt
    slot = i % 2
    half = buf.shape[3]
    group = 8

    def issue_rows(step, into, r0, rows):
        for j in range(rows):
            for k in range(TOP_K):
                pltpu.make_async_copy(y_hbm.at[pl.ds(pos_ref[step * n + (r0 + j) * TOP_K + k], 1)],
                                      buf.at[into, k, pl.ds(r0 + j, 1)], sem.at[into]).start(
                    priority=k % DMA_PRIORITIES)

    @pl.when(i == 0)
    def _():
        def first(g, carry):
            issue_rows(0, 0, g * group, group)
            return carry

        lax.fori_loop(0, tt // group, first, 0)

    for k in range(TOP_K):
        pltpu.make_async_copy(y_hbm.at[pl.ds(0, tt)], buf.at[slot, k], sem.at[slot]).wait()
    last = pl.num_programs(0) - 1
    nxt = jnp.minimum(i + 1, last)

    def body(g, carry):
        r0 = pl.multiple_of(g * group, group)
        issue_rows(nxt, 1 - slot, r0, group)
        w = w_ref[pl.ds(r0, group), :]
        acc_lo, acc_hi = _unpack_halves(buf[slot, 0, pl.ds(r0, group), :])
        acc_lo = w[:, 0:1] * acc_lo
        acc_hi = w[:, 0:1] * acc_hi
        for k in range(1, TOP_K):
            y_lo, y_hi = _unpack_halves(buf[slot, k, pl.ds(r0, group), :])
            acc_lo = acc_lo + w[:, k:k + 1] * y_lo
            acc_hi = acc_hi + w[:, k:k + 1] * y_hi
        o_ref[pl.ds(r0, group), :half] = acc_lo
        o_ref[pl.ds(r0, group), half:] = acc_hi
        return carry

    lax.fori_loop(0, tt // group, body, 0)

    @pl.when(i == last)
    def _():
        for k in range(TOP_K):
            pltpu.make_async_copy(y_hbm.at[pl.ds(0, tt)], buf.at[1 - slot, k], sem.at[1 - slot]).wait()


def _combine(ys, pos_flat, wgt, rows, layer):
    dp = ys.shape[1]
    d = 2 * dp
    tt = COMBINE_TOKENS
    grid_spec = pltpu.PrefetchScalarGridSpec(
        num_scalar_prefetch=1,
        grid=(rows // tt,),
        in_specs=[pl.BlockSpec((tt, TOP_K), lambda i, ps: (i, 0)),
                  pl.BlockSpec(memory_space=pl.ANY)],
        out_specs=pl.BlockSpec((tt, d), lambda i, ps: (i, 0)),
        scratch_shapes=[pltpu.VMEM((2, TOP_K, tt, dp), jnp.uint32), pltpu.SemaphoreType.DMA((2,))],
    )
    return pl.pallas_call(
        _combine_body,
        grid_spec=grid_spec,
        out_shape=jax.ShapeDtypeStruct((rows, d), F32),
        compiler_params=_cp("arbitrary"),
        name=f"combine{layer}",
    )(pos_flat, wgt, ys)


def _moe(x1, mod, sel, layer, norm_g, router_w, router_b, exp_wg, exp_wu, exp_wd, s_gate, s_up, s_down, tm_big):
    rows, d = x1.shape
    router_wt = jnp.swapaxes(router_w, 1, 2)
    h2b, h2p, idx_t, wgt_t, rank_t, counts = _ffn_router(x1, norm_g[layer][None], mod, sel, router_wt,
                                                         router_b[:, :, None], layer)
    pos, *visits = _dispatch(idx_t, rank_t, counts)
    xs = _push(h2p, pos, layer)
    ys = _experts(xs, visits, exp_wg, exp_wu, exp_wd, layer)
    routed = _combine(ys, pos, wgt_t.T, rows, layer)
    hs = _glu(h2b, s_gate, s_up, layer, tm=tm_big, name=f"shared_glu{layer}")
    return _linear([hs], s_down, layer, n_cols=d, tm=ROW_TILE, out_dtype=F32, name=f"shared_down{layer}",
                   res=(x1, mod, 5, routed, sel), tn=4 * COL_TILE)


def _rope_tables(ctx_len, lat_len):
    rows = lat_len // GRID_W
    pos_r = np.repeat(np.arange(rows), GRID_W).astype(np.float32)
    pos_c = np.tile(np.arange(GRID_W), rows).astype(np.float32)
    n = A_HEAD_DIM // 4
    inv = (np.float32(ROPE_BASE) ** (-np.arange(n, dtype=np.float32) / np.float32(n))).astype(np.float32)
    ang = np.concatenate([pos_r[:, None] * inv, pos_c[:, None] * inv], axis=-1).astype(np.float32)
    cos, sin = np.cos(ang).astype(np.float32), np.sin(ang).astype(np.float32)
    cos_t = np.concatenate([np.ones((ctx_len, A_HEAD_DIM), np.float32), np.concatenate([cos, cos], axis=-1)], axis=0)
    sin_t = np.concatenate([np.zeros((ctx_len, A_HEAD_DIM), np.float32), np.concatenate([-sin, sin], axis=-1)], axis=0)
    return jnp.asarray(cos_t), jnp.asarray(sin_t)


def kernel(x, c, ctx, c_ctx, ada_w, ada_b, norm_mix_g, norm_ffn_g, ab_w_in, ab_w_out, a_q_norm_g, a_k_norm_g,
           a_sink, b_conv_w, b_gate_b, b_norm_g, c_w_in, c_w_out, c_lb_logits, c_norm_g, router_w, router_b,
           exp_w_gate, exp_w_up, exp_w_down, shared_w_gate, shared_w_up, shared_w_down):
    batch, lat_len, d = x.shape
    ctx_len = ctx.shape[1]
    depth = ada_w.shape[0]
    seq_all = ctx_len + lat_len
    rows = batch * seq_all
    assert ctx_len % ROW_TILE == 0 and lat_len % ROW_TILE == 0 and ctx_len % MLSTM_CHUNK == 0
    assert lat_len % GRID_W == 0 and d % (min(HGRN_HEADS, d // C_HEAD_DIM) * C_HEAD_DIM) == 0
    tiles_per_sample = seq_all // ROW_TILE
    ctx_tiles = ctx_len // ROW_TILE
    tm_big = _row_tile_size(rows, 1152)
    tm_in = _row_tile_size(rows, 576)

    def sel(i):
        return jnp.where(i % tiles_per_sample < ctx_tiles, batch, i // tiles_per_sample)

    cos_t, sin_t = _rope_tables(ctx_len, lat_len)
    lb_soft = jax.nn.softmax(c_lb_logits.astype(F32), axis=0)
    lower_bounds = jnp.cumsum(lb_soft, axis=0) - lb_soft[0:1]
    cond = jnp.concatenate([jax.nn.silu(c), jax.nn.silu(c_ctx)[None]], axis=0)
    cond = jnp.pad(cond, ((0, 16 - cond.shape[0]), (0, 0))).astype(BF16)

    lat_tiles = lat_len // ROW_TILE

    def lat_map(i):
        return (i // lat_tiles) * tiles_per_sample + ctx_tiles + i % lat_tiles

    if depth > 1:
        xa = _SplitRows(
            ctx.reshape(batch * ctx_len, d), x.reshape(batch * lat_len, d),
            lambda i: (i // tiles_per_sample) * ctx_tiles + jnp.minimum(i % tiles_per_sample, ctx_tiles - 1),
            lambda i: (i // tiles_per_sample) * lat_tiles + jnp.clip(i % tiles_per_sample - ctx_tiles, 0,
                                                                    lat_tiles - 1),
            lambda i: i % tiles_per_sample < ctx_tiles)
    else:
        xa = jnp.concatenate([ctx, x], axis=1).reshape(rows, d)
    for layer in range(depth):
        last = layer == depth - 1
        tail = dict(row_map=lat_map, out_rows=batch * lat_len) if last else {}
        tail_sel = (lambda i: i // lat_tiles) if last else sel
        mod = _ada(cond, ada_w, ada_b, layer)[:batch + 1].reshape(batch + 1, 6, d)
        h = _modulate(xa, norm_mix_g[layer][None], mod, sel, shift_row=0, scale_row=1, name=f"mod_mix{layer}")
        if layer % 2 == 0:
            e = layer // 2
            w_in_t = jnp.swapaxes(ab_w_in, 1, 2)
            proj = _linear([h], w_in_t, e, n_cols=AB_MAIN, tm=tm_in, out_dtype=F32, name=f"ab_in{layer}",
                           tn=2 * COL_TILE, w_t=True)
            w_gates_t = jnp.pad(w_in_t[e, AB_MAIN:, :], ((0, LANE_BLOCK - AB_GATES), (0, 0)))[None]
            graw = _linear([h], w_gates_t, 0, n_cols=LANE_BLOCK, tm=tm_big, out_dtype=F32, name=f"ab_gates{layer}",
                           w_t=True)
            qn, kn, vb = _attn_prep(proj, cos_t, sin_t, a_q_norm_g[e][None], a_k_norm_g[e][None], tiles_per_sample)
            ya = _attention(qn, kn, vb, a_sink[e], batch=batch, seq_all=seq_all, ctx_len=ctx_len)
            qk = _mlstm_conv(proj, b_conv_w[e], batch=batch, seq_all=seq_all, ctx_len=ctx_len)
            gate_b = jnp.pad(b_gate_b[e], (0, LANE_BLOCK - AB_GATES))[None]
            gates = _mlstm_gates(graw, gate_b)[:, :AB_GATES]
            gates_t = gates.reshape(batch, seq_all, 4, B_HEADS).transpose(0, 3, 2, 1)
            hf, hb = _mlstm_scan(qk, proj, gates_t, batch=batch, seq_all=seq_all, ctx_len=ctx_len)
            yb = _headnorm_gate(hf, hb, proj, AB_MAIN - B_V, b_norm_g[e][None], head_dim=B_V_DIM,
                                gate_fn=_sigmoid, name="mlstm_out")
            xa = _linear([ya, yb], ab_w_out, e, n_cols=d, tm=ROW_TILE, out_dtype=F32, name=f"ab_out{layer}",
                         res=(xa, mod, 2, None, tail_sel), tn=2 * COL_TILE, **tail)
        else:
            o = layer // 2
            proj = _linear([h], c_w_in, o, n_cols=5 * d, tm=tm_in, out_dtype=F32, name=f"c_in{layer}",
                           tn=2 * COL_TILE)
            of, ob = _hgrn_scan(proj, lower_bounds[layer], batch=batch, seq_all=seq_all, ctx_len=ctx_len, d_model=d)
            yc = _headnorm_gate(of, ob, proj, 2 * d, c_norm_g[o][None], head_dim=C_HEAD_DIM, gate_fn=_silu,
                                name="hgrn_out")
            xa = _linear([yc], c_w_out, o, n_cols=d, tm=ROW_TILE, out_dtype=F32, name=f"c_out{layer}",
                         res=(xa, mod, 2, None, tail_sel), tn=2 * COL_TILE, **tail)
        xa = _moe(xa, mod, tail_sel, layer, norm_ffn_g, router_w, router_b, exp_w_gate, exp_w_up, exp_w_down,
                  shared_w_gate, shared_w_up, shared_w_down, _row_tile_size(xa.shape[0], 1152))
    return xa.reshape(batch, lat_len, d)
```

```python
import functools

import jax
import jax.numpy as jnp
import numpy as np
from jax import lax
from jax.experimental import pallas as pl
from jax.experimental.pallas import tpu as pltpu

F32 = jnp.float32
BF16 = jnp.bfloat16
NEG_INF = float("-inf")

EPS = 1e-6
GRID_W = 64
ROPE_BASE = 10000.0

A_HEADS = 16
A_KV_HEADS = 4
A_GROUP = A_HEADS // A_KV_HEADS
A_HEAD_DIM = 128
WINDOW = 128
A_Q = A_HEADS * A_HEAD_DIM
A_KV = A_KV_HEADS * A_HEAD_DIM

B_HEADS = 8
B_QK_DIM = 128
B_V_DIM = 256
B_CONV_W = 5
GATE_CAP = 15.0
B_QK = B_HEADS * B_QK_DIM
B_V = B_HEADS * B_V_DIM
AB_MAIN = A_Q + 2 * A_KV + 2 * B_QK + 2 * B_V
AB_GATES = 4 * B_HEADS

C_HEAD_DIM = 128

N_EXPERTS = 64
TOP_K = 8
N_GROUPS = 8
TOPK_GROUPS = 4
GROUP_SIZE = N_EXPERTS // N_GROUPS
ROUTED_SCALE = 2.5

VMEM_LIMIT_BYTES = 60 * 1024 * 1024
LANE_BLOCK = 128
ROW_TILE = 256
WIDE_ROWS_MAX = 1152
COL_TILE = 512
MLSTM_CHUNK = 256
MLSTM_HEADS = 2
HGRN_BLOCK = 256
HGRN_CHUNK = 32
HGRN_HEADS = 8
MOE_TILE = 256
DMA_PRIORITIES = 2
PUSH_TOKENS = 128
COMBINE_TOKENS = 64

NT_DIMS = (((1,), (1,)), ((), ()))
TN_DIMS = (((0,), (0,)), ((), ()))


def _cp(*sem):
    return pltpu.CompilerParams(dimension_semantics=sem, vmem_limit_bytes=VMEM_LIMIT_BYTES)


def _sigmoid(x):
    return jax.nn.sigmoid(x)


def _silu(x):
    return x * jax.nn.sigmoid(x)


def _rms(x, g):
    return x * lax.rsqrt(jnp.mean(x * x, axis=-1, keepdims=True) + EPS) * g


def _row_tile_size(rows, limit):
    best = 16
    for t in range(16, limit + 1, 16):
        if rows % t == 0:
            best = t
    return best


def _ada_body(a_ref, w_ref, b_ref, o_ref):
    o_ref[...] = jnp.dot(a_ref[...], w_ref[...].astype(BF16), preferred_element_type=F32) + b_ref[...]


def _ada(a, ada_w, ada_b, layer):
    depth, d, d6 = ada_w.shape
    rows = a.shape[0]
    return pl.pallas_call(
        _ada_body,
        grid=(d6 // COL_TILE,),
        in_specs=[pl.BlockSpec((rows, d), lambda j: (0, 0)),
                  pl.BlockSpec((None, d, COL_TILE), lambda j: (layer, 0, j)),
                  pl.BlockSpec((None, 1, COL_TILE), lambda j: (layer, 0, j))],
        out_specs=pl.BlockSpec((rows, COL_TILE), lambda j: (0, j)),
        out_shape=jax.ShapeDtypeStruct((rows, d6), F32),
        compiler_params=_cp("arbitrary"),
        name=f"ada{layer}",
    )(a, ada_w, ada_b.reshape(depth, 1, d6))


class _SplitRows:
    def __init__(self, ctx, lat, ctx_map, lat_map, is_ctx):
        self.arrays = (ctx, lat)
        self.maps = (ctx_map, lat_map)
        self.is_ctx = is_ctx
        self.rows = ctx.shape[0] + lat.shape[0]

    def specs(self, block, col, tile_arg):
        return [pl.BlockSpec(block, lambda *g, m=m: (m(g[tile_arg]), col(*g))) for m in self.maps]


def _pick_rows(is_ctx, tile, ctx_ref, lat_ref):
    return jnp.where(is_ctx(tile), ctx_ref[...], lat_ref[...])


def _modulate_body(*refs, shift_row, scale_row, is_ctx):
    *x_refs, g_ref, mod_ref, o_ref = refs
    x = x_refs[0][...] if is_ctx is None else _pick_rows(is_ctx, pl.program_id(0), *x_refs)
    y = _rms(x, g_ref[...])
    o_ref[...] = (y * (1.0 + mod_ref[scale_row:scale_row + 1, :])
                  + mod_ref[shift_row:shift_row + 1, :]).astype(o_ref.dtype)


def _modulate(x, g, mod, sel, *, shift_row, scale_row, name):
    split = isinstance(x, _SplitRows)
    xs = x.arrays if split else (x,)
    rows, d = (x.rows if split else x.shape[0]), xs[0].shape[1]
    x_specs = (x.specs((ROW_TILE, d), lambda i: 0, 0) if split
               else [pl.BlockSpec((ROW_TILE, d), lambda i: (i, 0))])
    return pl.pallas_call(
        functools.partial(_modulate_body, shift_row=shift_row, scale_row=scale_row,
                          is_ctx=x.is_ctx if split else None),
        grid=(rows // ROW_TILE,),
        in_specs=x_specs + [pl.BlockSpec((1, d), lambda i: (0, 0)),
                            pl.BlockSpec((None, 6, d), lambda i: (sel(i), 0, 0))],
        out_specs=pl.BlockSpec((ROW_TILE, d), lambda i: (i, 0)),
        out_shape=jax.ShapeDtypeStruct((rows, d), BF16),
        compiler_params=_cp("arbitrary"),
        name=name,
    )(*xs, g, mod)


def _linear_body(*refs, n_a, gate_row, has_extra, w_t, is_ctx):
    def mm(a, wb):
        if w_t:
            return lax.dot_general(a, wb, NT_DIMS, preferred_element_type=F32)
        return jnp.dot(a, wb, preferred_element_type=F32)

    a_refs = refs[:n_a]
    w_refs = refs[n_a:2 * n_a]
    p = 2 * n_a
    x_refs = mod_ref = e_ref = None
    if gate_row is not None:
        n_x = 1 if is_ctx is None else 2
        x_refs, mod_ref = refs[p:p + n_x], refs[p + n_x]
        p += n_x + 1
        if has_extra:
            e_ref = refs[p]
            p += 1
    o_ref = refs[p]
    wb_refs = refs[p + 1:]

    @pl.when(pl.program_id(1) == 0)
    def _():
        for w_ref, wb in zip(w_refs, wb_refs):
            wb[...] = w_ref[...].astype(BF16)

    acc = mm(a_refs[0][...], wb_refs[0][...])
    for a_ref, wb in zip(a_refs[1:], wb_refs[1:]):
        acc = acc + mm(a_ref[...], wb[...])
    if gate_row is not None:
        if e_ref is not None:
            acc = acc + e_ref[...]
        x = x_refs[0][...] if is_ctx is None else _pick_rows(is_ctx, pl.program_id(1), *x_refs)
        acc = x + mod_ref[gate_row:gate_row + 1, :] * acc
    o_ref[...] = acc.astype(o_ref.dtype)


def _linear(a_list, w, layer, *, n_cols, tm, out_dtype, name, res=None, tn=COL_TILE, row_map=None, out_rows=None,
            w_t=False):
    n_a = len(a_list)
    rows = a_list[0].shape[0]
    if row_map is None:
        row_map = lambda i: i
    else:
        rows = out_rows
    k_each = w.shape[2 if w_t else 1] // n_a
    tn = min(tn, n_cols)
    while n_cols % tn:
        tn //= 2
    in_specs = [pl.BlockSpec((tm, k_each), lambda j, i: (row_map(i), 0)) for _ in a_list]
    for idx in range(n_a):
        if w_t:
            in_specs.append(pl.BlockSpec((None, tn, k_each), lambda j, i, idx=idx: (layer, j, idx)))
        else:
            in_specs.append(pl.BlockSpec((None, k_each, tn), lambda j, i, idx=idx: (layer, idx, j)))
    args = list(a_list) + [w] * n_a
    gate_row, has_extra, is_ctx = None, False, None
    if res is not None:
        x, mod, gate_row, extra, sel = res
        if isinstance(x, _SplitRows):
            is_ctx = x.is_ctx
            in_specs += x.specs((tm, tn), lambda j, i: j, 1)
            args += list(x.arrays)
        else:
            in_specs.append(pl.BlockSpec((tm, tn), lambda j, i: (row_map(i), j)))
            args.append(x)
        in_specs.append(pl.BlockSpec((None, 6, tn), lambda j, i: (sel(i), 0, j)))
        args.append(mod)
        if extra is not None:
            has_extra = True
            in_specs.append(pl.BlockSpec((tm, tn), lambda j, i: (i, j)))
            args.append(extra)
    return pl.pallas_call(
        functools.partial(_linear_body, n_a=n_a, gate_row=gate_row, has_extra=has_extra, w_t=w_t, is_ctx=is_ctx),
        grid=(n_cols // tn, rows // tm),
        in_specs=in_specs,
        out_specs=pl.BlockSpec((tm, tn), lambda j, i: (i, j)),
        out_shape=jax.ShapeDtypeStruct((rows, n_cols), out_dtype),
        scratch_shapes=[pltpu.VMEM((tn, k_each) if w_t else (k_each, tn), BF16) for _ in range(n_a)],
        compiler_params=_cp("arbitrary", "arbitrary"),
        name=name,
    )(*args)


def _glu_body(a_ref, wg_ref, wu_ref, o_ref, wgb, wub):
    @pl.when(pl.program_id(1) == 0)
    def _():
        wgb[...] = wg_ref[...].astype(BF16)
        wub[...] = wu_ref[...].astype(BF16)

    a = a_ref[...]
    g = jnp.dot(a, wgb[...], preferred_element_type=F32)
    u = jnp.dot(a, wub[...], preferred_element_type=F32)
    o_ref[...] = (_silu(g) * u).astype(o_ref.dtype)


def _glu(a, wg, wu, layer, *, tm, name):
    rows, k = a.shape
    n_cols = wg.shape[2]
    tn = min(COL_TILE // 2, n_cols)
    wspec = pl.BlockSpec((None, k, tn), lambda j, i: (layer, 0, j))
    return pl.pallas_call(
        _glu_body,
        grid=(n_cols // tn, rows // tm),
        in_specs=[pl.BlockSpec((tm, k), lambda j, i: (i, 0)), wspec, wspec],
        out_specs=pl.BlockSpec((tm, tn), lambda j, i: (i, j)),
        out_shape=jax.ShapeDtypeStruct((rows, n_cols), BF16),
        scratch_shapes=[pltpu.VMEM((k, tn), BF16), pltpu.VMEM((k, tn), BF16)],
        compiler_params=_cp("arbitrary", "arbitrary"),
        name=name,
    )(a, wg, wu)


def _attn_prep_body(p_ref, cos_ref, sin_ref, qg_ref, kg_ref, q_ref, k_ref, v_ref):
    cos = cos_ref[...]
    sin = sin_ref[...]

    def norm_rope(xh, g):
        y = _rms(xh, g)
        return y * cos + pltpu.roll(y, A_HEAD_DIM // 2, axis=1) * sin

    for hd in range(A_HEADS):
        sl = slice(hd * A_HEAD_DIM, (hd + 1) * A_HEAD_DIM)
        q_ref[:, sl] = norm_rope(p_ref[:, sl], qg_ref[...]).astype(BF16)
    for hd in range(A_KV_HEADS):
        sl = slice(hd * A_HEAD_DIM, (hd + 1) * A_HEAD_DIM)
        k_ref[:, sl] = norm_rope(p_ref[:, A_Q + hd * A_HEAD_DIM:A_Q + (hd + 1) * A_HEAD_DIM], kg_ref[...]).astype(BF16)
    v_ref[...] = p_ref[:, A_Q + A_KV:A_Q + 2 * A_KV].astype(BF16)


def _attn_prep(proj, cos_t, sin_t, qg, kg, tiles_per_sample):
    rows = proj.shape[0]
    width = A_Q + 2 * A_KV
    return pl.pallas_call(
        _attn_prep_body,
        grid=(rows // ROW_TILE,),
        in_specs=[pl.BlockSpec((ROW_TILE, width), lambda i: (i, 0)),
                  pl.BlockSpec((ROW_TILE, A_HEAD_DIM), lambda i: (i % tiles_per_sample, 0)),
                  pl.BlockSpec((ROW_TILE, A_HEAD_DIM), lambda i: (i % tiles_per_sample, 0)),
                  pl.BlockSpec((1, A_HEAD_DIM), lambda i: (0, 0)),
                  pl.BlockSpec((1, A_HEAD_DIM), lambda i: (0, 0))],
        out_specs=[pl.BlockSpec((ROW_TILE, A_Q), lambda i: (i, 0)),
                   pl.BlockSpec((ROW_TILE, A_KV), lambda i: (i, 0)),
                   pl.BlockSpec((ROW_TILE, A_KV), lambda i: (i, 0))],
        out_shape=[jax.ShapeDtypeStruct((rows, A_Q), BF16),
                   jax.ShapeDtypeStruct((rows, A_KV), BF16),
                   jax.ShapeDtypeStruct((rows, A_KV), BF16)],
        compiler_params=_cp("arbitrary"),
        name="attn_prep",
    )(proj, cos_t, sin_t, qg, kg)


def _attn_body(sink_ref, q_ref, kp_ref, ko_ref, kn_ref, kc_ref, vp_ref, vo_ref, vn_ref, vc_ref, o_ref,
               *, ctx_blocks, lat_len):
    h = pl.program_id(1)
    j = pl.program_id(2)
    n = j - ctx_blocks
    nw = 3 * WINDOW
    m_ctx = kc_ref.shape[0]
    kw = jnp.concatenate([kp_ref[...], ko_ref[...], kn_ref[...], kc_ref[...]], axis=0)
    vw = jnp.concatenate([vp_ref[...], vo_ref[...], vn_ref[...], vc_ref[...]], axis=0)
    qi = lax.broadcasted_iota(jnp.int32, (WINDOW, nw + m_ctx), 0)
    wi = lax.broadcasted_iota(jnp.int32, (WINDOW, nw + m_ctx), 1)
    key_pos = (n - 1) * WINDOW + wi
    in_win = (jnp.abs(wi - WINDOW - qi) <= WINDOW) & (key_pos >= 0) & (key_pos < lat_len) & (n >= 0)
    valid = (wi >= nw) | in_win
    bias = jnp.where(valid, 0.0, NEG_INF).astype(F32)
    bias = jnp.concatenate([bias] * A_GROUP, axis=0)
    scale = A_HEAD_DIM ** -0.5
    heads = [slice(g * A_HEAD_DIM, (g + 1) * A_HEAD_DIM) for g in range(A_GROUP)]
    q = jnp.concatenate([q_ref[:, sl] for sl in heads], axis=0)
    sink = jnp.concatenate([jnp.full((WINDOW, 1), sink_ref[h * A_GROUP + g], F32) for g in range(A_GROUP)], axis=0)
    s = lax.dot_general(q, kw, NT_DIMS, preferred_element_type=F32) * scale + bias
    m = jnp.maximum(jnp.max(s, axis=-1, keepdims=True), sink)
    p = jnp.exp(s - m)
    denom = jnp.sum(p, axis=-1, keepdims=True) + jnp.exp(sink - m)
    o = jnp.dot(p.astype(BF16), vw, preferred_element_type=F32) / denom
    for g, sl in enumerate(heads):
        o_ref[:, sl] = o[g * WINDOW:(g + 1) * WINDOW, :].astype(o_ref.dtype)


def _attention(qn, kn, vb, sink, *, batch, seq_all, ctx_len):
    blocks = seq_all // WINDOW
    ctx_blocks = ctx_len // WINDOW
    last = blocks - 1
    qw = A_GROUP * A_HEAD_DIM

    def kv_spec(shift):
        def imap(b, h, j):
            return (b * blocks + jnp.clip(j + shift, ctx_blocks, last), h)
        return pl.BlockSpec((WINDOW, A_HEAD_DIM), imap)

    ctx_spec = pl.BlockSpec((ctx_len, A_HEAD_DIM), lambda b, h, j: (b * (seq_all // ctx_len), h))
    return pl.pallas_call(
        functools.partial(_attn_body, ctx_blocks=ctx_blocks, lat_len=seq_all - ctx_len),
        grid=(batch, A_KV_HEADS, blocks),
        in_specs=[pl.BlockSpec(memory_space=pltpu.SMEM),
                  pl.BlockSpec((WINDOW, qw), lambda b, h, j: (b * blocks + j, h)),
                  kv_spec(-1), kv_spec(0), kv_spec(1), ctx_spec,
                  kv_spec(-1), kv_spec(0), kv_spec(1), ctx_spec],
        out_specs=pl.BlockSpec((WINDOW, qw), lambda b, h, j: (b * blocks + j, h)),
        out_shape=jax.ShapeDtypeStruct((batch * seq_all, A_Q), BF16),
        compiler_params=_cp("arbitrary", "arbitrary", "arbitrary"),
        name="window_attention",
    )(sink, qn, kn, kn, kn, kn, vb, vb, vb, vb)


def _conv_body(x_ref, w_ref, o_ref, *, ctx_len, k_first_block):
    x = x_ref[...]
    t_len = x.shape[0]
    t = lax.broadcasted_iota(jnp.int32, x.shape, 0)
    half = B_CONV_W // 2
    acc = x * w_ref[half:half + 1, :]
    for d in range(-half, half + 1):
        if d == 0:
            continue
        xs = pltpu.roll(x, (-d) % t_len, axis=0)
        u = t + d
        ok = ((t < ctx_len) & (u >= 0) & (u < ctx_len)) | ((t >= ctx_len) & (u >= ctx_len) & (u < t_len))
        acc = acc + jnp.where(ok, xs, 0.0) * w_ref[half + d:half + d + 1, :]
    k_scale = jnp.where(pl.program_id(1) >= k_first_block, B_QK_DIM ** -0.5, 1.0).astype(F32)
    o_ref[...] = (_silu(acc) * k_scale).astype(o_ref.dtype)


def _mlstm_conv(proj, conv_w, *, batch, seq_all, ctx_len):
    first = (A_Q + 2 * A_KV) // LANE_BLOCK
    nblk = 2 * B_QK // LANE_BLOCK
    return pl.pallas_call(
        functools.partial(_conv_body, ctx_len=ctx_len, k_first_block=B_QK // LANE_BLOCK),
        grid=(batch, nblk),
        in_specs=[pl.BlockSpec((seq_all, LANE_BLOCK), lambda b, c: (b, first + c)),
                  pl.BlockSpec((B_CONV_W, LANE_BLOCK), lambda b, c: (0, c))],
        out_specs=pl.BlockSpec((seq_all, LANE_BLOCK), lambda b, c: (b, c)),
        out_shape=jax.ShapeDtypeStruct((batch * seq_all, 2 * B_QK), BF16),
        compiler_params=_cp("arbitrary", "arbitrary"),
        name="mlstm_conv",
    )(proj, conv_w)


def _gates_body(raw_ref, b_ref, o_ref):
    g = raw_ref[...] + b_ref[...]
    g = GATE_CAP * jnp.tanh(g / GATE_CAP)
    lane = lax.broadcasted_iota(jnp.int32, g.shape, 1)
    is_forget = (lane // B_HEADS) % 2 == 1
    log_sig = jnp.minimum(g, 0.0) - jnp.log(1.0 + jnp.exp(-jnp.abs(g)))
    o_ref[...] = jnp.where(is_forget, log_sig, g)


def _mlstm_gates(raw, gate_b):
    rows, width = raw.shape
    tr = _row_tile_size(rows, 2 * WIDE_ROWS_MAX)
    return pl.pallas_call(
        _gates_body,
        grid=(rows // tr,),
        in_specs=[pl.BlockSpec((tr, width), lambda i: (i, 0)),
                  pl.BlockSpec((1, width), lambda i: (0, 0))],
        out_specs=pl.BlockSpec((tr, width), lambda i: (i, 0)),
        out_shape=jax.ShapeDtypeStruct((rows, width), F32),
        compiler_params=_cp("arbitrary"),
        name="mlstm_gates",
    )(raw, gate_b)


def _mlstm_direction(q_ref, k_ref, v_ref, g_ref, o_ref, st_ref, n_ref, m_ref, d, rev, hd):
    c_len = q_ref.shape[0]
    qk_cols = slice(hd * B_QK_DIM, (hd + 1) * B_QK_DIM)
    v_cols = slice(hd * B_V_DIM, (hd + 1) * B_V_DIM)
    q = q_ref[:, qk_cols]
    k = k_ref[:, qk_cols]
    v = v_ref[:, v_cols].astype(BF16)
    ig = g_ref[hd, 2 * d:2 * d + 1, :]
    lf = g_ref[hd, 2 * d + 1:2 * d + 2, :]
    d = d * MLSTM_HEADS + hd
    r = lax.broadcasted_iota(jnp.int32, (c_len, c_len), 0)
    c = lax.broadcasted_iota(jnp.int32, (c_len, c_len), 1)
    eye = r == c
    tri = (c >= r) if rev else (c <= r)
    tri_t = (r >= c) if rev else (r <= c)
    lf_col = jnp.sum(jnp.where(eye, lf, 0.0), axis=1, keepdims=True)
    ig_col = jnp.sum(jnp.where(eye, ig, 0.0), axis=1, keepdims=True)
    b_col = jnp.sum(jnp.where(tri, lf, 0.0), axis=1, keepdims=True)
    b_row = jnp.sum(jnp.where(tri_t, lf_col, 0.0), axis=0, keepdims=True)
    m_prev = m_ref[d]
    d_st = b_col + m_prev
    d_in = jnp.where(tri, b_col - b_row + ig, NEG_INF)
    m_t = jnp.maximum(d_st, jnp.max(d_in, axis=1, keepdims=True))
    s = jnp.exp(d_in - m_t) * lax.dot_general(q, k, NT_DIMS, preferred_element_type=F32)
    w_st = jnp.exp(d_st - m_t)
    st = st_ref[d]
    num = (jnp.dot(s.astype(BF16), v, preferred_element_type=F32)
           + w_st * jnp.dot(q, st.astype(BF16), preferred_element_type=F32))
    qn = jnp.sum(q.astype(F32) * n_ref[d], axis=1, keepdims=True)
    den = jnp.sum(s, axis=1, keepdims=True) + w_st * qn
    o_ref[:, v_cols] = num / jnp.maximum(jnp.abs(den), jnp.exp(-m_t))
    b_end = jnp.sum(lf, axis=1, keepdims=True)
    d_up = b_end - b_col + ig_col
    m_new = jnp.maximum(b_end + m_prev, jnp.max(d_up, axis=0, keepdims=True))
    w_up = jnp.exp(d_up - m_new)
    a = jnp.exp(b_end + m_prev - m_new)
    kw = k.astype(F32) * w_up
    st_ref[d] = a * st + lax.dot_general(kw.astype(BF16), v, TN_DIMS, preferred_element_type=F32)
    n_ref[d] = a * n_ref[d] + jnp.sum(kw, axis=0, keepdims=True)
    m_ref[d] = m_new


def _mlstm_body(qf, kf, vf, gf, qb, kb, vb, gb, of, ob, st_ref, n_ref, m_ref):
    @pl.when(pl.program_id(2) == 0)
    def _():
        st_ref[...] = jnp.zeros_like(st_ref)
        n_ref[...] = jnp.zeros_like(n_ref)
        m_ref[...] = jnp.zeros_like(m_ref)

    for hd in range(MLSTM_HEADS):
        _mlstm_direction(qf, kf, vf, gf, of, st_ref, n_ref, m_ref, 0, False, hd)
        _mlstm_direction(qb, kb, vb, gb, ob, st_ref, n_ref, m_ref, 1, True, hd)


def _scan_orders(n_chunks, ctx_chunks):
    fwd = lambda j: j
    bwd = lambda j: jnp.where(j < ctx_chunks, ctx_chunks - 1 - j, n_chunks - 1 - (j - ctx_chunks))
    return fwd, bwd


def _mlstm_scan(qk, proj, gates_t, *, batch, seq_all, ctx_len):
    ch = MLSTM_CHUNK
    n_chunks = seq_all // ch
    fwd, bwd = _scan_orders(n_chunks, ctx_len // ch)
    v_first = (A_Q + 2 * A_KV + 2 * B_QK) // B_V_DIM

    nh = MLSTM_HEADS
    groups = B_HEADS // nh

    def specs(order):
        return [pl.BlockSpec((ch, nh * B_QK_DIM), lambda b, h, j: (b * n_chunks + order(j), h)),
                pl.BlockSpec((ch, nh * B_QK_DIM), lambda b, h, j: (b * n_chunks + order(j), groups + h)),
                pl.BlockSpec((ch, nh * B_V_DIM), lambda b, h, j: (b * n_chunks + order(j), v_first // nh + h)),
                pl.BlockSpec((None, nh, 4, ch), lambda b, h, j: (b, h, 0, order(j)))]

    def out_spec(order):
        return pl.BlockSpec((ch, nh * B_V_DIM), lambda b, h, j: (b * n_chunks + order(j), h))

    out = jax.ShapeDtypeStruct((batch * seq_all, B_V), F32)
    return pl.pallas_call(
        _mlstm_body,
        grid=(batch, groups, n_chunks),
        in_specs=specs(fwd) + specs(bwd),
        out_specs=[out_spec(fwd), out_spec(bwd)],
        out_shape=[out, out],
        scratch_shapes=[pltpu.VMEM((2 * nh, B_QK_DIM, B_V_DIM), F32),
                        pltpu.VMEM((2 * nh, 1, B_QK_DIM), F32),
                        pltpu.VMEM((2 * nh, 1, 1), F32)],
        compiler_params=_cp("arbitrary", "arbitrary", "arbitrary"),
        name="mlstm_scan",
    )(qk, qk, proj, gates_t, qk, qk, proj, gates_t)


def _headnorm_gate_body(a_ref, b_ref, gate_ref, g_ref, o_ref, *, head_dim, gate_fn):
    for hd in range(a_ref.shape[1] // head_dim):
        sl = slice(hd * head_dim, (hd + 1) * head_dim)
        y = _rms(a_ref[:, sl] + b_ref[:, sl], g_ref[:, sl])
        o_ref[:, sl] = (y * gate_fn(gate_ref[:, sl])).astype(o_ref.dtype)


def _headnorm_gate(a, b, proj, gate_col0, norm_g, *, head_dim, gate_fn, name):
    rows, width = a.shape
    tc = min(2 * COL_TILE, width)
    return pl.pallas_call(
        functools.partial(_headnorm_gate_body, head_dim=head_dim, gate_fn=gate_fn),
        grid=(rows // ROW_TILE, width // tc),
        in_specs=[pl.BlockSpec((ROW_TILE, tc), lambda i, c: (i, c)),
                  pl.BlockSpec((ROW_TILE, tc), lambda i, c: (i, c)),
                  pl.BlockSpec((ROW_TILE, tc), lambda i, c: (i, gate_col0 // tc + c)),
                  pl.BlockSpec((1, tc), lambda i, c: (0, c))],
        out_specs=pl.BlockSpec((ROW_TILE, tc), lambda i, c: (i, c)),
        out_shape=jax.ShapeDtypeStruct((rows, width), BF16),
        compiler_params=_cp("arbitrary", "arbitrary"),
        name=name,
    )(a, b, proj, norm_g)


def _hgrn_direction(q_ref, i_ref, f_ref, lb_ref, o_ref, st_ref, d, rev):
    rows = q_ref.shape[0]
    ch = HGRN_CHUNK
    lb = lb_ref[d:d + 1, :]
    q = _silu(q_ref[...])
    f = lb + (1.0 - lb) * _sigmoid(f_ref[...])
    kk = 1.0 - f
    r = lax.broadcasted_iota(jnp.int32, (rows, rows), 0)
    c = lax.broadcasted_iota(jnp.int32, (rows, rows), 1)
    visible = (r // ch == c // ch) & ((c >= r) if rev else (c <= r))
    a_cum = jnp.log(f)
    pos = lax.broadcasted_iota(jnp.int32, a_cum.shape, 0) % ch
    sh = 1
    while sh < ch:
        if rev:
            a_cum = a_cum + jnp.where(pos < ch - sh, pltpu.roll(a_cum, rows - sh, axis=0), 0.0)
        else:
            a_cum = a_cum + jnp.where(pos >= sh, pltpu.roll(a_cum, sh, axis=0), 0.0)
        sh *= 2
    n_chunks = rows // ch
    end_row = 0 if rev else ch - 1
    mid_row = ch // 2
    width = a_cum.shape[1]

    def per_chunk_row(row):
        return jnp.concatenate([jnp.broadcast_to(a_cum[ci * ch + row:ci * ch + row + 1, :], (ch, width))
                                for ci in range(n_chunks)], axis=0)

    a_mid = per_chunk_row(mid_row)
    a_end = per_chunk_row(end_row)
    v = i_ref[...].astype(BF16)
    q_in = (q * jnp.exp(a_cum)).astype(BF16)
    qh = (q * jnp.exp(a_cum - a_mid)).astype(BF16)
    kh = (kk * jnp.exp(a_mid - a_cum)).astype(BF16)
    ke = (kk * jnp.exp(a_end - a_cum)).astype(BF16)
    order = range(n_chunks - 1, -1, -1) if rev else range(n_chunks)
    for hd in range(width // C_HEAD_DIM):
        cs = slice(hd * C_HEAD_DIM, (hd + 1) * C_HEAD_DIM)
        att = jnp.where(visible, lax.dot_general(qh[:, cs], kh[:, cs], NT_DIMS, preferred_element_type=F32), 0.0)
        o_local = jnp.dot(att.astype(BF16), v[:, cs], preferred_element_type=F32)
        updates = {ci: lax.dot_general(v[ci * ch:(ci + 1) * ch, cs], ke[ci * ch:(ci + 1) * ch, cs], TN_DIMS,
                                       preferred_element_type=F32) for ci in order}
        st = st_ref[d, hd]
        for ci in order:
            rs = slice(ci * ch, (ci + 1) * ch)
            o_ref[rs, cs] = o_local[rs, :] + lax.dot_general(q_in[rs, cs], st.astype(BF16), NT_DIMS,
                                                              preferred_element_type=F32)
            st = jnp.exp(a_cum[ci * ch + end_row:ci * ch + end_row + 1, cs]) * st + updates[ci]
        st_ref[d, hd] = st


def _hgrn_body(qf, vf, ff, qb, vb, fb, lb_ref, of, ob, st_ref):
    @pl.when(pl.program_id(2) == 0)
    def _():
        st_ref[...] = jnp.zeros_like(st_ref)

    _hgrn_direction(qf, vf, ff, lb_ref, of, st_ref, 0, False)
    _hgrn_direction(qb, vb, fb, lb_ref, ob, st_ref, 1, True)


def _hgrn_scan(proj, lb, *, batch, seq_all, ctx_len, d_model):
    blk = HGRN_BLOCK
    n_blocks = seq_all // blk
    fwd, bwd = _scan_orders(n_blocks, ctx_len // blk)
    heads = min(HGRN_HEADS, d_model // C_HEAD_DIM)
    wcols = heads * C_HEAD_DIM
    per = d_model // wcols

    def spec(order, part):
        return pl.BlockSpec((blk, wcols), lambda b, h, j: (b * n_blocks + order(j), part * per + h))

    def out_spec(order):
        return pl.BlockSpec((blk, wcols), lambda b, h, j: (b * n_blocks + order(j), h))

    out = jax.ShapeDtypeStruct((batch * seq_all, d_model), F32)
    return pl.pallas_call(
        _hgrn_body,
        grid=(batch, per, n_blocks),
        in_specs=[spec(fwd, 0), spec(fwd, 1), spec(fwd, 3), spec(bwd, 0), spec(bwd, 1), spec(bwd, 4),
                  pl.BlockSpec((2, wcols), lambda b, h, j: (0, h))],
        out_specs=[out_spec(fwd), out_spec(bwd)],
        out_shape=[out, out],
        scratch_shapes=[pltpu.VMEM((2, heads, C_HEAD_DIM, C_HEAD_DIM), F32)],
        compiler_params=_cp("arbitrary", "arbitrary", "arbitrary"),
        name="hgrn_scan",
    )(proj, proj, proj, proj, proj, proj, lb)


def _pack_halves(v):
    half = v.shape[1] // 2
    bits = lax.bitcast_convert_type(v.astype(BF16).astype(F32), jnp.uint32)
    return (bits[:, :half] >> 16) | (bits[:, half:] & jnp.uint32(0xFFFF0000))


def _unpack_halves(p):
    lo = lax.bitcast_convert_type(p << 16, F32)
    hi = lax.bitcast_convert_type(p & jnp.uint32(0xFFFF0000), F32)
    return lo, hi


def _ffn_router_body(x_ref, g_ref, mod_ref, rwt_ref, rb_ref, hb_ref, hp_ref, idx_ref, wgt_ref, rank_ref, cnt_ref):
    @pl.when(pl.program_id(0) == 0)
    def _():
        cnt_ref[...] = jnp.zeros_like(cnt_ref)

    y = _rms(x_ref[...], g_ref[...])
    h2 = y * (1.0 + mod_ref[4:5, :]) + mod_ref[3:4, :]
    hb = h2.astype(BF16)
    hb_ref[...] = hb
    hp_ref[...] = _pack_halves(hb)
    tm = h2.shape[0]
    h_lo = (h2 - hb.astype(F32)).astype(BF16)
    rw = rwt_ref[...]
    rw_hi = rw.astype(BF16)
    rw_lo = (rw - rw_hi.astype(F32)).astype(BF16)
    logits = (lax.dot_general(rw_hi, hb, NT_DIMS, preferred_element_type=F32)
              + lax.dot_general(rw_hi, h_lo, NT_DIMS, preferred_element_type=F32)
              + lax.dot_general(rw_lo, hb, NT_DIMS, preferred_element_type=F32))
    s = _sigmoid(logits)
    sel = s + rb_ref[...]
    iota_g = lax.broadcasted_iota(jnp.int32, (GROUP_SIZE, tm), 0)
    group_scores = []
    for g in range(N_GROUPS):
        xg = sel[g * GROUP_SIZE:(g + 1) * GROUP_SIZE, :]
        m1 = jnp.max(xg, axis=0, keepdims=True)
        i1 = jnp.min(jnp.where(xg == m1, iota_g, GROUP_SIZE), axis=0, keepdims=True)
        m2 = jnp.max(jnp.where(iota_g == i1, NEG_INF, xg), axis=0, keepdims=True)
        group_scores.append(m1 + m2)
    gsc = jnp.concatenate(group_scores, axis=0)
    iota_n = lax.broadcasted_iota(jnp.int32, (N_GROUPS, tm), 0)
    gmask = jnp.zeros((N_GROUPS, tm), F32)
    for _ in range(TOPK_GROUPS):
        m = jnp.max(gsc, axis=0, keepdims=True)
        i = jnp.min(jnp.where(gsc == m, iota_n, N_GROUPS), axis=0, keepdims=True)
        hit = iota_n == i
        gmask = jnp.where(hit, 1.0, gmask)
        gsc = jnp.where(hit, NEG_INF, gsc)
    emask = jnp.concatenate([jnp.broadcast_to(gmask[g:g + 1, :], (GROUP_SIZE, tm)) for g in range(N_GROUPS)], axis=0)
    cur = jnp.where(emask > 0.0, sel, NEG_INF)
    iota_e = lax.broadcasted_iota(jnp.int32, (N_EXPERTS, tm), 0)
    idx_rows, w_rows, hits = [], [], []
    for _ in range(TOP_K):
        m = jnp.max(cur, axis=0, keepdims=True)
        i = jnp.min(jnp.where(cur == m, iota_e, N_EXPERTS), axis=0, keepdims=True)
        hit = iota_e == i
        idx_rows.append(i)
        hits.append(hit)
        w_rows.append(jnp.sum(jnp.where(hit, s, 0.0), axis=0, keepdims=True))
        cur = jnp.where(hit, NEG_INF, cur)
    w = jnp.concatenate(w_rows, axis=0)
    idx_ref[...] = jnp.concatenate(idx_rows, axis=0)
    wgt_ref[...] = ROUTED_SCALE * w / jnp.sum(w, axis=0, keepdims=True)
    chosen = jnp.zeros((N_EXPERTS, tm), F32)
    for hit in hits:
        chosen = jnp.where(hit, 1.0, chosen)
    earlier = (lax.broadcasted_iota(jnp.int32, (tm, tm), 0) < lax.broadcasted_iota(jnp.int32, (tm, tm), 1))
    before = cnt_ref[...] + jnp.dot(chosen.astype(BF16), earlier.astype(BF16), preferred_element_type=F32)
    rank_ref[...] = jnp.concatenate(
        [jnp.sum(jnp.where(hit, before, 0.0), axis=0, keepdims=True) for hit in hits], axis=0).astype(jnp.int32)
    cnt_ref[...] = cnt_ref[...] + jnp.sum(chosen, axis=1, keepdims=True)


def _ffn_router(x, g, mod, sel, router_wt, router_b, layer):
    rows, d = x.shape
    return pl.pallas_call(
        _ffn_router_body,
        grid=(rows // ROW_TILE,),
        in_specs=[pl.BlockSpec((ROW_TILE, d), lambda i: (i, 0)),
                  pl.BlockSpec((1, d), lambda i: (0, 0)),
                  pl.BlockSpec((None, 6, d), lambda i: (sel(i), 0, 0)),
                  pl.BlockSpec((None, N_EXPERTS, d), lambda i: (layer, 0, 0)),
                  pl.BlockSpec((None, N_EXPERTS, 1), lambda i: (layer, 0, 0))],
        out_specs=[pl.BlockSpec((ROW_TILE, d), lambda i: (i, 0)),
                   pl.BlockSpec((ROW_TILE, d // 2), lambda i: (i, 0)),
                   pl.BlockSpec((TOP_K, ROW_TILE), lambda i: (0, i)),
                   pl.BlockSpec((TOP_K, ROW_TILE), lambda i: (0, i)),
                   pl.BlockSpec((TOP_K, ROW_TILE), lambda i: (0, i)),
                   pl.BlockSpec((N_EXPERTS, 1), lambda i: (0, 0))],
        out_shape=[jax.ShapeDtypeStruct((rows, d), BF16),
                   jax.ShapeDtypeStruct((rows, d // 2), jnp.uint32),
                   jax.ShapeDtypeStruct((TOP_K, rows), jnp.int32),
                   jax.ShapeDtypeStruct((TOP_K, rows), F32),
                   jax.ShapeDtypeStruct((TOP_K, rows), jnp.int32),
                   jax.ShapeDtypeStruct((N_EXPERTS, 1), F32)],
        compiler_params=_cp("arbitrary"),
        name=f"ffn_router{layer}",
    )(x, g, mod, router_wt, router_b)


def _dispatch(idx_t, rank_t, counts):
    tm = MOE_TILE
    slots = idx_t.shape[0] * idx_t.shape[1]
    n_tiles = slots // tm
    counts = counts.reshape(-1).astype(jnp.int32)
    ends = jnp.cumsum(counts)
    starts = ends - counts
    is_e = idx_t[:, :, None] == jnp.arange(N_EXPERTS, dtype=jnp.int32)
    pos = (jnp.sum(jnp.where(is_e, starts, 0), axis=-1) + rank_t).T.reshape(-1)
    cuts = jnp.sort(jnp.concatenate([jnp.arange(n_tiles, dtype=jnp.int32) * tm, ends[:-1]]))
    nxt = jnp.concatenate([cuts[1:], jnp.full((1,), slots, jnp.int32)])
    v_tile = jnp.minimum(cuts // tm, n_tiles - 1)
    v_exp = jnp.minimum(jnp.sum((ends[None, :] <= cuts[:, None]).astype(jnp.int32), axis=1), N_EXPERTS - 1)
    v_lo = cuts - v_tile * tm
    v_hi = nxt - v_tile * tm
    n_vis = cuts.shape[0]
    v_first = (v_exp != jnp.concatenate([jnp.full((1,), -1, jnp.int32), v_exp[:-1]])).astype(jnp.int32)
    order = jnp.arange(n_vis, dtype=jnp.int32)
    next_first = lax.cummin(jnp.where(v_first == 1, order, n_vis)[::-1])[::-1]
    after = jnp.concatenate([next_first[1:], jnp.full((1,), n_vis, jnp.int32)])
    v_next = jnp.where(after < n_vis, v_exp[jnp.minimum(after, n_vis - 1)], -1)
    return (pos.astype(jnp.int32), v_tile.astype(jnp.int32), v_exp.astype(jnp.int32), v_lo, v_hi, v_first,
            v_next.astype(jnp.int32))


def _push_body(pos_ref, h_ref, xs_hbm, sem):
    i = pl.program_id(0)
    tt = h_ref.shape[0]
    base = i * tt * TOP_K

    def issue(r, carry):
        src = h_ref.at[pl.ds(r, 1)]
        for k in range(TOP_K):
            pltpu.make_async_copy(src, xs_hbm.at[pl.ds(pos_ref[base + r * TOP_K + k], 1)], sem).start(
                priority=k % DMA_PRIORITIES)
        return carry

    lax.fori_loop(0, tt, issue, 0)
    for _ in range(TOP_K):
        pltpu.make_async_copy(h_ref, xs_hbm.at[pl.ds(0, tt)], sem).wait()


def _push(h2f, pos, layer):
    rows, d = h2f.shape
    tt = PUSH_TOKENS
    grid_spec = pltpu.PrefetchScalarGridSpec(
        num_scalar_prefetch=1,
        grid=(rows // tt,),
        in_specs=[pl.BlockSpec((tt, d), lambda i, ps: (i, 0))],
        out_specs=pl.BlockSpec(memory_space=pl.ANY),
        scratch_shapes=[pltpu.SemaphoreType.DMA(())],
    )
    return pl.pallas_call(
        _push_body,
        grid_spec=grid_spec,
        out_shape=jax.ShapeDtypeStruct((rows * TOP_K, d), h2f.dtype),
        compiler_params=_cp("arbitrary"),
        name=f"push{layer}",
    )(pos, h2f)


def _expert_body(vt_ref, ve_ref, lo_ref, hi_ref, first_ref, next_ref, x_ref, wg_hbm, wu_hbm, wd_hbm, y_ref,
                 wgf, wuf, wdf, wgb, wub, wdb, sem, *, layer):
    v = pl.program_id(0)
    lo = lo_ref[v]
    hi = hi_ref[v]

    def weight_copies(e):
        return (pltpu.make_async_copy(wg_hbm.at[layer, e], wgf, sem),
                pltpu.make_async_copy(wu_hbm.at[layer, e], wuf, sem),
                pltpu.make_async_copy(wd_hbm.at[layer, e], wdf, sem))

    @pl.when(v == 0)
    def _():
        for cp in weight_copies(ve_ref[0]):
            cp.start()

    @pl.when(first_ref[v] == 1)
    def _():
        for cp in weight_copies(ve_ref[v]):
            cp.wait()
        wgb[...] = wgf[...].astype(BF16)
        wub[...] = wuf[...].astype(BF16)
        wdb[...] = wdf[...].astype(BF16)

        @pl.when(next_ref[v] >= 0)
        def _():
            for cp in weight_copies(next_ref[v]):
                cp.start()

    @pl.when(hi > lo)
    def _():
        x_lo, x_hi = _unpack_halves(x_ref[...])
        x_lo = x_lo.astype(BF16)
        x_hi = x_hi.astype(BF16)
        half = x_lo.shape[1]
        g = (jnp.dot(x_lo, wgb[:half, :], preferred_element_type=F32)
             + jnp.dot(x_hi, wgb[half:, :], preferred_element_type=F32))
        u = (jnp.dot(x_lo, wub[:half, :], preferred_element_type=F32)
             + jnp.dot(x_hi, wub[half:, :], preferred_element_type=F32))
        hid = (_silu(g) * u).astype(BF16)
        y = _pack_halves(jnp.dot(hid, wdb[...], preferred_element_type=F32))
        r = lax.broadcasted_iota(jnp.int32, (y.shape[0], 1), 0)
        mine = (r >= lo) & (r < hi)

        @pl.when(lo == 0)
        def _():
            y_ref[...] = jnp.where(mine, y, jnp.uint32(0))

        @pl.when(lo > 0)
        def _():
            y_ref[...] = jnp.where(mine, y, y_ref[...])


def _experts(xs, visits, wg, wu, wd, layer):
    slots, dp = xs.shape
    d = wg.shape[2]
    de = wg.shape[3]
    tm = MOE_TILE
    hbm = pl.BlockSpec(memory_space=pl.ANY)
    grid_spec = pltpu.PrefetchScalarGridSpec(
        num_scalar_prefetch=len(visits),
        grid=(visits[0].shape[0],),
        in_specs=[pl.BlockSpec((tm, dp), lambda v, vt, *_: (vt[v], 0)), hbm, hbm, hbm],
        out_specs=pl.BlockSpec((tm, dp), lambda v, vt, *_: (vt[v], 0)),
        scratch_shapes=[pltpu.VMEM((d, de), F32), pltpu.VMEM((d, de), F32), pltpu.VMEM((de, d), F32),
                        pltpu.VMEM((d, de), BF16), pltpu.VMEM((d, de), BF16), pltpu.VMEM((de, d), BF16),
                        pltpu.SemaphoreType.DMA(())],
    )
    return pl.pallas_call(
        functools.partial(_expert_body, layer=layer),
        grid_spec=grid_spec,
        out_shape=jax.ShapeDtypeStruct((slots, dp), jnp.uint32),
        compiler_params=_cp("arbitrary"),
        name=f"experts{layer}",
    )(*visits, xs, wg, wu, wd)


def _combine_body(pos_ref, w_ref, y_hbm, o_ref, buf, sem):
    i = pl.program_id(0)
    tt = o_ref.shape[0]
    n = TOP_K * tt
    slot = i % 2
    half = buf.shape[3]
    group = 8

    def issue_rows(step, into, r0, rows):
        for j in range(rows):
            for k in range(TOP_K):
                pltpu.make_async_copy(y_hbm.at[pl.ds(pos_ref[step * n + (r0 + j) * TOP_K + k], 1)],
                                      buf.at[into, k, pl.ds(r0 + j, 1)], sem.at[into]).start(
                    priority=k % DMA_PRIORITIES)

    @pl.when(i == 0)
    def _():
        def first(g, carry):
            issue_rows(0, 0, g * group, group)
            return carry

        lax.fori_loop(0, tt // group, first, 0)

    for k in range(TOP_K):
        pltpu.make_async_copy(y_hbm.at[pl.ds(0, tt)], buf.at[slot, k], sem.at[slot]).wait()
    last = pl.num_programs(0) - 1
    nxt = jnp.minimum(i + 1, last)

    def body(g, carry):
        r0 = pl.multiple_of(g * group, group)
        issue_rows(nxt, 1 - slot, r0, group)
        w = w_ref[pl.ds(r0, group), :]
        acc_lo, acc_hi = _unpack_halves(buf[slot, 0, pl.ds(r0, group), :])
        acc_lo = w[:, 0:1] * acc_lo
        acc_hi = w[:, 0:1] * acc_hi
        for k in range(1, TOP_K):
            y_lo, y_hi = _unpack_halves(buf[slot, k, pl.ds(r0, group), :])
            acc_lo = acc_lo + w[:, k:k + 1] * y_lo
            acc_hi = acc_hi + w[:, k:k + 1] * y_hi
        o_ref[pl.ds(r0, group), :half] = acc_lo
        o_ref[pl.ds(r0, group), half:] = acc_hi
        return carry

    lax.fori_loop(0, tt // group, body, 0)

    @pl.when(i == last)
    def _():
        for k in range(TOP_K):
            pltpu.make_async_copy(y_hbm.at[pl.ds(0, tt)], buf.at[1 - slot, k], sem.at[1 - slot]).wait()


def _combine(ys, pos_flat, wgt, rows, layer):
    dp = ys.shape[1]
    d = 2 * dp
    tt = COMBINE_TOKENS
    grid_spec = pltpu.PrefetchScalarGridSpec(
        num_scalar_prefetch=1,
        grid=(rows // tt,),
        in_specs=[pl.BlockSpec((tt, TOP_K), lambda i, ps: (i, 0)),
                  pl.BlockSpec(memory_space=pl.ANY)],
        out_specs=pl.BlockSpec((tt, d), lambda i, ps: (i, 0)),
        scratch_shapes=[pltpu.VMEM((2, TOP_K, tt, dp), jnp.uint32), pltpu.SemaphoreType.DMA((2,))],
    )
    return pl.pallas_call(
        _combine_body,
        grid_spec=grid_spec,
        out_shape=jax.ShapeDtypeStruct((rows, d), F32),
        compiler_params=_cp("arbitrary"),
        name=f"combine{layer}",
    )(pos_flat, wgt, ys)


def _moe(x1, mod, sel, layer, norm_g, router_w, router_b, exp_wg, exp_wu, exp_wd, s_gate, s_up, s_down, tm_big):
    rows, d = x1.shape
    router_wt = jnp.swapaxes(router_w, 1, 2)
    h2b, h2p, idx_t, wgt_t, rank_t, counts = _ffn_router(x1, norm_g[layer][None], mod, sel, router_wt,
                                                         router_b[:, :, None], layer)
    pos, *visits = _dispatch(idx_t, rank_t, counts)
    xs = _push(h2p, pos, layer)
    ys = _experts(xs, visits, exp_wg, exp_wu, exp_wd, layer)
    routed = _combine(ys, pos, wgt_t.T, rows, layer)
    hs = _glu(h2b, s_gate, s_up, layer, tm=tm_big, name=f"shared_glu{layer}")
    return _linear([hs], s_down, layer, n_cols=d, tm=ROW_TILE, out_dtype=F32, name=f"shared_down{layer}",
                   res=(x1, mod, 5, routed, sel), tn=4 * COL_TILE)


def _rope_tables(ctx_len, lat_len):
    rows = lat_len // GRID_W
    pos_r = np.repeat(np.arange(rows), GRID_W).astype(np.float32)
    pos_c = np.tile(np.arange(GRID_W), rows).astype(np.float32)
    n = A_HEAD_DIM // 4
    inv = (np.float32(ROPE_BASE) ** (-np.arange(n, dtype=np.float32) / np.float32(n))).astype(np.float32)
    ang = np.concatenate([pos_r[:, None] * inv, pos_c[:, None] * inv], axis=-1).astype(np.float32)
    cos, sin = np.cos(ang).astype(np.float32), np.sin(ang).astype(np.float32)
    cos_t = np.concatenate([np.ones((ctx_len, A_HEAD_DIM), np.float32), np.concatenate([cos, cos], axis=-1)], axis=0)
    sin_t = np.concatenate([np.zeros((ctx_len, A_HEAD_DIM), np.float32), np.concatenate([-sin, sin], axis=-1)], axis=0)
    return jnp.asarray(cos_t), jnp.asarray(sin_t)


def kernel(x, c, ctx, c_ctx, ada_w, ada_b, norm_mix_g, norm_ffn_g, ab_w_in, ab_w_out, a_q_norm_g, a_k_norm_g,
           a_sink, b_conv_w, b_gate_b, b_norm_g, c_w_in, c_w_out, c_lb_logits, c_norm_g, router_w, router_b,
           exp_w_gate, exp_w_up, exp_w_down, shared_w_gate, shared_w_up, shared_w_down):
    batch, lat_len, d = x.shape
    ctx_len = ctx.shape[1]
    depth = ada_w.shape[0]
    seq_all = ctx_len + lat_len
    rows = batch * seq_all
    assert ctx_len % ROW_TILE == 0 and lat_len % ROW_TILE == 0 and ctx_len % MLSTM_CHUNK == 0
    assert lat_len % GRID_W == 0 and d % (min(HGRN_HEADS, d // C_HEAD_DIM) * C_HEAD_DIM) == 0
    tiles_per_sample = seq_all // ROW_TILE
    ctx_tiles = ctx_len // ROW_TILE
    tm_big = _row_tile_size(rows, WIDE_ROWS_MAX)
    tm_in = _row_tile_size(rows, WIDE_ROWS_MAX // 2)

    def sel(i):
        return jnp.where(i % tiles_per_sample < ctx_tiles, batch, i // tiles_per_sample)

    cos_t, sin_t = _rope_tables(ctx_len, lat_len)
    lb_soft = jax.nn.softmax(c_lb_logits.astype(F32), axis=0)
    lower_bounds = jnp.cumsum(lb_soft, axis=0) - lb_soft[0:1]
    cond = jnp.concatenate([jax.nn.silu(c), jax.nn.silu(c_ctx)[None]], axis=0)
    cond = jnp.pad(cond, ((0, 16 - cond.shape[0]), (0, 0))).astype(BF16)

    lat_tiles = lat_len // ROW_TILE

    def lat_map(i):
        return (i // lat_tiles) * tiles_per_sample + ctx_tiles + i % lat_tiles

    if depth > 1:
        xa = _SplitRows(
            ctx.reshape(batch * ctx_len, d), x.reshape(batch * lat_len, d),
            lambda i: (i // tiles_per_sample) * ctx_tiles + jnp.minimum(i % tiles_per_sample, ctx_tiles - 1),
            lambda i: (i // tiles_per_sample) * lat_tiles + jnp.clip(i % tiles_per_sample - ctx_tiles, 0,
                                                                    lat_tiles - 1),
            lambda i: i % tiles_per_sample < ctx_tiles)
    else:
        xa = jnp.concatenate([ctx, x], axis=1).reshape(rows, d)
    for layer in range(depth):
        last = layer == depth - 1
        tail = dict(row_map=lat_map, out_rows=batch * lat_len) if last else {}
        tail_sel = (lambda i: i // lat_tiles) if last else sel
        mod = _ada(cond, ada_w, ada_b, layer)[:batch + 1].reshape(batch + 1, 6, d)
        h = _modulate(xa, norm_mix_g[layer][None], mod, sel, shift_row=0, scale_row=1, name=f"mod_mix{layer}")
        if layer % 2 == 0:
            e = layer // 2
            w_in_t = jnp.swapaxes(ab_w_in, 1, 2)
            proj = _linear([h], w_in_t, e, n_cols=AB_MAIN, tm=tm_in, out_dtype=F32, name=f"ab_in{layer}",
                           tn=2 * COL_TILE, w_t=True)
            w_gates_t = jnp.pad(w_in_t[e, AB_MAIN:, :], ((0, LANE_BLOCK - AB_GATES), (0, 0)))[None]
            graw = _linear([h], w_gates_t, 0, n_cols=LANE_BLOCK, tm=tm_big, out_dtype=F32, name=f"ab_gates{layer}",
                           w_t=True)
            qn, kn, vb = _attn_prep(proj, cos_t, sin_t, a_q_norm_g[e][None], a_k_norm_g[e][None], tiles_per_sample)
            ya = _attention(qn, kn, vb, a_sink[e], batch=batch, seq_all=seq_all, ctx_len=ctx_len)
            qk = _mlstm_conv(proj, b_conv_w[e], batch=batch, seq_all=seq_all, ctx_len=ctx_len)
            gate_b = jnp.pad(b_gate_b[e], (0, LANE_BLOCK - AB_GATES))[None]
            gates = _mlstm_gates(graw, gate_b)[:, :AB_GATES]
            gates_t = gates.reshape(batch, seq_all, 4, B_HEADS).transpose(0, 3, 2, 1)
            hf, hb = _mlstm_scan(qk, proj, gates_t, batch=batch, seq_all=seq_all, ctx_len=ctx_len)
            yb = _headnorm_gate(hf, hb, proj, AB_MAIN - B_V, b_norm_g[e][None], head_dim=B_V_DIM,
                                gate_fn=_sigmoid, name="mlstm_out")
            xa = _linear([ya, yb], ab_w_out, e, n_cols=d, tm=ROW_TILE, out_dtype=F32, name=f"ab_out{layer}",
                         res=(xa, mod, 2, None, tail_sel), tn=2 * COL_TILE, **tail)
        else:
            o = layer // 2
            proj = _linear([h], c_w_in, o, n_cols=5 * d, tm=tm_in, out_dtype=F32, name=f"c_in{layer}",
                           tn=2 * COL_TILE)
            of, ob = _hgrn_scan(proj, lower_bounds[layer], batch=batch, seq_all=seq_all, ctx_len=ctx_len, d_model=d)
            yc = _headnorm_gate(of, ob, proj, 2 * d, c_norm_g[o][None], head_dim=C_HEAD_DIM, gate_fn=_silu,
                                name="hgrn_out")
            xa = _linear([yc], c_w_out, o, n_cols=d, tm=ROW_TILE, out_dtype=F32, name=f"c_out{layer}",
                         res=(xa, mod, 2, None, tail_sel), tn=2 * COL_TILE, **tail)
        xa = _moe(xa, mod, tail_sel, layer, norm_ffn_g, router_w, router_b, exp_w_gate, exp_w_up, exp_w_down,
                  shared_w_gate, shared_w_up, shared_w_down, _row_tile_size(xa.shape[0], WIDE_ROWS_MAX))
    return xa.reshape(batch, lat_len, d)
```
